```python
import math
import jax, jax.numpy as jnp
from jax import lax
import numpy as np

D_MODEL = 2048
BATCH = 4
SEQ = 2048
DEPTH = 2

N_EVEN = (DEPTH + 1) // 2
N_ODD = DEPTH // 2
EPS = 1e-6

A_HEADS = 8
A_QK_DIM = 64
A_V_DIM = 2 * A_QK_DIM
A_QK_WIDTH = A_HEADS * 2 * A_QK_DIM
A_WIDTH = A_HEADS * A_V_DIM
ROPE_THETA = 500000.0
ROPE_DIM = A_QK_DIM // 4
Q_BLOCK = 128

B_GROUPS = 8
B_GROUP_DIM = 128
B_WIDTH = B_GROUPS * B_GROUP_DIM
B_CHUNK = 128

EVEN_IN = 2 * A_QK_WIDTH + A_WIDTH + 2 * B_WIDTH
EVEN_OUT = A_WIDTH + B_WIDTH

C_EXPAND = 128
C_HEADS = D_MODEL // C_EXPAND
C_DK = C_EXPAND
C_DV = D_MODEL // C_HEADS
C_WIDTH = C_HEADS * C_DK
C_CHUNK = 64
ODD_IN = 5 * C_WIDTH

FFN_HIDDEN = ((8 * D_MODEL // 3 + 255) // 256) * 256

kernel_name = "hybrid_diffattn_gmlp_hgrn2_encoder"


def rms_norm(x, g):
    xf = x.astype(jnp.float32)
    y = xf * lax.rsqrt(jnp.mean(xf * xf, axis=-1, keepdims=True) + EPS)
    return (y * g.astype(jnp.float32)).astype(x.dtype)


def layer_norm(x, g, b):
    xf = x.astype(jnp.float32)
    mu = jnp.mean(xf, axis=-1, keepdims=True)
    xc = xf - mu
    var = jnp.mean(xc * xc, axis=-1, keepdims=True)
    y = xc * lax.rsqrt(var + EPS) * g.astype(jnp.float32) + b.astype(jnp.float32)
    return y.astype(x.dtype)


def partial_rope(x, pos):
    half = ROPE_DIM // 2
    inv_freq = ROPE_THETA ** (-jnp.arange(half, dtype=jnp.float32) / half)
    ang = pos[:, None] * inv_freq[None, :]
    cos = jnp.cos(ang).astype(x.dtype)
    sin = jnp.sin(ang).astype(x.dtype)
    x1 = x[..., :half]
    x2 = x[..., half:ROPE_DIM]
    rest = x[..., ROPE_DIM:]
    return jnp.concatenate([x1 * cos - x2 * sin, x2 * cos + x1 * sin, rest], axis=-1)


def diff_attention(q, k, v, lam, pos):
    B, S = q.shape[0], q.shape[1]
    q = partial_rope(jnp.einsum('bshcd->bhcsd', q), pos)
    k = partial_rope(jnp.einsum('bshcd->bhcsd', k), pos)
    v = jnp.einsum('bshd->bhsd', v)
    scale = A_QK_DIM ** -0.5
    nb = S // Q_BLOCK
    qb = jnp.moveaxis(q.reshape(B, A_HEADS, 2, nb, Q_BLOCK, A_QK_DIM), 3, 0)

    def block(qi):
        s = jnp.einsum('bhcqd,bhckd->bhcqk', qi, k).astype(jnp.float32) * scale
        p = jax.nn.softmax(s, axis=-1)
        a = p[:, :, 0] - lam * p[:, :, 1]
        return jnp.einsum('bhqk,bhkd->bhqd', a.astype(v.dtype), v)

    o = lax.map(block, qb)
    return jnp.transpose(o, (1, 0, 3, 2, 4)).reshape(B, S, A_HEADS, A_V_DIM)


def even_mixer(h, w_in, w_out, lq1, lk1, lq2, lk2, subln, ln_g, ln_b, w_s, b_s,
               layer_idx, pos):
    B, S, _ = h.shape
    proj = h @ w_in
    s1 = A_QK_WIDTH
    s2 = 2 * A_QK_WIDTH
    s3 = s2 + A_WIDTH
    s4 = s3 + B_WIDTH
    q, k, va, u, vb = jnp.split(proj, [s1, s2, s3, s4], axis=-1)

    q = q.reshape(B, S, A_HEADS, 2, A_QK_DIM)
    k = k.reshape(B, S, A_HEADS, 2, A_QK_DIM)
    va = va.reshape(B, S, A_HEADS, A_V_DIM)
    lam_init = 0.8 - 0.6 * math.exp(-0.3 * layer_idx)
    lam = (jnp.exp(jnp.sum(lq1.astype(jnp.float32) * lk1.astype(jnp.float32)))
           - jnp.exp(jnp.sum(lq2.astype(jnp.float32) * lk2.astype(jnp.float32)))
           + lam_init)
    oa = diff_attention(q, k, va, lam, pos)
    oa = (rms_norm(oa, subln) * (1.0 - lam_init)).reshape(B, S, A_WIDTH)

    u = jax.nn.gelu(u)
    vb = layer_norm(jax.nn.gelu(vb), ln_g, ln_b)
    nc = S // B_CHUNK
    vb = vb.reshape(B, nc, B_CHUNK, B_GROUPS, B_GROUP_DIM)
    sv = jnp.einsum('gpq,bnqgc->bnpgc', w_s, vb) + jnp.transpose(b_s)[None, None, :, :, None]
    ob = u * sv.reshape(B, S, B_WIDTH)

    return jnp.concatenate([oa, ob], axis=-1) @ w_out


def hgrn2_scan(q, k, v, logf):
    D2, B, H, S, dk = q.shape
    dv = v.shape[-1]
    nc = S // C_CHUNK

    def to_chunks(t):
        return jnp.moveaxis(t.reshape(D2, B, H, nc, C_CHUNK, t.shape[-1]), 3, 0)

    xs = (to_chunks(q), to_chunks(k), to_chunks(v), to_chunks(logf))
    mask = jnp.tril(jnp.ones((C_CHUNK, C_CHUNK), dtype=bool))[:, :, None]

    def step(state, inp):
        qi, ki, vi, gi = inp
        b = jnp.cumsum(gi, axis=-2)
        diff = b[..., :, None, :] - b[..., None, :, :]
        decay = jnp.exp(jnp.where(mask, diff, -jnp.inf))
        scores = jnp.einsum('...tk,...tsk->...ts', qi, decay * ki[..., None, :, :])
        o_intra = jnp.einsum('...ts,...sv->...tv', scores, vi)
        o_inter = jnp.einsum('...tk,...kv->...tv', qi * jnp.exp(b), state)
        b_last = b[..., -1:, :]
        k_dec = ki * jnp.exp(b_last - b)
        new_state = (state * jnp.exp(b_last)[..., 0, :, None]
                     + jnp.einsum('...sk,...sv->...kv', k_dec, vi))
        return new_state, o_intra + o_inter

    state0 = jnp.zeros((D2, B, H, dk, dv), jnp.float32)
    _, o = lax.scan(step, state0, xs)
    return jnp.moveaxis(o, 0, 3).reshape(D2, B, H, S, dv)


def odd_mixer(h, w_in, w_out, lower_bounds, g_norm, layer_idx):
    B, S, _ = h.shape
    proj = h @ w_in
    q, f_fwd, f_bwd, i, g = jnp.split(proj, 5, axis=-1)
    lbs = jax.nn.softmax(lower_bounds.astype(jnp.float32), axis=1)
    lb = (jnp.cumsum(lbs, axis=1) - lbs[:, :1])[:, layer_idx]
    lb = lb[:, None, None, :]
    f = lb + (1.0 - lb) * jax.nn.sigmoid(jnp.stack([f_fwd, f_bwd], 0).astype(jnp.float32))
    k = 1.0 - f
    logf = jnp.log(f)
    q = jax.nn.silu(q).astype(jnp.float32)
    i = i.astype(jnp.float32)

    def bidir(fwd, bwd):
        return jnp.stack([fwd, jnp.flip(bwd, axis=1)], axis=0)

    def heads(t, d):
        return jnp.transpose(t.reshape(2, B, S, C_HEADS, d), (0, 1, 3, 2, 4))

    o = hgrn2_scan(heads(bidir(q, q), C_DK), heads(bidir(k[0], k[1]), C_DK),
                   heads(bidir(i, i), C_DV), heads(bidir(logf[0], logf[1]), C_DK))
    o = o[0] + jnp.flip(o[1], axis=2)
    o = jnp.transpose(o, (0, 2, 1, 3)).reshape(B, S, C_HEADS * C_DV).astype(h.dtype)
    o = rms_norm(o, g_norm) * jax.nn.silu(g)
    return o @ w_out


def swiglu(h, w_gate, w_up, w_down):
    return (jax.nn.silu(h @ w_gate) * (h @ w_up)) @ w_down


def setup_inputs(seed: int = 0) -> dict:
    key = jax.random.key(seed)
    ks = jax.random.split(key, 24)
    f32 = jnp.float32

    def nrm(k, shape, scale):
        return jax.random.normal(k, shape, f32) * scale

    def gain(k, shape):
        return 1.0 + 0.02 * jax.random.normal(k, shape, f32)

    return {
        "x": jax.random.normal(ks[0], (BATCH, SEQ, D_MODEL), f32),
        "mix_norm": gain(ks[1], (DEPTH, D_MODEL)),
        "even_w_in": nrm(ks[2], (N_EVEN, D_MODEL, EVEN_IN), D_MODEL ** -0.5),
        "even_w_out": nrm(ks[3], (N_EVEN, EVEN_OUT, D_MODEL), EVEN_OUT ** -0.5),
        "diff_lq1": nrm(ks[4], (N_EVEN, A_QK_DIM), 0.1),
        "diff_lk1": nrm(ks[5], (N_EVEN, A_QK_DIM), 0.1),
        "diff_lq2": nrm(ks[6], (N_EVEN, A_QK_DIM), 0.1),
        "diff_lk2": nrm(ks[7], (N_EVEN, A_QK_DIM), 0.1),
        "diff_subln": gain(ks[8], (N_EVEN, A_V_DIM)),
        "gmlp_ln_g": gain(ks[9], (N_EVEN, B_WIDTH)),
        "gmlp_ln_b": nrm(ks[10], (N_EVEN, B_WIDTH), 0.02),
        "gmlp_w_s": nrm(ks[11], (N_EVEN, B_GROUPS, B_CHUNK, B_CHUNK), B_CHUNK ** -0.5),
        "gmlp_b_s": gain(ks[12], (N_EVEN, B_GROUPS, B_CHUNK)),
        "hgrn_w_in": nrm(ks[13], (N_ODD, D_MODEL, ODD_IN), D_MODEL ** -0.5),
        "hgrn_w_out": nrm(ks[14], (N_ODD, C_WIDTH, D_MODEL), C_WIDTH ** -0.5),
        "hgrn_lower_bounds": nrm(ks[15], (2, DEPTH, C_WIDTH), 0.1),
        "hgrn_g_norm": gain(ks[16], (N_ODD, C_WIDTH)),
        "ffn_norm": gain(ks[17], (DEPTH, D_MODEL)),
        "ffn_w_gate": nrm(ks[18], (DEPTH, D_MODEL, FFN_HIDDEN), D_MODEL ** -0.5),
        "ffn_w_up": nrm(ks[19], (DEPTH, D_MODEL, FFN_HIDDEN), D_MODEL ** -0.5),
        "ffn_w_down": nrm(ks[20], (DEPTH, FFN_HIDDEN, D_MODEL), FFN_HIDDEN ** -0.5),
        "final_norm": gain(ks[21], (D_MODEL,)),
    }


def reference(x, mix_norm, even_w_in, even_w_out, diff_lq1, diff_lk1, diff_lq2,
              diff_lk2, diff_subln, gmlp_ln_g, gmlp_ln_b, gmlp_w_s, gmlp_b_s,
              hgrn_w_in, hgrn_w_out, hgrn_lower_bounds, hgrn_g_norm, ffn_norm,
              ffn_w_gate, ffn_w_up, ffn_w_down, final_norm):
    S = x.shape[1]
    pos = jnp.arange(S, dtype=jnp.float32)
    h = x
    for l in range(DEPTH):
        hn = rms_norm(h, mix_norm[l])
        if l % 2 == 0:
            e = l // 2
            h = h + even_mixer(hn, even_w_in[e], even_w_out[e], diff_lq1[e], diff_lk1[e],
                               diff_lq2[e], diff_lk2[e], diff_subln[e], gmlp_ln_g[e],
                               gmlp_ln_b[e], gmlp_w_s[e], gmlp_b_s[e], l, pos)
        else:
            o = l // 2
            h = h + odd_mixer(hn, hgrn_w_in[o], hgrn_w_out[o], hgrn_lower_bounds,
                              hgrn_g_norm[o], l)
        hn = rms_norm(h, ffn_norm[l])
        h = h + swiglu(hn, ffn_w_gate[l], ffn_w_up[l], ffn_w_down[l])
    return rms_norm(h, final_norm)
```

```python
import functools
import math

import jax
import jax.numpy as jnp
from jax import lax
from jax.experimental import pallas as pl
from jax.experimental.pallas import tpu as pltpu

F32 = jnp.float32
BF16 = jnp.bfloat16
EPS = 1e-6
MIB = 1024 * 1024

A_HEADS = 8
A_QK_DIM = 64
A_V_DIM = 128
ROPE_THETA = 500000.0
ROPE_DIM = A_QK_DIM // 4
B_GROUPS = 8
B_GROUP_DIM = 128
B_CHUNK = 128
C_HEADS = 16
C_DK = 128
HG_CHUNK = 64
HG_SUB = 16


def _params(semantics, vmem_mib):
    return pltpu.CompilerParams(dimension_semantics=semantics,
                                vmem_limit_bytes=vmem_mib * MIB)


def _rms(x, gain):
    ms = jnp.mean(x * x, axis=-1, keepdims=True)
    return x * lax.rsqrt(ms + EPS) * gain


def _sigmoid(x):
    return 1.0 / (1.0 + jnp.exp(-x))


def _gelu_tanh(x):
    c = math.sqrt(2.0 / math.pi)
    return x * (0.5 * (1.0 + jnp.tanh(c * (x + 0.044715 * (x * x * x)))))


def _dot(a, b):
    return jnp.dot(a, b, preferred_element_type=F32)


def _dot_nt(a, b):
    return lax.dot_general(a, b, (((1,), (1,)), ((), ())), preferred_element_type=F32)


def _dot_tn(a, b):
    return lax.dot_general(a, b, (((0,), (0,)), ((), ())), preferred_element_type=F32)


def _norm_proj_body(x_ref, g_ref, w_ref, o_ref, xn_ref):
    @pl.when(pl.program_id(1) == 0)
    def _():
        xn_ref[...] = _rms(x_ref[...], g_ref[...]).astype(BF16)

    o_ref[...] = _dot(xn_ref[...], w_ref[...])


def norm_proj(x, gain, w, tm=1024, tn=512):
    m, k = x.shape
    n = w.shape[1]
    tm, tn = min(tm, m), min(tn, n)
    return pl.pallas_call(
        _norm_proj_body,
        grid=(m // tm, n // tn),
        in_specs=[pl.BlockSpec((tm, k), lambda i, j: (i, 0)),
                  pl.BlockSpec((1, k), lambda i, j: (0, 0)),
                  pl.BlockSpec((k, tn), lambda i, j: (0, j))],
        out_specs=pl.BlockSpec((tm, tn), lambda i, j: (i, j)),
        out_shape=jax.ShapeDtypeStruct((m, n), F32),
        scratch_shapes=[pltpu.VMEM((tm, k), BF16)],
        compiler_params=_params(("parallel", "arbitrary"), 48),
        name="norm_proj",
    )(x, gain.reshape(1, k), w)


def _rope(x, cos, sin_lo, sin_hi):
    half = ROPE_DIM // 2
    return (x * cos + pltpu.roll(x, half, 1) * sin_lo
            + pltpu.roll(x, x.shape[1] - half, 1) * sin_hi)


def _attn_body(lam_ref, subln_ref, cq_ref, slq_ref, shq_ref, ck_ref, slk_ref, shk_ref,
               q_ref, k_ref, v_ref, o_ref, kb_ref, vb_ref, *, lam_init):
    @pl.when(pl.program_id(2) == 0)
    def _():
        kb_ref[...] = _rope(k_ref[0], ck_ref[...], slk_ref[...], shk_ref[...]).astype(BF16)
        vb_ref[...] = v_ref[0].astype(BF16)

    scale = A_QK_DIM ** -0.5
    q = _rope(q_ref[0], cq_ref[...], slq_ref[...], shq_ref[...]) * scale
    lane = lax.broadcasted_iota(jnp.int32, q.shape, 1)
    kb = kb_ref[...]
    vb = vb_ref[...]

    def softmax_pv(qc):
        s = _dot_nt(qc.astype(BF16), kb)
        p = jnp.exp(s - jnp.max(s, axis=-1, keepdims=True))
        return _dot(p.astype(BF16), vb) / jnp.sum(p, axis=-1, keepdims=True)

    o0 = softmax_pv(jnp.where(lane < A_QK_DIM, q, 0.0))
    o1 = softmax_pv(jnp.where(lane >= A_QK_DIM, q, 0.0))
    lv = lam_ref[...]
    lam = (jnp.exp(jnp.sum(lv[0:1] * lv[1:2], axis=-1, keepdims=True))
           - jnp.exp(jnp.sum(lv[2:3] * lv[3:4], axis=-1, keepdims=True)) + lam_init)
    o = o0 - lam * o1
    o_ref[0] = (_rms(o, subln_ref[...]) * (1.0 - lam_init)).astype(BF16)


def _rope_tables(s):
    half = ROPE_DIM // 2
    pos = jnp.arange(s, dtype=F32)
    inv_freq = ROPE_THETA ** (-jnp.arange(half, dtype=F32) / half)
    ang = pos[:, None] * inv_freq[None, :]
    cos, sin = jnp.cos(ang), jnp.sin(ang)
    pad = A_QK_DIM - ROPE_DIM
    one = jnp.ones((s, pad), F32)
    zero = jnp.zeros((s, pad), F32)
    zh = jnp.zeros((s, half), F32)
    cos_c = jnp.concatenate([cos, cos, one], axis=1)
    lo_c = jnp.concatenate([zh, sin, zero], axis=1)
    hi_c = jnp.concatenate([-sin, zh, zero], axis=1)
    rep = lambda t: jnp.concatenate([t, t], axis=1)
    return rep(cos_c), rep(lo_c), rep(hi_c)


def diff_attention(proj, lam_vec, subln, lam_init, tq=256):
    b, s, _ = proj.shape
    tq = min(tq, s)
    cos, lo, hi = _rope_tables(s)
    hd = A_V_DIM
    tab_q = pl.BlockSpec((tq, hd), lambda bi, h, qi: (qi, 0))
    tab_k = pl.BlockSpec((s, hd), lambda bi, h, qi: (0, 0))
    return pl.pallas_call(
        functools.partial(_attn_body, lam_init=lam_init),
        grid=(b, A_HEADS, s // tq),
        in_specs=[pl.BlockSpec((4, A_QK_DIM), lambda bi, h, qi: (0, 0)),
                  pl.BlockSpec((1, hd), lambda bi, h, qi: (0, 0)),
                  tab_q, tab_q, tab_q, tab_k, tab_k, tab_k,
                  pl.BlockSpec((1, tq, hd), lambda bi, h, qi: (bi, qi, h)),
                  pl.BlockSpec((1, s, hd), lambda bi, h, qi: (bi, 0, A_HEADS + h)),
                  pl.BlockSpec((1, s, hd), lambda bi, h, qi: (bi, 0, 2 * A_HEADS + h))],
        out_specs=pl.BlockSpec((1, tq, hd), lambda bi, h, qi: (bi, qi, h)),
        out_shape=jax.ShapeDtypeStruct((b, s, A_HEADS * hd), BF16),
        scratch_shapes=[pltpu.VMEM((s, hd), BF16), pltpu.VMEM((s, hd), BF16)],
        compiler_params=_params(("parallel", "parallel", "arbitrary"), 48),
        name="diff_attention",
    )(lam_vec, subln.reshape(1, hd), cos, lo, hi, cos, lo, hi, proj, proj, proj)


def _gmlp_body(u_ref, v_ref, lng_ref, lnb_ref, ws_ref, bs_ref, o_ref, *, n_chunks):
    gd = B_GROUP_DIM
    for c in range(n_chunks):
        rows = slice(c * B_CHUNK, (c + 1) * B_CHUNK)
        vb = _gelu_tanh(v_ref[0, rows, :])
        xc = vb - jnp.mean(vb, axis=-1, keepdims=True)
        var = jnp.mean(xc * xc, axis=-1, keepdims=True)
        y = (xc * lax.rsqrt(var + EPS) * lng_ref[...] + lnb_ref[...]).astype(BF16)
        u = _gelu_tanh(u_ref[0, rows, :])
        for g in range(B_GROUPS):
            cols = slice(g * gd, (g + 1) * gd)
            sv = _dot(ws_ref[g], y[:, cols]) + bs_ref[:, g:g + 1]
            o_ref[0, rows, cols] = (u[:, cols] * sv).astype(BF16)


def gmlp(proj, ln_g, ln_b, w_s, b_s, u_block, n_chunks=4):
    b, s, _ = proj.shape
    width = B_GROUPS * B_GROUP_DIM
    n_chunks = min(n_chunks, s // B_CHUNK)
    t = n_chunks * B_CHUNK
    return pl.pallas_call(
        functools.partial(_gmlp_body, n_chunks=n_chunks),
        grid=(b, s // t),
        in_specs=[pl.BlockSpec((1, t, width), lambda bi, n: (bi, n, u_block)),
                  pl.BlockSpec((1, t, width), lambda bi, n: (bi, n, u_block + 1)),
                  pl.BlockSpec((1, width), lambda bi, n: (0, 0)),
                  pl.BlockSpec((1, width), lambda bi, n: (0, 0)),
                  pl.BlockSpec((B_GROUPS, B_CHUNK, B_CHUNK), lambda bi, n: (0, 0, 0)),
                  pl.BlockSpec((B_CHUNK, B_GROUPS), lambda bi, n: (0, 0))],
        out_specs=pl.BlockSpec((1, t, width), lambda bi, n: (bi, n, 0)),
        out_shape=jax.ShapeDtypeStruct((b, s, width), BF16),
        compiler_params=_params(("parallel", "parallel"), 32),
        name="gmlp",
    )(proj, proj, ln_g.reshape(1, width), ln_b.reshape(1, width), w_s.astype(BF16), b_s.T)


def _out_proj2_body(a_ref, b_ref, w_ref, r_ref, o_ref):
    ka = a_ref.shape[1]
    acc = _dot(a_ref[...], w_ref[0:ka, :]) + _dot(b_ref[...], w_ref[ka:, :])
    o_ref[...] = r_ref[...] + acc


def out_proj2(a, b, w, res, tm=1024, tn=512):
    m, ka = a.shape
    kb = b.shape[1]
    n = w.shape[1]
    tm, tn = min(tm, m), min(tn, n)
    return pl.pallas_call(
        _out_proj2_body,
        grid=(m // tm, n // tn),
        in_specs=[pl.BlockSpec((tm, ka), lambda i, j: (i, 0)),
                  pl.BlockSpec((tm, kb), lambda i, j: (i, 0)),
                  pl.BlockSpec((ka + kb, tn), lambda i, j: (0, j)),
                  pl.BlockSpec((tm, tn), lambda i, j: (i, j))],
        out_specs=pl.BlockSpec((tm, tn), lambda i, j: (i, j)),
        out_shape=jax.ShapeDtypeStruct((m, n), F32),
        compiler_params=_params(("parallel", "parallel"), 48),
        name="out_proj_even",
    )(a, b, w, res)


def _out_proj_gated_body(o_in_ref, gate_ref, gn_ref, w_ref, r_ref, o_ref, y_ref):
    @pl.when(pl.program_id(1) == 0)
    def _():
        gate = gate_ref[...]
        y = _rms(o_in_ref[...], gn_ref[...]) * (gate * _sigmoid(gate))
        y_ref[...] = y.astype(BF16)

    o_ref[...] = r_ref[...] + _dot(y_ref[...], w_ref[...])


def out_proj_gated(o_in, proj, gate_block, gain, w, res, tm=512, tn=512):
    m, k = o_in.shape
    n = w.shape[1]
    tm, tn = min(tm, m), min(tn, n)
    return pl.pallas_call(
        _out_proj_gated_body,
        grid=(m // tm, n // tn),
        in_specs=[pl.BlockSpec((tm, k), lambda i, j: (i, 0)),
                  pl.BlockSpec((tm, k), lambda i, j: (i, gate_block)),
                  pl.BlockSpec((1, k), lambda i, j: (0, 0)),
                  pl.BlockSpec((k, tn), lambda i, j: (0, j)),
                  pl.BlockSpec((tm, tn), lambda i, j: (i, j))],
        out_specs=pl.BlockSpec((tm, tn), lambda i, j: (i, j)),
        out_shape=jax.ShapeDtypeStruct((m, n), F32),
        scratch_shapes=[pltpu.VMEM((tm, k), BF16)],
        compiler_params=_params(("parallel", "arbitrary"), 48),
        name="out_proj_odd",
    )(o_in, proj, gain.reshape(1, k), w, res)


def _ffn_body(*refs, final_norm):
    if final_norm:
        x_ref, g_ref, wg_ref, wu_ref, wd_ref, fn_ref, o_ref, xn_ref = refs
    else:
        x_ref, g_ref, wg_ref, wu_ref, wd_ref, o_ref, xn_ref = refs
    f = pl.program_id(1)

    @pl.when(f == 0)
    def _():
        x = x_ref[...]
        xn_ref[...] = _rms(x, g_ref[...]).astype(BF16)
        o_ref[...] = x

    xn = xn_ref[...]
    a = _dot(xn, wg_ref[...])
    u = _dot(xn, wu_ref[...])
    hid = (a * _sigmoid(a) * u).astype(BF16)
    o_ref[...] += _dot(hid, wd_ref[...])

    if final_norm:
        @pl.when(f == pl.num_programs(1) - 1)
        def _():
            o_ref[...] = _rms(o_ref[...], fn_ref[...])


def ffn(x, gain, w_gate, w_up, w_down, final_gain=None, tm=512, tf=512):
    m, k = x.shape
    hdim = w_gate.shape[1]
    tm, tf = min(tm, m), min(tf, hdim)
    vec = pl.BlockSpec((1, k), lambda i, f: (0, 0))
    in_specs = [pl.BlockSpec((tm, k), lambda i, f: (i, 0)), vec,
                pl.BlockSpec((k, tf), lambda i, f: (0, f)),
                pl.BlockSpec((k, tf), lambda i, f: (0, f)),
                pl.BlockSpec((tf, k), lambda i, f: (f, 0))]
    args = [x, gain.reshape(1, k), w_gate, w_up, w_down]
    if final_gain is not None:
        in_specs.append(vec)
        args.append(final_gain.reshape(1, k))
    return pl.pallas_call(
        functools.partial(_ffn_body, final_norm=final_gain is not None),
        grid=(m // tm, hdim // tf),
        in_specs=in_specs,
        out_specs=pl.BlockSpec((tm, k), lambda i, f: (i, 0)),
        out_shape=jax.ShapeDtypeStruct((m, k), F32),
        scratch_shapes=[pltpu.VMEM((tm, k), BF16)],
        compiler_params=_params(("parallel", "arbitrary"), 48),
        name="ffn",
    )(*args)


def _hgrn_chunk(qs, x_f, val, lb, tri, state, reverse):
    c, sub = HG_CHUNK, HG_SUB
    n_sub = c // sub
    f = lb + (1.0 - lb) * _sigmoid(x_f)
    kk = 1.0 - f
    logf = jnp.log(f)
    bcum = jnp.dot(tri, logf, preferred_element_type=F32, precision=lax.Precision.HIGHEST)
    vb = val.astype(BF16)

    o_inter = _dot_nt((qs * jnp.exp(bcum)).astype(BF16), state.astype(BF16))

    ckey = bcum - jnp.log(kk)
    t_idx = lax.broadcasted_iota(jnp.int32, (sub, C_DK), 0)
    ones = jnp.ones((C_DK, C_DK), BF16)
    outs = []
    for i in range(n_sub):
        r0 = i * sub
        b_i = bcum[r0:r0 + sub]
        q_i = qs[r0:r0 + sub]
        parts = []
        for s in range(sub):
            e = jnp.exp(jnp.minimum(b_i - ckey[r0 + s:r0 + s + 1], 0.0))
            keep = (t_idx <= s) if reverse else (t_idx >= s)
            parts.append(jnp.where(keep, q_i * e, 0.0).astype(BF16))
        rsum = _dot(jnp.concatenate(parts, axis=0), ones)
        o_i = jnp.zeros((sub, C_DK), F32)
        for s in range(sub):
            o_i = o_i + rsum[s * sub:(s + 1) * sub] * val[r0 + s:r0 + s + 1]
        if reverse and i < n_sub - 1:
            lo, hi = r0 + sub, c
            piv = bcum[lo:lo + 1]
        elif (not reverse) and i > 0:
            lo, hi = 0, r0
            piv = bcum[r0 - 1:r0]
        else:
            lo = hi = 0
        if hi > lo:
            qt = (q_i * jnp.exp(b_i - piv)).astype(BF16)
            kt = (kk[lo:hi] * jnp.exp(piv - bcum[lo:hi])).astype(BF16)
            sc = _dot_nt(qt, kt)
            o_i = o_i + _dot(sc.astype(BF16), vb[lo:hi])
        outs.append(o_i)
    o = o_inter + jnp.concatenate(outs, axis=0)

    edge = bcum[0:1] if reverse else bcum[c - 1:c]
    kdec = (kk * jnp.exp(edge - bcum)).astype(BF16)
    new_state = state * jnp.exp(edge) + _dot_tn(vb, kdec)
    return o, new_state


def _hgrn_body(q_ref, ff_ref, fb_ref, v_ref, lbr_ref, o_ref, *, depth, layer_idx):
    s = q_ref.shape[1]
    c = HG_CHUNK
    nc = s // c

    def lower_bound(d):
        rows = lbr_ref[d * depth:(d + 1) * depth, :]
        e = jnp.exp(rows - jnp.max(rows, axis=0, keepdims=True))
        sm = e / jnp.sum(e, axis=0, keepdims=True)
        acc = sm[0:1] * 0.0
        for l in range(1, layer_idx + 1):
            acc = acc + sm[l:l + 1]
        return acc

    lb_f, lb_b = lower_bound(0), lower_bound(1)
    row = lax.broadcasted_iota(jnp.int32, (c, c), 0)
    col = lax.broadcasted_iota(jnp.int32, (c, c), 1)
    tri_f = (col <= row).astype(F32)
    tri_b = (col >= row).astype(F32)
    o_ref[0] = jnp.zeros((s, C_DK), F32)

    def step(ci, carry):
        st_f, st_b = carry
        rf = pl.multiple_of(ci * c, c)
        rb = pl.multiple_of((nc - 1 - ci) * c, c)
        qf = q_ref[0, pl.ds(rf, c), :]
        o_f, st_f = _hgrn_chunk(qf * _sigmoid(qf), ff_ref[0, pl.ds(rf, c), :],
                                v_ref[0, pl.ds(rf, c), :], lb_f, tri_f, st_f, False)
        o_ref[0, pl.ds(rf, c), :] += o_f
        qb = q_ref[0, pl.ds(rb, c), :]
        o_b, st_b = _hgrn_chunk(qb * _sigmoid(qb), fb_ref[0, pl.ds(rb, c), :],
                                v_ref[0, pl.ds(rb, c), :], lb_b, tri_b, st_b, True)
        o_ref[0, pl.ds(rb, c), :] += o_b
        return st_f, st_b

    zero = jnp.zeros((C_DK, C_DK), F32)
    lax.fori_loop(0, nc, step, (zero, zero))


def hgrn2(proj, lower_bounds, layer_idx):
    b, s, _ = proj.shape
    depth = lower_bounds.shape[1]
    dk = C_DK
    blk = lambda off: pl.BlockSpec((1, s, dk), lambda bi, h: (bi, 0, off * C_HEADS + h))
    return pl.pallas_call(
        functools.partial(_hgrn_body, depth=depth, layer_idx=layer_idx),
        grid=(b, C_HEADS),
        in_specs=[blk(0), blk(1), blk(2), blk(3),
                  pl.BlockSpec((2 * depth, dk), lambda bi, h: (0, h))],
        out_specs=pl.BlockSpec((1, s, dk), lambda bi, h: (bi, 0, h)),
        out_shape=jax.ShapeDtypeStruct((b, s, C_HEADS * dk), F32),
        compiler_params=_params(("parallel", "parallel"), 32),
        name="hgrn2",
    )(proj, proj, proj, proj, lower_bounds.reshape(2 * depth, C_HEADS * dk))


def kernel(x, mix_norm, even_w_in, even_w_out, diff_lq1, diff_lk1, diff_lq2, diff_lk2,
           diff_subln, gmlp_ln_g, gmlp_ln_b, gmlp_w_s, gmlp_b_s, hgrn_w_in, hgrn_w_out,
           hgrn_lower_bounds, hgrn_g_norm, ffn_norm, ffn_w_gate, ffn_w_up, ffn_w_down,
           final_norm):
    b, s, d = x.shape
    depth = mix_norm.shape[0]
    m = b * s
    h = x.reshape(m, d)
    bf = lambda w: w.astype(BF16)
    for l in range(depth):
        if l % 2 == 0:
            e = l // 2
            proj = norm_proj(h, mix_norm[l], bf(even_w_in[e])).reshape(b, s, -1)
            lam_init = 0.8 - 0.6 * math.exp(-0.3 * l)
            lam_vec = jnp.stack([diff_lq1[e], diff_lk1[e], diff_lq2[e], diff_lk2[e]]).astype(F32)
            oa = diff_attention(proj, lam_vec, diff_subln[e], lam_init)
            u_block = (3 * A_HEADS * A_V_DIM) // (B_GROUPS * B_GROUP_DIM)
            ob = gmlp(proj, gmlp_ln_g[e], gmlp_ln_b[e], gmlp_w_s[e], gmlp_b_s[e], u_block)
            h = out_proj2(oa.reshape(m, -1), ob.reshape(m, -1), bf(even_w_out[e]), h)
        else:
            o = l // 2
            proj = norm_proj(h, mix_norm[l], bf(hgrn_w_in[o]))
            scan = hgrn2(proj.reshape(b, s, -1), hgrn_lower_bounds, l)
            h = out_proj_gated(scan.reshape(m, -1), proj, 4, hgrn_g_norm[o],
                               bf(hgrn_w_out[o]), h)
        last = l == depth - 1
        h = ffn(h, ffn_norm[l], bf(ffn_w_gate[l]), bf(ffn_w_up[l]), bf(ffn_w_down[l]),
                final_gain=final_norm if last else None)
    if depth == 0:
        h = _rms(h, final_norm)
    return h.reshape(b, s, d)
```

```python
import functools
import math

import jax
import jax.numpy as jnp
from jax import lax
from jax.experimental import pallas as pl
from jax.experimental.pallas import tpu as pltpu

F32 = jnp.float32
BF16 = jnp.bfloat16
EPS = 1e-6
MIB = 1024 * 1024

A_HEADS = 8
A_QK_DIM = 64
A_V_DIM = 128
ROPE_THETA = 500000.0
ROPE_DIM = A_QK_DIM // 4
B_GROUPS = 8
B_GROUP_DIM = 128
B_CHUNK = 128
C_HEADS = 16
C_DK = 128
HG_CHUNK = 64
HG_SUB = 16
HG_PAR = 2
HG_PREP = 4


def _params(semantics, vmem_mib):
    return pltpu.CompilerParams(dimension_semantics=semantics,
                                vmem_limit_bytes=vmem_mib * MIB)


def _rms(x, gain):
    ms = jnp.mean(x * x, axis=-1, keepdims=True)
    return x * lax.rsqrt(ms + EPS) * gain


def _sigmoid(x):
    return 1.0 / (1.0 + jnp.exp(-x))


def _gelu_tanh(x):
    c = math.sqrt(2.0 / math.pi)
    return x * (0.5 * (1.0 + jnp.tanh(c * (x + 0.044715 * (x * x * x)))))


def _dot(a, b):
    return jnp.dot(a, b, preferred_element_type=F32)


def _dot_nt(a, b):
    return lax.dot_general(a, b, (((1,), (1,)), ((), ())), preferred_element_type=F32)


def _dot_tn(a, b):
    return lax.dot_general(a, b, (((0,), (0,)), ((), ())), preferred_element_type=F32)


def _norm_proj_body(x_ref, g_ref, w_ref, o_ref, xn_ref):
    @pl.when(pl.program_id(1) == 0)
    def _():
        xn_ref[...] = _rms(x_ref[...], g_ref[...]).astype(BF16)

    o_ref[...] = _dot(xn_ref[...], w_ref[...])


def norm_proj(x, gain, w, tm=1024, tn=512):
    m, k = x.shape
    n = w.shape[1]
    tm, tn = min(tm, m), min(tn, n)
    return pl.pallas_call(
        _norm_proj_body,
        grid=(m // tm, n // tn),
        in_specs=[pl.BlockSpec((tm, k), lambda i, j: (i, 0)),
                  pl.BlockSpec((1, k), lambda i, j: (0, 0)),
                  pl.BlockSpec((k, tn), lambda i, j: (0, j))],
        out_specs=pl.BlockSpec((tm, tn), lambda i, j: (i, j)),
        out_shape=jax.ShapeDtypeStruct((m, n), F32),
        scratch_shapes=[pltpu.VMEM((tm, k), BF16)],
        compiler_params=_params(("parallel", "arbitrary"), 48),
        name="norm_proj",
    )(x, gain.reshape(1, k), w)


def _rope(x, cos, sin_lo, sin_hi):
    half = ROPE_DIM // 2
    return (x * cos + pltpu.roll(x, half, 1) * sin_lo
            + pltpu.roll(x, x.shape[1] - half, 1) * sin_hi)


def _attn_body(lam_ref, subln_ref, cq_ref, slq_ref, shq_ref, ck_ref, slk_ref, shk_ref,
               q_ref, k_ref, v_ref, o_ref, kb_ref, vb_ref, *, lam_init):
    @pl.when(pl.program_id(2) == 0)
    def _():
        kb_ref[...] = _rope(k_ref[0], ck_ref[...], slk_ref[...], shk_ref[...]).astype(BF16)
        vb_ref[...] = v_ref[0].astype(BF16)

    scale = A_QK_DIM ** -0.5
    q = _rope(q_ref[0], cq_ref[...], slq_ref[...], shq_ref[...]) * scale
    lane = lax.broadcasted_iota(jnp.int32, q.shape, 1)
    kb = kb_ref[...]
    vb = vb_ref[...]

    def softmax_pv(qc):
        s = _dot_nt(qc.astype(BF16), kb)
        p = jnp.exp(s - jnp.max(s, axis=-1, keepdims=True))
        return _dot(p.astype(BF16), vb) / jnp.sum(p, axis=-1, keepdims=True)

    o0 = softmax_pv(jnp.where(lane < A_QK_DIM, q, 0.0))
    o1 = softmax_pv(jnp.where(lane >= A_QK_DIM, q, 0.0))
    lv = lam_ref[...]
    lam = (jnp.exp(jnp.sum(lv[0:1] * lv[1:2], axis=-1, keepdims=True))
           - jnp.exp(jnp.sum(lv[2:3] * lv[3:4], axis=-1, keepdims=True)) + lam_init)
    o = o0 - lam * o1
    o_ref[0] = (_rms(o, subln_ref[...]) * (1.0 - lam_init)).astype(BF16)


def _rope_tables(s):
    half = ROPE_DIM // 2
    pos = jnp.arange(s, dtype=F32)
    inv_freq = ROPE_THETA ** (-jnp.arange(half, dtype=F32) / half)
    ang = pos[:, None] * inv_freq[None, :]
    cos, sin = jnp.cos(ang), jnp.sin(ang)
    pad = A_QK_DIM - ROPE_DIM
    one = jnp.ones((s, pad), F32)
    zero = jnp.zeros((s, pad), F32)
    zh = jnp.zeros((s, half), F32)
    cos_c = jnp.concatenate([cos, cos, one], axis=1)
    lo_c = jnp.concatenate([zh, sin, zero], axis=1)
    hi_c = jnp.concatenate([-sin, zh, zero], axis=1)
    rep = lambda t: jnp.concatenate([t, t], axis=1)
    return rep(cos_c), rep(lo_c), rep(hi_c)


def diff_attention(proj, lam_vec, subln, lam_init, tq=256):
    b, s, _ = proj.shape
    tq = min(tq, s)
    cos, lo, hi = _rope_tables(s)
    hd = A_V_DIM
    tab_q = pl.BlockSpec((tq, hd), lambda bi, h, qi: (qi, 0))
    tab_k = pl.BlockSpec((s, hd), lambda bi, h, qi: (0, 0))
    return pl.pallas_call(
        functools.partial(_attn_body, lam_init=lam_init),
        grid=(b, A_HEADS, s // tq),
        in_specs=[pl.BlockSpec((4, A_QK_DIM), lambda bi, h, qi: (0, 0)),
                  pl.BlockSpec((1, hd), lambda bi, h, qi: (0, 0)),
                  tab_q, tab_q, tab_q, tab_k, tab_k, tab_k,
                  pl.BlockSpec((1, tq, hd), lambda bi, h, qi: (bi, qi, h)),
                  pl.BlockSpec((1, s, hd), lambda bi, h, qi: (bi, 0, A_HEADS + h)),
                  pl.BlockSpec((1, s, hd), lambda bi, h, qi: (bi, 0, 2 * A_HEADS + h))],
        out_specs=pl.BlockSpec((1, tq, hd), lambda bi, h, qi: (bi, qi, h)),
        out_shape=jax.ShapeDtypeStruct((b, s, A_HEADS * hd), BF16),
        scratch_shapes=[pltpu.VMEM((s, hd), BF16), pltpu.VMEM((s, hd), BF16)],
        compiler_params=_params(("parallel", "parallel", "arbitrary"), 48),
        name="diff_attention",
    )(lam_vec, subln.reshape(1, hd), cos, lo, hi, cos, lo, hi, proj, proj, proj)


def _gmlp_body(u_ref, v_ref, lng_ref, lnb_ref, ws_ref, bs_ref, o_ref, *, n_chunks):
    gd = B_GROUP_DIM
    for c in range(n_chunks):
        rows = slice(c * B_CHUNK, (c + 1) * B_CHUNK)
        vb = _gelu_tanh(v_ref[0, rows, :])
        xc = vb - jnp.mean(vb, axis=-1, keepdims=True)
        var = jnp.mean(xc * xc, axis=-1, keepdims=True)
        y = (xc * lax.rsqrt(var + EPS) * lng_ref[...] + lnb_ref[...]).astype(BF16)
        u = _gelu_tanh(u_ref[0, rows, :])
        for g in range(B_GROUPS):
            cols = slice(g * gd, (g + 1) * gd)
            sv = _dot(ws_ref[g], y[:, cols]) + bs_ref[:, g:g + 1]
            o_ref[0, rows, cols] = (u[:, cols] * sv).astype(BF16)


def gmlp(proj, ln_g, ln_b, w_s, b_s, u_block, n_chunks=4):
    b, s, _ = proj.shape
    width = B_GROUPS * B_GROUP_DIM
    n_chunks = min(n_chunks, s // B_CHUNK)
    t = n_chunks * B_CHUNK
    return pl.pallas_call(
        functools.partial(_gmlp_body, n_chunks=n_chunks),
        grid=(b, s // t),
        in_specs=[pl.BlockSpec((1, t, width), lambda bi, n: (bi, n, u_block)),
                  pl.BlockSpec((1, t, width), lambda bi, n: (bi, n, u_block + 1)),
                  pl.BlockSpec((1, width), lambda bi, n: (0, 0)),
                  pl.BlockSpec((1, width), lambda bi, n: (0, 0)),
                  pl.BlockSpec((B_GROUPS, B_CHUNK, B_CHUNK), lambda bi, n: (0, 0, 0)),
                  pl.BlockSpec((B_CHUNK, B_GROUPS), lambda bi, n: (0, 0))],
        out_specs=pl.BlockSpec((1, t, width), lambda bi, n: (bi, n, 0)),
        out_shape=jax.ShapeDtypeStruct((b, s, width), BF16),
        compiler_params=_params(("parallel", "parallel"), 32),
        name="gmlp",
    )(proj, proj, ln_g.reshape(1, width), ln_b.reshape(1, width), w_s.astype(BF16), b_s.T)


def _out_proj2_body(a_ref, b_ref, w_ref, r_ref, o_ref):
    ka = a_ref.shape[1]
    acc = _dot(a_ref[...], w_ref[0:ka, :]) + _dot(b_ref[...], w_ref[ka:, :])
    o_ref[...] = r_ref[...] + acc


def out_proj2(a, b, w, res, tm=1024, tn=512):
    m, ka = a.shape
    kb = b.shape[1]
    n = w.shape[1]
    tm, tn = min(tm, m), min(tn, n)
    return pl.pallas_call(
        _out_proj2_body,
        grid=(m // tm, n // tn),
        in_specs=[pl.BlockSpec((tm, ka), lambda i, j: (i, 0)),
                  pl.BlockSpec((tm, kb), lambda i, j: (i, 0)),
                  pl.BlockSpec((ka + kb, tn), lambda i, j: (0, j)),
                  pl.BlockSpec((tm, tn), lambda i, j: (i, j))],
        out_specs=pl.BlockSpec((tm, tn), lambda i, j: (i, j)),
        out_shape=jax.ShapeDtypeStruct((m, n), F32),
        compiler_params=_params(("parallel", "parallel"), 48),
        name="out_proj_even",
    )(a, b, w, res)


def _out_proj_gated_body(o_in_ref, gate_ref, gn_ref, w_ref, r_ref, o_ref, y_ref):
    @pl.when(pl.program_id(1) == 0)
    def _():
        gate = gate_ref[...]
        y = _rms(o_in_ref[...], gn_ref[...]) * (gate * _sigmoid(gate))
        y_ref[...] = y.astype(BF16)

    o_ref[...] = r_ref[...] + _dot(y_ref[...], w_ref[...])


def out_proj_gated(o_in, proj, gate_block, gain, w, res, tm=512, tn=512):
    m, k = o_in.shape
    n = w.shape[1]
    tm, tn = min(tm, m), min(tn, n)
    return pl.pallas_call(
        _out_proj_gated_body,
        grid=(m // tm, n // tn),
        in_specs=[pl.BlockSpec((tm, k), lambda i, j: (i, 0)),
                  pl.BlockSpec((tm, k), lambda i, j: (i, gate_block)),
                  pl.BlockSpec((1, k), lambda i, j: (0, 0)),
                  pl.BlockSpec((k, tn), lambda i, j: (0, j)),
                  pl.BlockSpec((tm, tn), lambda i, j: (i, j))],
        out_specs=pl.BlockSpec((tm, tn), lambda i, j: (i, j)),
        out_shape=jax.ShapeDtypeStruct((m, n), F32),
        scratch_shapes=[pltpu.VMEM((tm, k), BF16)],
        compiler_params=_params(("parallel", "arbitrary"), 48),
        name="out_proj_odd",
    )(o_in, proj, gain.reshape(1, k), w, res)


def _ffn_body(*refs, final_norm):
    if final_norm:
        x_ref, g_ref, wg_ref, wu_ref, wd_ref, fn_ref, o_ref, xn_ref = refs
    else:
        x_ref, g_ref, wg_ref, wu_ref, wd_ref, o_ref, xn_ref = refs
    f = pl.program_id(1)

    @pl.when(f == 0)
    def _():
        x = x_ref[...]
        xn_ref[...] = _rms(x, g_ref[...]).astype(BF16)
        o_ref[...] = x

    xn = xn_ref[...]
    a = _dot(xn, wg_ref[...])
    u = _dot(xn, wu_ref[...])
    hid = (a * _sigmoid(a) * u).astype(BF16)
    o_ref[...] += _dot(hid, wd_ref[...])

    if final_norm:
        @pl.when(f == pl.num_programs(1) - 1)
        def _():
            o_ref[...] = _rms(o_ref[...], fn_ref[...])


def ffn(x, gain, w_gate, w_up, w_down, final_gain=None, tm=512, tf=512):
    m, k = x.shape
    hdim = w_gate.shape[1]
    tm, tf = min(tm, m), min(tf, hdim)
    vec = pl.BlockSpec((1, k), lambda i, f: (0, 0))
    in_specs = [pl.BlockSpec((tm, k), lambda i, f: (i, 0)), vec,
                pl.BlockSpec((k, tf), lambda i, f: (0, f)),
                pl.BlockSpec((k, tf), lambda i, f: (0, f)),
                pl.BlockSpec((tf, k), lambda i, f: (f, 0))]
    args = [x, gain.reshape(1, k), w_gate, w_up, w_down]
    if final_gain is not None:
        in_specs.append(vec)
        args.append(final_gain.reshape(1, k))
    return pl.pallas_call(
        functools.partial(_ffn_body, final_norm=final_gain is not None),
        grid=(m // tm, hdim // tf),
        in_specs=in_specs,
        out_specs=pl.BlockSpec((tm, k), lambda i, f: (i, 0)),
        out_shape=jax.ShapeDtypeStruct((m, k), F32),
        scratch_shapes=[pltpu.VMEM((tm, k), BF16)],
        compiler_params=_params(("parallel", "arbitrary"), 48),
        name="ffn",
    )(*args)


_HALF = 8


def _hgrn_tables(ti, carry, *, x_refs, lbs, q_ref, v_ref, tri_ref, bcum_ref, ckey_ref,
                 qs_ref, vb_ref, upd_ref, dec_ref):
    c = HG_CHUNK
    rows = [pl.multiple_of((ti * HG_PREP + u) * c, c) for u in range(HG_PREP)]
    kdec = [[None, None] for _ in range(HG_PREP)]
    decay = [[None, None] for _ in range(HG_PREP)]
    for d in range(2):
        lb = lbs[d]
        splits, log_k = [], []
        for r in rows:
            f = lb + (1.0 - lb) * _sigmoid(x_refs[d][0, pl.ds(r, c), :])
            logf = jnp.log(f)
            log_k.append(jnp.log(1.0 - f))
            hi = logf.astype(BF16)
            rem = logf - hi.astype(F32)
            mid = rem.astype(BF16)
            low = (rem - mid.astype(F32)).astype(BF16)
            splits.append(jnp.concatenate([hi, mid, low], axis=0))
        bc = _dot(tri_ref[d], jnp.concatenate(splits, axis=1))
        for u, r in enumerate(rows):
            bcum = bc[:, u * C_DK:(u + 1) * C_DK]
            ckey = bcum - log_k[u]
            bcum_ref[d, pl.ds(r, c), :] = bcum
            ckey_ref[d, pl.ds(r, c), :] = ckey
            edge = bcum[c - 1:c] if d == 0 else bcum[0:1]
            kdec[u][d] = jnp.exp(edge - ckey).astype(BF16)
            decay[u][d] = jnp.exp(edge)
    for u, r in enumerate(rows):
        ci = ti * HG_PREP + u
        vb = v_ref[0, pl.ds(r, c), :].astype(BF16)
        vb_ref[pl.ds(r, c), :] = vb
        upd_ref[ci] = _dot_tn(vb, jnp.concatenate(kdec[u], axis=1))
        dec_ref[ci] = jnp.concatenate(decay[u], axis=1)
        q = q_ref[0, pl.ds(r, c), :]
        qs_ref[pl.ds(r, c), :] = q * _sigmoid(q)
    return carry


def _hgrn_scan(ci, carry, *, upd_ref, dec_ref, sbf_ref, nc):
    st_f, st_b = carry
    cb = nc - 1 - ci
    sbf_ref[0, ci] = st_f.astype(BF16)
    sbf_ref[1, cb] = st_b.astype(BF16)
    st_f = st_f * dec_ref[ci, :, 0:C_DK] + upd_ref[ci, :, 0:C_DK]
    st_b = st_b * dec_ref[cb, :, C_DK:2 * C_DK] + upd_ref[cb, :, C_DK:2 * C_DK]
    return st_f, st_b


def _direct_pairs(reverse):
    pairs = []
    for s in range(HG_SUB):
        for h in range(2):
            sh = s // _HALF
            if not ((h > sh) if reverse else (h < sh)):
                pairs.append((s, h))
    return pairs


def _hgrn_direct_pieces(d, r0, *, bcum_ref, ckey_ref, qs_ref):
    reverse = d == 1
    t_idx = lax.broadcasted_iota(jnp.int32, (_HALF, C_DK), 0)
    b = [bcum_ref[d, pl.ds(r0 + _HALF * h, _HALF), :] for h in range(2)]
    q = [qs_ref[pl.ds(r0 + _HALF * h, _HALF), :] for h in range(2)]
    pieces = []
    for s, h in _direct_pairs(reverse):
        p = q[h] * jnp.exp(b[h] - ckey_ref[d, pl.ds(r0 + s, 1), :])
        if h == s // _HALF:
            t0 = s - _HALF * h
            p = jnp.where((t_idx <= t0) if reverse else (t_idx >= t0), p, 0.0)
        pieces.append(p)
    rows = len(pieces) // 2 * _HALF
    lhs = jnp.concatenate(pieces, axis=0).astype(BF16)
    return jnp.concatenate([lhs[:rows], lhs[rows:]], axis=1)


def _hgrn_direct_combine(d, r0, rs_ref, row0, *, v_ref):
    pairs = _direct_pairs(d == 1)
    n_half = len(pairs) // 2
    out = [jnp.zeros((_HALF, C_DK), F32), jnp.zeros((_HALF, C_DK), F32)]
    for p_idx, (s, h) in enumerate(pairs):
        col, row = divmod(p_idx, n_half)
        rp = rs_ref[row0 + row * _HALF:row0 + (row + 1) * _HALF, col * C_DK:(col + 1) * C_DK]
        out[h] = out[h] + rp * v_ref[0, pl.ds(r0 + s, 1), :]
    return jnp.concatenate(out, axis=0)


def _pivot_plan(reverse):
    s, h = HG_SUB, HG_CHUNK // 2
    assert HG_CHUNK == 4 * HG_SUB
    if reverse:
        return [(0, h, h, h, h), (0, s, s, s, s), (h, s, h + s, s, h + s)]
    return [(h, h, 0, h, h - 1), (s, s, 0, s, s - 1), (h + s, s, h, s, h + s - 1)]


def _hgrn_output(ti, carry, *, v_ref, o_ref, bcum_ref, ckey_ref, qs_ref, vb_ref, sbf_ref,
                 w2_ref, pmask_ref, lhs_ref, rs_ref):
    c, sub = HG_CHUNK, HG_SUB
    n_sub = c // sub
    blk_rows = len(_direct_pairs(False)) // 2 * _HALF
    rows = [pl.multiple_of((ti * HG_PAR + u) * c, c) for u in range(HG_PAR)]
    for u, r in enumerate(rows):
        for d in range(2):
            for i in range(n_sub):
                row0 = ((u * 2 + d) * n_sub + i) * blk_rows
                lhs_ref[row0:row0 + blk_rows, :] = _hgrn_direct_pieces(
                    d, r + i * sub, bcum_ref=bcum_ref, ckey_ref=ckey_ref, qs_ref=qs_ref)
    rs_ref[...] = _dot(lhs_ref[...], w2_ref[...])
    q_parts, k_parts, v_parts = [], [], []
    for r in rows:
        for d in range(2):
            for q0, qn, k0, kn, prow in _pivot_plan(d == 1):
                piv = bcum_ref[d, pl.ds(r + prow, 1), :]
                q_parts.append(qs_ref[pl.ds(r + q0, qn), :]
                               * jnp.exp(bcum_ref[d, pl.ds(r + q0, qn), :] - piv))
                k_parts.append(jnp.exp(piv - ckey_ref[d, pl.ds(r + k0, kn), :]))
                v_parts.append(vb_ref[pl.ds(r + k0, kn), :])
    sc = _dot_nt(jnp.concatenate(q_parts, axis=0).astype(BF16),
                 jnp.concatenate(k_parts, axis=0).astype(BF16)) * pmask_ref[...]
    pv = _dot(sc.astype(BF16), jnp.concatenate(v_parts, axis=0))
    for u, r in enumerate(rows):
        ci = ti * HG_PAR + u
        total = None
        for d in range(2):
            blocks = [_hgrn_direct_combine(d, r + i * sub, rs_ref,
                                           ((u * 2 + d) * n_sub + i) * blk_rows, v_ref=v_ref)
                      for i in range(n_sub)]
            off = (u * 2 + d) * c
            for q0, qn, _, _, _ in _pivot_plan(d == 1):
                for j in range(qn // sub):
                    blocks[q0 // sub + j] = (blocks[q0 // sub + j]
                                             + pv[off + j * sub:off + (j + 1) * sub])
                off += qn
            qe = (qs_ref[pl.ds(r, c), :] * jnp.exp(bcum_ref[d, pl.ds(r, c), :])).astype(BF16)
            o_d = _dot_nt(qe, sbf_ref[d, ci]) + jnp.concatenate(blocks, axis=0)
            total = o_d if total is None else total + o_d
        o_ref[0, pl.ds(r, c), :] = total
    return carry


def _hgrn_body(q_ref, ff_ref, fb_ref, v_ref, lbr_ref, o_ref,
               bcum_ref, ckey_ref, qs_ref, vb_ref, upd_ref, dec_ref, sbf_ref, tri_ref, w2_ref,
               pmask_ref, lhs_ref, rs_ref, *, depth, layer_idx):
    s = q_ref.shape[1]
    c = HG_CHUNK
    nc = s // c
    assert nc % HG_PAR == 0 and nc % HG_PREP == 0

    def lower_bound(d):
        rows = lbr_ref[d * depth:(d + 1) * depth, :]
        e = jnp.exp(rows - jnp.max(rows, axis=0, keepdims=True))
        sm = e / jnp.sum(e, axis=0, keepdims=True)
        acc = sm[0:1] * 0.0
        for l in range(1, layer_idx + 1):
            acc = acc + sm[l:l + 1]
        return acc

    row = lax.broadcasted_iota(jnp.int32, (c, 3 * c), 0)
    col = lax.broadcasted_iota(jnp.int32, (c, 3 * c), 1) % c
    tri_ref[0] = (col <= row).astype(BF16)
    tri_ref[1] = (col >= row).astype(BF16)
    wr = lax.broadcasted_iota(jnp.int32, (2 * C_DK, 2 * C_DK), 0) // C_DK
    wc = lax.broadcasted_iota(jnp.int32, (2 * C_DK, 2 * C_DK), 1) // C_DK
    w2_ref[...] = (wr == wc).astype(BF16)
    n = HG_PAR * 2 * c
    pr = lax.broadcasted_iota(jnp.int32, (n, n), 0)
    pc = lax.broadcasted_iota(jnp.int32, (n, n), 1)
    entry = lambda x: jnp.where(x % c < c // 2, 0, (x % c) // HG_SUB)
    pmask_ref[...] = ((pr // c == pc // c) & (entry(pr) == entry(pc))).astype(F32)

    lax.fori_loop(0, nc // HG_PREP, functools.partial(
        _hgrn_tables, x_refs=(ff_ref, fb_ref), lbs=(lower_bound(0), lower_bound(1)), q_ref=q_ref,
        v_ref=v_ref, tri_ref=tri_ref, bcum_ref=bcum_ref, ckey_ref=ckey_ref, qs_ref=qs_ref,
        vb_ref=vb_ref, upd_ref=upd_ref, dec_ref=dec_ref), 0)
    zero = jnp.zeros((C_DK, C_DK), F32)
    lax.fori_loop(0, nc, functools.partial(
        _hgrn_scan, upd_ref=upd_ref, dec_ref=dec_ref, sbf_ref=sbf_ref, nc=nc), (zero, zero))
    lax.fori_loop(0, nc // HG_PAR, functools.partial(
        _hgrn_output, v_ref=v_ref, o_ref=o_ref, bcum_ref=bcum_ref, ckey_ref=ckey_ref,
        qs_ref=qs_ref, vb_ref=vb_ref, sbf_ref=sbf_ref, w2_ref=w2_ref, pmask_ref=pmask_ref,
        lhs_ref=lhs_ref, rs_ref=rs_ref), 0)


def hgrn2(proj, lower_bounds, layer_idx):
    b, s, _ = proj.shape
    depth = lower_bounds.shape[1]
    dk = C_DK
    nc = s // HG_CHUNK
    direct_rows = HG_PAR * 2 * (HG_CHUNK // HG_SUB) * (len(_direct_pairs(False)) // 2 * _HALF)
    pivot_rows = HG_PAR * 2 * HG_CHUNK
    blk = lambda off: pl.BlockSpec((1, s, dk), lambda bi, h: (bi, 0, off * C_HEADS + h))
    return pl.pallas_call(
        functools.partial(_hgrn_body, depth=depth, layer_idx=layer_idx),
        grid=(b, C_HEADS),
        in_specs=[blk(0), blk(1), blk(2), blk(3),
                  pl.BlockSpec((2 * depth, dk), lambda bi, h: (0, h))],
        out_specs=pl.BlockSpec((1, s, dk), lambda bi, h: (bi, 0, h)),
        out_shape=jax.ShapeDtypeStruct((b, s, C_HEADS * dk), F32),
        scratch_shapes=[pltpu.VMEM((2, s, dk), F32),
                        pltpu.VMEM((2, s, dk), F32),
                        pltpu.VMEM((s, dk), F32),
                        pltpu.VMEM((s, dk), BF16),
                        pltpu.VMEM((nc, dk, 2 * dk), F32),
                        pltpu.VMEM((nc, 1, 2 * dk), F32),
                        pltpu.VMEM((2, nc, dk, dk), BF16),
                        pltpu.VMEM((2, HG_CHUNK, 3 * HG_CHUNK), BF16),
                        pltpu.VMEM((2 * dk, 2 * dk), BF16),
                        pltpu.VMEM((pivot_rows, pivot_rows), F32),
                        pltpu.VMEM((direct_rows, 2 * dk), BF16),
                        pltpu.VMEM((direct_rows, 2 * dk), F32)],
        compiler_params=_params(("parallel", "parallel"), 40),
        name="hgrn2",
    )(proj, proj, proj, proj, lower_bounds.reshape(2 * depth, C_HEADS * dk))


def kernel(x, mix_norm, even_w_in, even_w_out, diff_lq1, diff_lk1, diff_lq2, diff_lk2,
           diff_subln, gmlp_ln_g, gmlp_ln_b, gmlp_w_s, gmlp_b_s, hgrn_w_in, hgrn_w_out,
           hgrn_lower_bounds, hgrn_g_norm, ffn_norm, ffn_w_gate, ffn_w_up, ffn_w_down,
           final_norm):
    b, s, d = x.shape
    depth = mix_norm.shape[0]
    m = b * s
    h = x.reshape(m, d)
    bf = lambda w: w.astype(BF16)
    for l in range(depth):
        if l % 2 == 0:
            e = l // 2
            proj = norm_proj(h, mix_norm[l], bf(even_w_in[e])).reshape(b, s, -1)
            lam_init = 0.8 - 0.6 * math.exp(-0.3 * l)
            lam_vec = jnp.stack([diff_lq1[e], diff_lk1[e], diff_lq2[e], diff_lk2[e]]).astype(F32)
            oa = diff_attention(proj, lam_vec, diff_subln[e], lam_init)
            u_block = (3 * A_HEADS * A_V_DIM) // (B_GROUPS * B_GROUP_DIM)
            ob = gmlp(proj, gmlp_ln_g[e], gmlp_ln_b[e], gmlp_w_s[e], gmlp_b_s[e], u_block)
            h = out_proj2(oa.reshape(m, -1), ob.reshape(m, -1), bf(even_w_out[e]), h)
        else:
            o = l // 2
            proj = norm_proj(h, mix_norm[l], bf(hgrn_w_in[o]))
            scan = hgrn2(proj.reshape(b, s, -1), hgrn_lower_bounds, l)
            h = out_proj_gated(scan.reshape(m, -1), proj, 4, hgrn_g_norm[o],
                               bf(hgrn_w_out[o]), h)
        last = l == depth - 1
        h = ffn(h, ffn_norm[l], bf(ffn_w_gate[l]), bf(ffn_w_up[l]), bf(ffn_w_down[l]),
                final_gain=final_norm if last else None)
    return h.reshape(b, s, d)
```

```python
import functools
import math

import jax
import jax.numpy as jnp
from jax import lax
from jax.experimental import pallas as pl
from jax.experimental.pallas import tpu as pltpu

F32 = jnp.float32
BF16 = jnp.bfloat16
EPS = 1e-6
MIB = 1024 * 1024

A_HEADS = 8
A_QK_DIM = 64
A_V_DIM = 128
ROPE_THETA = 500000.0
ROPE_DIM = A_QK_DIM // 4
B_GROUPS = 8
B_GROUP_DIM = 128
B_CHUNK = 128
C_HEADS = 16
C_DK = 128
HG_CHUNK = 64
HG_SUB = 16
HG_PAR = 2
HG_PREP = 4


def _params(semantics, vmem_mib):
    return pltpu.CompilerParams(dimension_semantics=semantics,
                                vmem_limit_bytes=vmem_mib * MIB)


def _rms(x, gain):
    ms = jnp.mean(x * x, axis=-1, keepdims=True)
    return x * lax.rsqrt(ms + EPS) * gain


def _sigmoid(x):
    return 1.0 / (1.0 + jnp.exp(-x))


def _gelu_tanh(x):
    c = math.sqrt(2.0 / math.pi)
    return x * (0.5 * (1.0 + jnp.tanh(c * (x + 0.044715 * (x * x * x)))))


def _dot(a, b):
    return jnp.dot(a, b, preferred_element_type=F32)


def _dot_nt(a, b):
    return lax.dot_general(a, b, (((1,), (1,)), ((), ())), preferred_element_type=F32)


def _dot_tn(a, b):
    return lax.dot_general(a, b, (((0,), (0,)), ((), ())), preferred_element_type=F32)


def _norm_proj_body(x_ref, g_ref, w_ref, o_ref, xn_ref):
    @pl.when(pl.program_id(1) == 0)
    def _():
        xn_ref[...] = _rms(x_ref[...], g_ref[...]).astype(BF16)

    o_ref[...] = _dot(xn_ref[...], w_ref[...])


def norm_proj(x, gain, w, tm=1024, tn=512):
    m, k = x.shape
    n = w.shape[1]
    tm, tn = min(tm, m), min(tn, n)
    return pl.pallas_call(
        _norm_proj_body,
        grid=(m // tm, n // tn),
        in_specs=[pl.BlockSpec((tm, k), lambda i, j: (i, 0)),
                  pl.BlockSpec((1, k), lambda i, j: (0, 0)),
                  pl.BlockSpec((k, tn), lambda i, j: (0, j))],
        out_specs=pl.BlockSpec((tm, tn), lambda i, j: (i, j)),
        out_shape=jax.ShapeDtypeStruct((m, n), F32),
        scratch_shapes=[pltpu.VMEM((tm, k), BF16)],
        compiler_params=_params(("parallel", "arbitrary"), 48),
        name="norm_proj",
    )(x, gain.reshape(1, k), w)


def _rope(x, cos, sin_lo, sin_hi):
    half = ROPE_DIM // 2
    return (x * cos + pltpu.roll(x, half, 1) * sin_lo
            + pltpu.roll(x, x.shape[1] - half, 1) * sin_hi)


def _attn_body(lam_ref, subln_ref, cos_ref, slo_ref, shi_ref, q_ref, k_ref, v_ref, o_ref,
               kb_ref, vb_ref, *, lam_init, tq):
    s_len = k_ref.shape[1]
    kb_ref[...] = _rope(k_ref[0], cos_ref[...], slo_ref[...], shi_ref[...]).astype(BF16)
    vb_ref[:, 0:A_V_DIM] = v_ref[0].astype(BF16)
    vb_ref[:, A_V_DIM:] = jnp.ones((s_len, A_V_DIM), BF16)
    lv = lam_ref[...]
    lam = (jnp.exp(jnp.sum(lv[0:1] * lv[1:2], axis=-1, keepdims=True))
           - jnp.exp(jnp.sum(lv[2:3] * lv[3:4], axis=-1, keepdims=True)) + lam_init)
    scale = A_QK_DIM ** -0.5 * math.log2(math.e)
    lane = lax.broadcasted_iota(jnp.int32, (tq, A_V_DIM), 1)
    for j in range(s_len // tq):
        rows = slice(j * tq, (j + 1) * tq)
        q = _rope(q_ref[0, rows, :], cos_ref[rows, :], slo_ref[rows, :], shi_ref[rows, :]) * scale
        q2 = jnp.concatenate([jnp.where(lane < A_QK_DIM, q, 0.0),
                              jnp.where(lane >= A_QK_DIM, q, 0.0)], axis=0).astype(BF16)
        s = _dot_nt(q2, kb_ref[...])
        p = jnp.exp2(s - jnp.max(s, axis=-1, keepdims=True))
        ov = _dot(p.astype(BF16), vb_ref[...])
        on = ov[:, 0:A_V_DIM] / ov[:, A_V_DIM:]
        o = on[0:tq] - lam * on[tq:]
        o_ref[0, rows, :] = (_rms(o, subln_ref[...]) * (1.0 - lam_init)).astype(BF16)


def _rope_tables(s):
    half = ROPE_DIM // 2
    pos = jnp.arange(s, dtype=F32)
    inv_freq = ROPE_THETA ** (-jnp.arange(half, dtype=F32) / half)
    ang = pos[:, None] * inv_freq[None, :]
    cos, sin = jnp.cos(ang), jnp.sin(ang)
    pad = A_QK_DIM - ROPE_DIM
    one = jnp.ones((s, pad), F32)
    zero = jnp.zeros((s, pad), F32)
    zh = jnp.zeros((s, half), F32)
    cos_c = jnp.concatenate([cos, cos, one], axis=1)
    lo_c = jnp.concatenate([zh, sin, zero], axis=1)
    hi_c = jnp.concatenate([-sin, zh, zero], axis=1)
    rep = lambda t: jnp.concatenate([t, t], axis=1)
    return rep(cos_c), rep(lo_c), rep(hi_c)


def diff_attention(proj, lam_vec, subln, lam_init, tq=256):
    b, s, _ = proj.shape
    tq = min(tq, s)
    cos, lo, hi = _rope_tables(s)
    hd = A_V_DIM
    table = pl.BlockSpec((s, hd), lambda bi, h: (0, 0))
    head = lambda off: pl.BlockSpec((1, s, hd), lambda bi, h: (bi, 0, off * A_HEADS + h))
    return pl.pallas_call(
        functools.partial(_attn_body, lam_init=lam_init, tq=tq),
        grid=(b, A_HEADS),
        in_specs=[pl.BlockSpec((4, A_QK_DIM), lambda bi, h: (0, 0)),
                  pl.BlockSpec((1, hd), lambda bi, h: (0, 0)),
                  table, table, table, head(0), head(1), head(2)],
        out_specs=pl.BlockSpec((1, s, hd), lambda bi, h: (bi, 0, h)),
        out_shape=jax.ShapeDtypeStruct((b, s, A_HEADS * hd), BF16),
        scratch_shapes=[pltpu.VMEM((s, hd), BF16), pltpu.VMEM((s, 2 * hd), BF16)],
        compiler_params=_params(("parallel", "parallel"), 56),
        name="diff_attention",
    )(lam_vec, subln.reshape(1, hd), cos, lo, hi, proj, proj, proj)


def _gmlp_body(u_ref, v_ref, lng_ref, lnb_ref, ws_ref, bs_ref, o_ref, *, n_chunks):
    gd = B_GROUP_DIM
    for c in range(n_chunks):
        rows = slice(c * B_CHUNK, (c + 1) * B_CHUNK)
        vb = _gelu_tanh(v_ref[0, rows, :])
        xc = vb - jnp.mean(vb, axis=-1, keepdims=True)
        var = jnp.mean(xc * xc, axis=-1, keepdims=True)
        y = (xc * lax.rsqrt(var + EPS) * lng_ref[...] + lnb_ref[...]).astype(BF16)
        u = _gelu_tanh(u_ref[0, rows, :])
        for g in range(B_GROUPS):
            cols = slice(g * gd, (g + 1) * gd)
            sv = _dot(ws_ref[g], y[:, cols]) + bs_ref[:, g:g + 1]
            o_ref[0, rows, cols] = (u[:, cols] * sv).astype(BF16)


def gmlp(proj, ln_g, ln_b, w_s, b_s, u_block, n_chunks=4):
    b, s, _ = proj.shape
    width = B_GROUPS * B_GROUP_DIM
    n_chunks = min(n_chunks, s // B_CHUNK)
    t = n_chunks * B_CHUNK
    return pl.pallas_call(
        functools.partial(_gmlp_body, n_chunks=n_chunks),
        grid=(b, s // t),
        in_specs=[pl.BlockSpec((1, t, width), lambda bi, n: (bi, n, u_block)),
                  pl.BlockSpec((1, t, width), lambda bi, n: (bi, n, u_block + 1)),
                  pl.BlockSpec((1, width), lambda bi, n: (0, 0)),
                  pl.BlockSpec((1, width), lambda bi, n: (0, 0)),
                  pl.BlockSpec((B_GROUPS, B_CHUNK, B_CHUNK), lambda bi, n: (0, 0, 0)),
                  pl.BlockSpec((B_CHUNK, B_GROUPS), lambda bi, n: (0, 0))],
        out_specs=pl.BlockSpec((1, t, width), lambda bi, n: (bi, n, 0)),
        out_shape=jax.ShapeDtypeStruct((b, s, width), BF16),
        compiler_params=_params(("parallel", "parallel"), 32),
        name="gmlp",
    )(proj, proj, ln_g.reshape(1, width), ln_b.reshape(1, width), w_s.astype(BF16), b_s.T)


def _out_proj2_body(a_ref, b_ref, w_ref, r_ref, o_ref):
    ka = a_ref.shape[1]
    acc = _dot(a_ref[...], w_ref[0:ka, :]) + _dot(b_ref[...], w_ref[ka:, :])
    o_ref[...] = r_ref[...] + acc


def out_proj2(a, b, w, res, tm=1024, tn=512):
    m, ka = a.shape
    kb = b.shape[1]
    n = w.shape[1]
    tm, tn = min(tm, m), min(tn, n)
    return pl.pallas_call(
        _out_proj2_body,
        grid=(m // tm, n // tn),
        in_specs=[pl.BlockSpec((tm, ka), lambda i, j: (i, 0)),
                  pl.BlockSpec((tm, kb), lambda i, j: (i, 0)),
                  pl.BlockSpec((ka + kb, tn), lambda i, j: (0, j)),
                  pl.BlockSpec((tm, tn), lambda i, j: (i, j))],
        out_specs=pl.BlockSpec((tm, tn), lambda i, j: (i, j)),
        out_shape=jax.ShapeDtypeStruct((m, n), F32),
        compiler_params=_params(("parallel", "parallel"), 48),
        name="out_proj_even",
    )(a, b, w, res)


def _out_proj_gated_body(o_in_ref, gate_ref, gn_ref, w_ref, r_ref, o_ref, y_ref):
    @pl.when(pl.program_id(1) == 0)
    def _():
        gate = gate_ref[...]
        y = _rms(o_in_ref[...], gn_ref[...]) * (gate * _sigmoid(gate))
        y_ref[...] = y.astype(BF16)

    o_ref[...] = r_ref[...] + _dot(y_ref[...], w_ref[...])


def out_proj_gated(o_in, proj, gate_block, gain, w, res, tm=512, tn=512):
    m, k = o_in.shape
    n = w.shape[1]
    tm, tn = min(tm, m), min(tn, n)
    return pl.pallas_call(
        _out_proj_gated_body,
        grid=(m // tm, n // tn),
        in_specs=[pl.BlockSpec((tm, k), lambda i, j: (i, 0)),
                  pl.BlockSpec((tm, k), lambda i, j: (i, gate_block)),
                  pl.BlockSpec((1, k), lambda i, j: (0, 0)),
                  pl.BlockSpec((k, tn), lambda i, j: (0, j)),
                  pl.BlockSpec((tm, tn), lambda i, j: (i, j))],
        out_specs=pl.BlockSpec((tm, tn), lambda i, j: (i, j)),
        out_shape=jax.ShapeDtypeStruct((m, n), F32),
        scratch_shapes=[pltpu.VMEM((tm, k), BF16)],
        compiler_params=_params(("parallel", "arbitrary"), 48),
        name="out_proj_odd",
    )(o_in, proj, gain.reshape(1, k), w, res)


def _ffn_body(*refs, final_norm):
    if final_norm:
        x_ref, g_ref, wg_ref, wu_ref, wd_ref, fn_ref, o_ref, xn_ref = refs
    else:
        x_ref, g_ref, wg_ref, wu_ref, wd_ref, o_ref, xn_ref = refs
    f = pl.program_id(1)

    @pl.when(f == 0)
    def _():
        x = x_ref[...]
        xn_ref[...] = _rms(x, g_ref[...]).astype(BF16)
        o_ref[...] = x

    xn = xn_ref[...]
    a = _dot(xn, wg_ref[...])
    u = _dot(xn, wu_ref[...])
    hid = (a * _sigmoid(a) * u).astype(BF16)
    o_ref[...] += _dot(hid, wd_ref[...])

    if final_norm:
        @pl.when(f == pl.num_programs(1) - 1)
        def _():
            o_ref[...] = _rms(o_ref[...], fn_ref[...])


def ffn(x, gain, w_gate, w_up, w_down, final_gain=None, tm=512, tf=512):
    m, k = x.shape
    hdim = w_gate.shape[1]
    tm, tf = min(tm, m), min(tf, hdim)
    vec = pl.BlockSpec((1, k), lambda i, f: (0, 0))
    in_specs = [pl.BlockSpec((tm, k), lambda i, f: (i, 0)), vec,
                pl.BlockSpec((k, tf), lambda i, f: (0, f)),
                pl.BlockSpec((k, tf), lambda i, f: (0, f)),
                pl.BlockSpec((tf, k), lambda i, f: (f, 0))]
    args = [x, gain.reshape(1, k), w_gate, w_up, w_down]
    if final_gain is not None:
        in_specs.append(vec)
        args.append(final_gain.reshape(1, k))
    return pl.pallas_call(
        functools.partial(_ffn_body, final_norm=final_gain is not None),
        grid=(m // tm, hdim // tf),
        in_specs=in_specs,
        out_specs=pl.BlockSpec((tm, k), lambda i, f: (i, 0)),
        out_shape=jax.ShapeDtypeStruct((m, k), F32),
        scratch_shapes=[pltpu.VMEM((tm, k), BF16)],
        compiler_params=_params(("parallel", "arbitrary"), 48),
        name="ffn",
    )(*args)


_HALF = 8


def _hgrn_tables(ti, carry, *, x_refs, lbs, q_ref, v_ref, tri_ref, bcum_ref, ckey_ref,
                 qs_ref, vb_ref, upd_ref, dec_ref):
    c = HG_CHUNK
    rows = [pl.multiple_of((ti * HG_PREP + u) * c, c) for u in range(HG_PREP)]
    kdec = [[None, None] for _ in range(HG_PREP)]
    decay = [[None, None] for _ in range(HG_PREP)]
    for d in range(2):
        lb = lbs[d]
        splits, log_k = [], []
        for r in rows:
            f = lb + (1.0 - lb) * _sigmoid(x_refs[d][0, pl.ds(r, c), :])
            logf = jnp.log(f)
            log_k.append(jnp.log(1.0 - f))
            hi = logf.astype(BF16)
            rem = logf - hi.astype(F32)
            mid = rem.astype(BF16)
            low = (rem - mid.astype(F32)).astype(BF16)
            splits.append(jnp.concatenate([hi, mid, low], axis=0))
        bc = _dot(tri_ref[d], jnp.concatenate(splits, axis=1))
        for u, r in enumerate(rows):
            bcum = bc[:, u * C_DK:(u + 1) * C_DK]
            ckey = bcum - log_k[u]
            bcum_ref[d, pl.ds(r, c), :] = bcum
            ckey_ref[d, pl.ds(r, c), :] = ckey
            edge = bcum[c - 1:c] if d == 0 else bcum[0:1]
            kdec[u][d] = jnp.exp(edge - ckey).astype(BF16)
            decay[u][d] = jnp.exp(edge)
    for u, r in enumerate(rows):
        ci = ti * HG_PREP + u
        vb = v_ref[0, pl.ds(r, c), :].astype(BF16)
        vb_ref[pl.ds(r, c), :] = vb
        upd_ref[ci] = _dot_tn(vb, jnp.concatenate(kdec[u], axis=1))
        dec_ref[ci] = jnp.concatenate(decay[u], axis=1)
        q = q_ref[0, pl.ds(r, c), :]
        qs_ref[pl.ds(r, c), :] = q * _sigmoid(q)
    return carry


def _hgrn_scan(ci, carry, *, upd_ref, dec_ref, sbf_ref, nc):
    st_f, st_b = carry
    cb = nc - 1 - ci
    sbf_ref[0, ci] = st_f.astype(BF16)
    sbf_ref[1, cb] = st_b.astype(BF16)
    st_f = st_f * dec_ref[ci, :, 0:C_DK] + upd_ref[ci, :, 0:C_DK]
    st_b = st_b * dec_ref[cb, :, C_DK:2 * C_DK] + upd_ref[cb, :, C_DK:2 * C_DK]
    return st_f, st_b


def _direct_pairs(reverse):
    pairs = []
    for s in range(HG_SUB):
        for h in range(2):
            sh = s // _HALF
            if not ((h > sh) if reverse else (h < sh)):
                pairs.append((s, h))
    return pairs


def _hgrn_direct_pieces(d, r0, *, bcum_ref, ckey_ref, qs_ref):
    reverse = d == 1
    t_idx = lax.broadcasted_iota(jnp.int32, (_HALF, C_DK), 0)
    b = [bcum_ref[d, pl.ds(r0 + _HALF * h, _HALF), :] for h in range(2)]
    q = [qs_ref[pl.ds(r0 + _HALF * h, _HALF), :] for h in range(2)]
    pieces = []
    for s, h in _direct_pairs(reverse):
        p = q[h] * jnp.exp(b[h] - ckey_ref[d, pl.ds(r0 + s, 1), :])
        if h == s // _HALF:
            t0 = s - _HALF * h
            p = jnp.where((t_idx <= t0) if reverse else (t_idx >= t0), p, 0.0)
        pieces.append(p)
    rows = len(pieces) // 2 * _HALF
    lhs = jnp.concatenate(pieces, axis=0).astype(BF16)
    return jnp.concatenate([lhs[:rows], lhs[rows:]], axis=1)


def _hgrn_direct_combine(d, r0, rs_ref, row0, *, v_ref):
    pairs = _direct_pairs(d == 1)
    n_half = len(pairs) // 2
    out = [jnp.zeros((_HALF, C_DK), F32), jnp.zeros((_HALF, C_DK), F32)]
    for p_idx, (s, h) in enumerate(pairs):
        col, row = divmod(p_idx, n_half)
        rp = rs_ref[row0 + row * _HALF:row0 + (row + 1) * _HALF, col * C_DK:(col + 1) * C_DK]
        out[h] = out[h] + rp * v_ref[0, pl.ds(r0 + s, 1), :]
    return jnp.concatenate(out, axis=0)


def _pivot_plan(reverse):
    s, h = HG_SUB, HG_CHUNK // 2
    assert HG_CHUNK == 4 * HG_SUB
    if reverse:
        return [(0, h, h, h, h), (0, s, s, s, s), (h, s, h + s, s, h + s)]
    return [(h, h, 0, h, h - 1), (s, s, 0, s, s - 1), (h + s, s, h, s, h + s - 1)]


def _hgrn_output(ti, carry, *, v_ref, o_ref, bcum_ref, ckey_ref, qs_ref, vb_ref, sbf_ref,
                 w2_ref, pmask_ref, lhs_ref, rs_ref):
    c, sub = HG_CHUNK, HG_SUB
    n_sub = c // sub
    blk_rows = len(_direct_pairs(False)) // 2 * _HALF
    rows = [pl.multiple_of((ti * HG_PAR + u) * c, c) for u in range(HG_PAR)]
    for u, r in enumerate(rows):
        for d in range(2):
            for i in range(n_sub):
                row0 = ((u * 2 + d) * n_sub + i) * blk_rows
                lhs_ref[row0:row0 + blk_rows, :] = _hgrn_direct_pieces(
                    d, r + i * sub, bcum_ref=bcum_ref, ckey_ref=ckey_ref, qs_ref=qs_ref)
    rs_ref[...] = _dot(lhs_ref[...], w2_ref[...])
    q_parts, k_parts, v_parts = [], [], []
    for r in rows:
        for d in range(2):
            for q0, qn, k0, kn, prow in _pivot_plan(d == 1):
                piv = bcum_ref[d, pl.ds(r + prow, 1), :]
                q_parts.append(qs_ref[pl.ds(r + q0, qn), :]
                               * jnp.exp(bcum_ref[d, pl.ds(r + q0, qn), :] - piv))
                k_parts.append(jnp.exp(piv - ckey_ref[d, pl.ds(r + k0, kn), :]))
                v_parts.append(vb_ref[pl.ds(r + k0, kn), :])
    sc = _dot_nt(jnp.concatenate(q_parts, axis=0).astype(BF16),
                 jnp.concatenate(k_parts, axis=0).astype(BF16)) * pmask_ref[...]
    pv = _dot(sc.astype(BF16), jnp.concatenate(v_parts, axis=0))
    for u, r in enumerate(rows):
        ci = ti * HG_PAR + u
        total = None
        for d in range(2):
            blocks = [_hgrn_direct_combine(d, r + i * sub, rs_ref,
                                           ((u * 2 + d) * n_sub + i) * blk_rows, v_ref=v_ref)
                      for i in range(n_sub)]
            off = (u * 2 + d) * c
            for q0, qn, _, _, _ in _pivot_plan(d == 1):
                for j in range(qn // sub):
                    blocks[q0 // sub + j] = (blocks[q0 // sub + j]
                                             + pv[off + j * sub:off + (j + 1) * sub])
                off += qn
            qe = (qs_ref[pl.ds(r, c), :] * jnp.exp(bcum_ref[d, pl.ds(r, c), :])).astype(BF16)
            o_d = _dot_nt(qe, sbf_ref[d, ci]) + jnp.concatenate(blocks, axis=0)
            total = o_d if total is None else total + o_d
        o_ref[0, pl.ds(r, c), :] = total
    return carry


def _hgrn_body(q_ref, ff_ref, fb_ref, v_ref, lbr_ref, o_ref,
               bcum_ref, ckey_ref, qs_ref, vb_ref, upd_ref, dec_ref, sbf_ref, tri_ref, w2_ref,
               pmask_ref, lhs_ref, rs_ref, *, depth, layer_idx):
    s = q_ref.shape[1]
    c = HG_CHUNK
    nc = s // c
    assert nc % HG_PAR == 0 and nc % HG_PREP == 0

    def lower_bound(d):
        rows = lbr_ref[d * depth:(d + 1) * depth, :]
        e = jnp.exp(rows - jnp.max(rows, axis=0, keepdims=True))
        sm = e / jnp.sum(e, axis=0, keepdims=True)
        acc = sm[0:1] * 0.0
        for l in range(1, layer_idx + 1):
            acc = acc + sm[l:l + 1]
        return acc

    row = lax.broadcasted_iota(jnp.int32, (c, 3 * c), 0)
    col = lax.broadcasted_iota(jnp.int32, (c, 3 * c), 1) % c
    tri_ref[0] = (col <= row).astype(BF16)
    tri_ref[1] = (col >= row).astype(BF16)
    wr = lax.broadcasted_iota(jnp.int32, (2 * C_DK, 2 * C_DK), 0) // C_DK
    wc = lax.broadcasted_iota(jnp.int32, (2 * C_DK, 2 * C_DK), 1) // C_DK
    w2_ref[...] = (wr == wc).astype(BF16)
    n = HG_PAR * 2 * c
    pr = lax.broadcasted_iota(jnp.int32, (n, n), 0)
    pc = lax.broadcasted_iota(jnp.int32, (n, n), 1)
    entry = lambda x: jnp.where(x % c < c // 2, 0, (x % c) // HG_SUB)
    pmask_ref[...] = ((pr // c == pc // c) & (entry(pr) == entry(pc))).astype(F32)

    lax.fori_loop(0, nc // HG_PREP, functools.partial(
        _hgrn_tables, x_refs=(ff_ref, fb_ref), lbs=(lower_bound(0), lower_bound(1)), q_ref=q_ref,
        v_ref=v_ref, tri_ref=tri_ref, bcum_ref=bcum_ref, ckey_ref=ckey_ref, qs_ref=qs_ref,
        vb_ref=vb_ref, upd_ref=upd_ref, dec_ref=dec_ref), 0)
    zero = jnp.zeros((C_DK, C_DK), F32)
    lax.fori_loop(0, nc, functools.partial(
        _hgrn_scan, upd_ref=upd_ref, dec_ref=dec_ref, sbf_ref=sbf_ref, nc=nc), (zero, zero))
    lax.fori_loop(0, nc // HG_PAR, functools.partial(
        _hgrn_output, v_ref=v_ref, o_ref=o_ref, bcum_ref=bcum_ref, ckey_ref=ckey_ref,
        qs_ref=qs_ref, vb_ref=vb_ref, sbf_ref=sbf_ref, w2_ref=w2_ref, pmask_ref=pmask_ref,
        lhs_ref=lhs_ref, rs_ref=rs_ref), 0)


def hgrn2(proj, lower_bounds, layer_idx):
    b, s, _ = proj.shape
    depth = lower_bounds.shape[1]
    dk = C_DK
    nc = s // HG_CHUNK
    direct_rows = HG_PAR * 2 * (HG_CHUNK // HG_SUB) * (len(_direct_pairs(False)) // 2 * _HALF)
    pivot_rows = HG_PAR * 2 * HG_CHUNK
    blk = lambda off: pl.BlockSpec((1, s, dk), lambda bi, h: (bi, 0, off * C_HEADS + h))
    return pl.pallas_call(
        functools.partial(_hgrn_body, depth=depth, layer_idx=layer_idx),
        grid=(b, C_HEADS),
        in_specs=[blk(0), blk(1), blk(2), blk(3),
                  pl.BlockSpec((2 * depth, dk), lambda bi, h: (0, h))],
        out_specs=pl.BlockSpec((1, s, dk), lambda bi, h: (bi, 0, h)),
        out_shape=jax.ShapeDtypeStruct((b, s, C_HEADS * dk), F32),
        scratch_shapes=[pltpu.VMEM((2, s, dk), F32),
                        pltpu.VMEM((2, s, dk), F32),
                        pltpu.VMEM((s, dk), F32),
                        pltpu.VMEM((s, dk), BF16),
                        pltpu.VMEM((nc, dk, 2 * dk), F32),
                        pltpu.VMEM((nc, 1, 2 * dk), F32),
                        pltpu.VMEM((2, nc, dk, dk), BF16),
                        pltpu.VMEM((2, HG_CHUNK, 3 * HG_CHUNK), BF16),
                        pltpu.VMEM((2 * dk, 2 * dk), BF16),
                        pltpu.VMEM((pivot_rows, pivot_rows), F32),
                        pltpu.VMEM((direct_rows, 2 * dk), BF16),
                        pltpu.VMEM((direct_rows, 2 * dk), F32)],
        compiler_params=_params(("parallel", "parallel"), 40),
        name="hgrn2",
    )(proj, proj, proj, proj, lower_bounds.reshape(2 * depth, C_HEADS * dk))


def kernel(x, mix_norm, even_w_in, even_w_out, diff_lq1, diff_lk1, diff_lq2, diff_lk2,
           diff_subln, gmlp_ln_g, gmlp_ln_b, gmlp_w_s, gmlp_b_s, hgrn_w_in, hgrn_w_out,
           hgrn_lower_bounds, hgrn_g_norm, ffn_norm, ffn_w_gate, ffn_w_up, ffn_w_down,
           final_norm):
    b, s, d = x.shape
    depth = mix_norm.shape[0]
    m = b * s
    h = x.reshape(m, d)
    bf = lambda w: w.astype(BF16)
    for l in range(depth):
        if l % 2 == 0:
            e = l // 2
            proj = norm_proj(h, mix_norm[l], bf(even_w_in[e])).reshape(b, s, -1)
            lam_init = 0.8 - 0.6 * math.exp(-0.3 * l)
            lam_vec = jnp.stack([diff_lq1[e], diff_lk1[e], diff_lq2[e], diff_lk2[e]]).astype(F32)
            oa = diff_attention(proj, lam_vec, diff_subln[e], lam_init)
            u_block = (3 * A_HEADS * A_V_DIM) // (B_GROUPS * B_GROUP_DIM)
            ob = gmlp(proj, gmlp_ln_g[e], gmlp_ln_b[e], gmlp_w_s[e], gmlp_b_s[e], u_block)
            h = out_proj2(oa.reshape(m, -1), ob.reshape(m, -1), bf(even_w_out[e]), h)
        else:
            o = l // 2
            proj = norm_proj(h, mix_norm[l], bf(hgrn_w_in[o]))
            scan = hgrn2(proj.reshape(b, s, -1), hgrn_lower_bounds, l)
            h = out_proj_gated(scan.reshape(m, -1), proj, 4, hgrn_g_norm[o],
                               bf(hgrn_w_out[o]), h)
        last = l == depth - 1
        h = ffn(h, ffn_norm[l], bf(ffn_w_gate[l]), bf(ffn_w_up[l]), bf(ffn_w_down[l]),
                final_gain=final_norm if last else None)
    return h.reshape(b, s, d)
```

```python
import functools
import math

import jax
import jax.numpy as jnp
from jax import lax
from jax.experimental import pallas as pl
from jax.experimental.pallas import tpu as pltpu

F32 = jnp.float32
BF16 = jnp.bfloat16
EPS = 1e-6
MIB = 1024 * 1024

A_HEADS = 8
A_QK_DIM = 64
A_V_DIM = 128
ROPE_THETA = 500000.0
ROPE_DIM = A_QK_DIM // 4
B_GROUPS = 8
B_GROUP_DIM = 128
B_CHUNK = 128
C_HEADS = 16
C_DK = 128
HG_CHUNK = 64
HG_SUB = 16
HG_PAR = 2
HG_PREP = 4


def _params(semantics, vmem_mib):
    return pltpu.CompilerParams(dimension_semantics=semantics,
                                vmem_limit_bytes=vmem_mib * MIB)


def _rms(x, gain):
    ms = jnp.mean(x * x, axis=-1, keepdims=True)
    return x * lax.rsqrt(ms + EPS) * gain


def _sigmoid(x):
    return 1.0 / (1.0 + jnp.exp(-x))


def _gelu_tanh(x):
    c = math.sqrt(2.0 / math.pi)
    return x * (0.5 * (1.0 + jnp.tanh(c * (x + 0.044715 * (x * x * x)))))


def _dot(a, b):
    return jnp.dot(a, b, preferred_element_type=F32)


def _dot_nt(a, b):
    return lax.dot_general(a, b, (((1,), (1,)), ((), ())), preferred_element_type=F32)


def _dot_tn(a, b):
    return lax.dot_general(a, b, (((0,), (0,)), ((), ())), preferred_element_type=F32)


def _norm_proj_body(x_ref, g_ref, w_ref, o_ref, xn_ref):
    @pl.when(pl.program_id(1) == 0)
    def _():
        xn_ref[...] = _rms(x_ref[...], g_ref[...]).astype(BF16)

    o_ref[...] = _dot(xn_ref[...], w_ref[...].astype(BF16))


def norm_proj(x, gain, w, layer, tm=1024, tn=512):
    m, k = x.shape
    n = w.shape[2]
    tm, tn = min(tm, m), min(tn, n)
    return pl.pallas_call(
        _norm_proj_body,
        grid=(m // tm, n // tn),
        in_specs=[pl.BlockSpec((tm, k), lambda i, j: (i, 0)),
                  pl.BlockSpec((1, k), lambda i, j: (0, 0)),
                  pl.BlockSpec((None, k, tn), lambda i, j: (layer, 0, j))],
        out_specs=pl.BlockSpec((tm, tn), lambda i, j: (i, j)),
        out_shape=jax.ShapeDtypeStruct((m, n), F32),
        scratch_shapes=[pltpu.VMEM((tm, k), BF16)],
        compiler_params=_params(("parallel", "arbitrary"), 48),
        name="norm_proj",
    )(x, gain.reshape(1, k), w)


def _rope(x, cos, sin_lo, sin_hi):
    half = ROPE_DIM // 2
    return (x * cos + pltpu.roll(x, half, 1) * sin_lo
            + pltpu.roll(x, x.shape[1] - half, 1) * sin_hi)


def _attn_body(lam_ref, subln_ref, cos_ref, slo_ref, shi_ref, q_ref, k_ref, v_ref, o_ref,
               kb_ref, vb_ref, *, lam_init, tq):
    s_len = k_ref.shape[1]
    kb_ref[...] = _rope(k_ref[0], cos_ref[...], slo_ref[...], shi_ref[...]).astype(BF16)
    vb_ref[:, 0:A_V_DIM] = v_ref[0].astype(BF16)
    vb_ref[:, A_V_DIM:] = jnp.ones((s_len, A_V_DIM), BF16)
    lv = lam_ref[...]
    lam = (jnp.exp(jnp.sum(lv[0:1] * lv[1:2], axis=-1, keepdims=True))
           - jnp.exp(jnp.sum(lv[2:3] * lv[3:4], axis=-1, keepdims=True)) + lam_init)
    scale = A_QK_DIM ** -0.5 * math.log2(math.e)
    lane = lax.broadcasted_iota(jnp.int32, (tq, A_V_DIM), 1)
    for j in range(s_len // tq):
        rows = slice(j * tq, (j + 1) * tq)
        q = _rope(q_ref[0, rows, :], cos_ref[rows, :], slo_ref[rows, :], shi_ref[rows, :]) * scale
        q2 = jnp.concatenate([jnp.where(lane < A_QK_DIM, q, 0.0),
                              jnp.where(lane >= A_QK_DIM, q, 0.0)], axis=0).astype(BF16)
        s = _dot_nt(q2, kb_ref[...])
        p = jnp.exp2(s - jnp.max(s, axis=-1, keepdims=True))
        ov = _dot(p.astype(BF16), vb_ref[...])
        on = ov[:, 0:A_V_DIM] / ov[:, A_V_DIM:]
        o = on[0:tq] - lam * on[tq:]
        o_ref[0, rows, :] = (_rms(o, subln_ref[...]) * (1.0 - lam_init)).astype(BF16)


def _rope_tables(s):
    half = ROPE_DIM // 2
    pos = jnp.arange(s, dtype=F32)
    inv_freq = ROPE_THETA ** (-jnp.arange(half, dtype=F32) / half)
    ang = pos[:, None] * inv_freq[None, :]
    cos, sin = jnp.cos(ang), jnp.sin(ang)
    pad = A_QK_DIM - ROPE_DIM
    one = jnp.ones((s, pad), F32)
    zero = jnp.zeros((s, pad), F32)
    zh = jnp.zeros((s, half), F32)
    cos_c = jnp.concatenate([cos, cos, one], axis=1)
    lo_c = jnp.concatenate([zh, sin, zero], axis=1)
    hi_c = jnp.concatenate([-sin, zh, zero], axis=1)
    rep = lambda t: jnp.concatenate([t, t], axis=1)
    return rep(cos_c), rep(lo_c), rep(hi_c)


def diff_attention(proj, lam_vec, subln, lam_init, tq=256):
    b, s, _ = proj.shape
    tq = min(tq, s)
    cos, lo, hi = _rope_tables(s)
    hd = A_V_DIM
    table = pl.BlockSpec((s, hd), lambda bi, h: (0, 0))
    head = lambda off: pl.BlockSpec((1, s, hd), lambda bi, h: (bi, 0, off * A_HEADS + h))
    return pl.pallas_call(
        functools.partial(_attn_body, lam_init=lam_init, tq=tq),
        grid=(b, A_HEADS),
        in_specs=[pl.BlockSpec((4, A_QK_DIM), lambda bi, h: (0, 0)),
                  pl.BlockSpec((1, hd), lambda bi, h: (0, 0)),
                  table, table, table, head(0), head(1), head(2)],
        out_specs=pl.BlockSpec((1, s, hd), lambda bi, h: (bi, 0, h)),
        out_shape=jax.ShapeDtypeStruct((b, s, A_HEADS * hd), BF16),
        scratch_shapes=[pltpu.VMEM((s, hd), BF16), pltpu.VMEM((s, 2 * hd), BF16)],
        compiler_params=_params(("parallel", "parallel"), 56),
        name="diff_attention",
    )(lam_vec, subln.reshape(1, hd), cos, lo, hi, proj, proj, proj)


def _gmlp_body(u_ref, v_ref, lng_ref, lnb_ref, ws_ref, bs_ref, o_ref, *, n_chunks):
    gd = B_GROUP_DIM
    for c in range(n_chunks):
        rows = slice(c * B_CHUNK, (c + 1) * B_CHUNK)
        vb = _gelu_tanh(v_ref[0, rows, :])
        xc = vb - jnp.mean(vb, axis=-1, keepdims=True)
        var = jnp.mean(xc * xc, axis=-1, keepdims=True)
        y = (xc * lax.rsqrt(var + EPS) * lng_ref[...] + lnb_ref[...]).astype(BF16)
        u = _gelu_tanh(u_ref[0, rows, :])
        for g in range(B_GROUPS):
            cols = slice(g * gd, (g + 1) * gd)
            sv = _dot(ws_ref[g], y[:, cols]) + bs_ref[:, g:g + 1]
            o_ref[0, rows, cols] = (u[:, cols] * sv).astype(BF16)


def gmlp(proj, ln_g, ln_b, w_s, b_s, u_block, n_chunks=4):
    b, s, _ = proj.shape
    width = B_GROUPS * B_GROUP_DIM
    n_chunks = min(n_chunks, s // B_CHUNK)
    t = n_chunks * B_CHUNK
    return pl.pallas_call(
        functools.partial(_gmlp_body, n_chunks=n_chunks),
        grid=(b, s // t),
        in_specs=[pl.BlockSpec((1, t, width), lambda bi, n: (bi, n, u_block)),
                  pl.BlockSpec((1, t, width), lambda bi, n: (bi, n, u_block + 1)),
                  pl.BlockSpec((1, width), lambda bi, n: (0, 0)),
                  pl.BlockSpec((1, width), lambda bi, n: (0, 0)),
                  pl.BlockSpec((B_GROUPS, B_CHUNK, B_CHUNK), lambda bi, n: (0, 0, 0)),
                  pl.BlockSpec((B_CHUNK, B_GROUPS), lambda bi, n: (0, 0))],
        out_specs=pl.BlockSpec((1, t, width), lambda bi, n: (bi, n, 0)),
        out_shape=jax.ShapeDtypeStruct((b, s, width), BF16),
        compiler_params=_params(("parallel", "parallel"), 32),
        name="gmlp",
    )(proj, proj, ln_g.reshape(1, width), ln_b.reshape(1, width), w_s.astype(BF16), b_s.T)


def _out_proj2_body(a_ref, b_ref, w_ref, r_ref, o_ref):
    ka = a_ref.shape[1]
    acc = (_dot(a_ref[...], w_ref[0:ka, :].astype(BF16))
           + _dot(b_ref[...], w_ref[ka:, :].astype(BF16)))
    o_ref[...] = r_ref[...] + acc


def out_proj2(a, b, w, layer, res, tm=1024, tn=512):
    m, ka = a.shape
    kb = b.shape[1]
    n = w.shape[2]
    tm, tn = min(tm, m), min(tn, n)
    return pl.pallas_call(
        _out_proj2_body,
        grid=(m // tm, n // tn),
        in_specs=[pl.BlockSpec((tm, ka), lambda i, j: (i, 0)),
                  pl.BlockSpec((tm, kb), lambda i, j: (i, 0)),
                  pl.BlockSpec((None, ka + kb, tn), lambda i, j: (layer, 0, j)),
                  pl.BlockSpec((tm, tn), lambda i, j: (i, j))],
        out_specs=pl.BlockSpec((tm, tn), lambda i, j: (i, j)),
        out_shape=jax.ShapeDtypeStruct((m, n), F32),
        compiler_params=_params(("parallel", "parallel"), 48),
        name="out_proj_even",
    )(a, b, w, res)


def _out_proj_gated_body(o_in_ref, gate_ref, gn_ref, w_ref, r_ref, o_ref, y_ref):
    @pl.when(pl.program_id(1) == 0)
    def _():
        gate = gate_ref[...]
        y = _rms(o_in_ref[...], gn_ref[...]) * (gate * _sigmoid(gate))
        y_ref[...] = y.astype(BF16)

    o_ref[...] = r_ref[...] + _dot(y_ref[...], w_ref[...].astype(BF16))


def out_proj_gated(o_in, proj, gate_block, gain, w, layer, res, tm=512, tn=1024):
    m, k = o_in.shape
    n = w.shape[2]
    tm, tn = min(tm, m), min(tn, n)
    return pl.pallas_call(
        _out_proj_gated_body,
        grid=(m // tm, n // tn),
        in_specs=[pl.BlockSpec((tm, k), lambda i, j: (i, 0)),
                  pl.BlockSpec((tm, k), lambda i, j: (i, gate_block)),
                  pl.BlockSpec((1, k), lambda i, j: (0, 0)),
                  pl.BlockSpec((None, k, tn), lambda i, j: (layer, 0, j)),
                  pl.BlockSpec((tm, tn), lambda i, j: (i, j))],
        out_specs=pl.BlockSpec((tm, tn), lambda i, j: (i, j)),
        out_shape=jax.ShapeDtypeStruct((m, n), F32),
        scratch_shapes=[pltpu.VMEM((tm, k), BF16)],
        compiler_params=_params(("parallel", "arbitrary"), 56),
        name="out_proj_odd",
    )(o_in, proj, gain.reshape(1, k), w, res)


def _ffn_body(*refs, final_norm):
    if final_norm:
        x_ref, g_ref, wg_ref, wu_ref, wd_ref, fn_ref, o_ref, xn_ref = refs
    else:
        x_ref, g_ref, wg_ref, wu_ref, wd_ref, o_ref, xn_ref = refs
    f = pl.program_id(1)

    @pl.when(f == 0)
    def _():
        x = x_ref[...]
        xn_ref[...] = _rms(x, g_ref[...]).astype(BF16)
        o_ref[...] = x

    xn = xn_ref[...]
    a = _dot(xn, wg_ref[...])
    u = _dot(xn, wu_ref[...])
    hid = (a * _sigmoid(a) * u).astype(BF16)
    o_ref[...] += _dot(hid, wd_ref[...])

    if final_norm:
        @pl.when(f == pl.num_programs(1) - 1)
        def _():
            o_ref[...] = _rms(o_ref[...], fn_ref[...])


def ffn(x, gain, w_gate, w_up, w_down, layer, final_gain=None, tm=512, tf=512):
    m, k = x.shape
    hdim = w_gate.shape[2]
    tm, tf = min(tm, m), min(tf, hdim)
    vec = pl.BlockSpec((1, k), lambda i, f: (0, 0))
    in_specs = [pl.BlockSpec((tm, k), lambda i, f: (i, 0)), vec,
                pl.BlockSpec((None, k, tf), lambda i, f: (layer, 0, f)),
                pl.BlockSpec((None, k, tf), lambda i, f: (layer, 0, f)),
                pl.BlockSpec((None, tf, k), lambda i, f: (layer, f, 0))]
    args = [x, gain.reshape(1, k), w_gate, w_up, w_down]
    if final_gain is not None:
        in_specs.append(vec)
        args.append(final_gain.reshape(1, k))
    return pl.pallas_call(
        functools.partial(_ffn_body, final_norm=final_gain is not None),
        grid=(m // tm, hdim // tf),
        in_specs=in_specs,
        out_specs=pl.BlockSpec((tm, k), lambda i, f: (i, 0)),
        out_shape=jax.ShapeDtypeStruct((m, k), F32),
        scratch_shapes=[pltpu.VMEM((tm, k), BF16)],
        compiler_params=_params(("parallel", "arbitrary"), 48),
        name="ffn",
    )(*args)


_HALF = 8


def _hgrn_tables(ti, carry, *, x_refs, lbs, q_ref, v_ref, tri_ref, bcum_ref, ckey_ref,
                 qs_ref, vb_ref, upd_ref, dec_ref):
    c = HG_CHUNK
    rows = [pl.multiple_of((ti * HG_PREP + u) * c, c) for u in range(HG_PREP)]
    kdec = [[None, None] for _ in range(HG_PREP)]
    decay = [[None, None] for _ in range(HG_PREP)]
    for d in range(2):
        lb = lbs[d]
        splits, log_k = [], []
        for r in rows:
            f = lb + (1.0 - lb) * _sigmoid(x_refs[d][0, pl.ds(r, c), :])
            logf = jnp.log(f)
            log_k.append(jnp.log(1.0 - f))
            hi = logf.astype(BF16)
            rem = logf - hi.astype(F32)
            mid = rem.astype(BF16)
            low = (rem - mid.astype(F32)).astype(BF16)
            splits.append(jnp.concatenate([hi, mid, low], axis=0))
        bc = _dot(tri_ref[d], jnp.concatenate(splits, axis=1))
        for u, r in enumerate(rows):
            bcum = bc[:, u * C_DK:(u + 1) * C_DK]
            ckey = bcum - log_k[u]
            bcum_ref[d, pl.ds(r, c), :] = bcum
            ckey_ref[d, pl.ds(r, c), :] = ckey
            edge = bcum[c - 1:c] if d == 0 else bcum[0:1]
            kdec[u][d] = jnp.exp(edge - ckey).astype(BF16)
            decay[u][d] = jnp.exp(edge)
    for u, r in enumerate(rows):
        ci = ti * HG_PREP + u
        vb = v_ref[0, pl.ds(r, c), :].astype(BF16)
        vb_ref[pl.ds(r, c), :] = vb
        upd_ref[ci] = _dot_tn(vb, jnp.concatenate(kdec[u], axis=1))
        dec_ref[ci] = jnp.concatenate(decay[u], axis=1)
        q = q_ref[0, pl.ds(r, c), :]
        qs_ref[pl.ds(r, c), :] = q * _sigmoid(q)
    return carry


def _hgrn_scan(ci, carry, *, upd_ref, dec_ref, sbf_ref, nc):
    st_f, st_b = carry
    cb = nc - 1 - ci
    sbf_ref[0, ci] = st_f.astype(BF16)
    sbf_ref[1, cb] = st_b.astype(BF16)
    st_f = st_f * dec_ref[ci, :, 0:C_DK] + upd_ref[ci, :, 0:C_DK]
    st_b = st_b * dec_ref[cb, :, C_DK:2 * C_DK] + upd_ref[cb, :, C_DK:2 * C_DK]
    return st_f, st_b


def _direct_pairs(reverse):
    pairs = []
    for s in range(HG_SUB):
        for h in range(2):
            sh = s // _HALF
            if not ((h > sh) if reverse else (h < sh)):
                pairs.append((s, h))
    return pairs


def _hgrn_direct_pieces(d, r0, *, bcum_ref, ckey_ref, qs_ref):
    reverse = d == 1
    t_idx = lax.broadcasted_iota(jnp.int32, (_HALF, C_DK), 0)
    b = [bcum_ref[d, pl.ds(r0 + _HALF * h, _HALF), :] for h in range(2)]
    q = [qs_ref[pl.ds(r0 + _HALF * h, _HALF), :] for h in range(2)]
    pieces = []
    for s, h in _direct_pairs(reverse):
        p = q[h] * jnp.exp(b[h] - ckey_ref[d, pl.ds(r0 + s, 1), :])
        if h == s // _HALF:
            t0 = s - _HALF * h
            p = jnp.where((t_idx <= t0) if reverse else (t_idx >= t0), p, 0.0)
        pieces.append(p)
    rows = len(pieces) // 2 * _HALF
    lhs = jnp.concatenate(pieces, axis=0).astype(BF16)
    return jnp.concatenate([lhs[:rows], lhs[rows:]], axis=1)


def _hgrn_direct_combine(d, r0, rs_ref, row0, *, v_ref):
    pairs = _direct_pairs(d == 1)
    n_half = len(pairs) // 2
    out = [jnp.zeros((_HALF, C_DK), F32), jnp.zeros((_HALF, C_DK), F32)]
    for p_idx, (s, h) in enumerate(pairs):
        col, row = divmod(p_idx, n_half)
        rp = rs_ref[row0 + row * _HALF:row0 + (row + 1) * _HALF, col * C_DK:(col + 1) * C_DK]
        out[h] = out[h] + rp * v_ref[0, pl.ds(r0 + s, 1), :]
    return jnp.concatenate(out, axis=0)


def _pivot_plan(reverse):
    s, h = HG_SUB, HG_CHUNK // 2
    assert HG_CHUNK == 4 * HG_SUB
    if reverse:
        return [(0, h, h, h, h), (0, s, s, s, s), (h, s, h + s, s, h + s)]
    return [(h, h, 0, h, h - 1), (s, s, 0, s, s - 1), (h + s, s, h, s, h + s - 1)]


def _hgrn_output(ti, carry, *, v_ref, o_ref, bcum_ref, ckey_ref, qs_ref, vb_ref, sbf_ref,
                 w2_ref, pmask_ref, lhs_ref, rs_ref):
    c, sub = HG_CHUNK, HG_SUB
    n_sub = c // sub
    blk_rows = len(_direct_pairs(False)) // 2 * _HALF
    rows = [pl.multiple_of((ti * HG_PAR + u) * c, c) for u in range(HG_PAR)]
    for u, r in enumerate(rows):
        for d in range(2):
            for i in range(n_sub):
                row0 = ((u * 2 + d) * n_sub + i) * blk_rows
                lhs_ref[row0:row0 + blk_rows, :] = _hgrn_direct_pieces(
                    d, r + i * sub, bcum_ref=bcum_ref, ckey_ref=ckey_ref, qs_ref=qs_ref)
    rs_ref[...] = _dot(lhs_ref[...], w2_ref[...])
    q_parts, k_parts, v_parts = [], [], []
    for r in rows:
        for d in range(2):
            for q0, qn, k0, kn, prow in _pivot_plan(d == 1):
                piv = bcum_ref[d, pl.ds(r + prow, 1), :]
                q_parts.append(qs_ref[pl.ds(r + q0, qn), :]
                               * jnp.exp(bcum_ref[d, pl.ds(r + q0, qn), :] - piv))
                k_parts.append(jnp.exp(piv - ckey_ref[d, pl.ds(r + k0, kn), :]))
                v_parts.append(vb_ref[pl.ds(r + k0, kn), :])
    sc = _dot_nt(jnp.concatenate(q_parts, axis=0).astype(BF16),
                 jnp.concatenate(k_parts, axis=0).astype(BF16)) * pmask_ref[...]
    pv = _dot(sc.astype(BF16), jnp.concatenate(v_parts, axis=0))
    for u, r in enumerate(rows):
        ci = ti * HG_PAR + u
        total = None
        for d in range(2):
            blocks = [_hgrn_direct_combine(d, r + i * sub, rs_ref,
                                           ((u * 2 + d) * n_sub + i) * blk_rows, v_ref=v_ref)
                      for i in range(n_sub)]
            off = (u * 2 + d) * c
            for q0, qn, _, _, _ in _pivot_plan(d == 1):
                for j in range(qn // sub):
                    blocks[q0 // sub + j] = (blocks[q0 // sub + j]
                                             + pv[off + j * sub:off + (j + 1) * sub])
                off += qn
            qe = (qs_ref[pl.ds(r, c), :] * jnp.exp(bcum_ref[d, pl.ds(r, c), :])).astype(BF16)
            o_d = _dot_nt(qe, sbf_ref[d, ci]) + jnp.concatenate(blocks, axis=0)
            total = o_d if total is None else total + o_d
        o_ref[0, pl.ds(r, c), :] = total
    return carry


def _hgrn_body(q_ref, ff_ref, fb_ref, v_ref, lbr_ref, o_ref,
               bcum_ref, ckey_ref, qs_ref, vb_ref, upd_ref, dec_ref, sbf_ref, tri_ref, w2_ref,
               pmask_ref, lhs_ref, rs_ref, *, depth, layer_idx):
    s = q_ref.shape[1]
    c = HG_CHUNK
    nc = s // c
    assert nc % HG_PAR == 0 and nc % HG_PREP == 0

    def lower_bound(d):
        rows = lbr_ref[d * depth:(d + 1) * depth, :]
        e = jnp.exp(rows - jnp.max(rows, axis=0, keepdims=True))
        sm = e / jnp.sum(e, axis=0, keepdims=True)
        acc = sm[0:1] * 0.0
        for l in range(1, layer_idx + 1):
            acc = acc + sm[l:l + 1]
        return acc

    row = lax.broadcasted_iota(jnp.int32, (c, 3 * c), 0)
    col = lax.broadcasted_iota(jnp.int32, (c, 3 * c), 1) % c
    tri_ref[0] = (col <= row).astype(BF16)
    tri_ref[1] = (col >= row).astype(BF16)
    wr = lax.broadcasted_iota(jnp.int32, (2 * C_DK, 2 * C_DK), 0) // C_DK
    wc = lax.broadcasted_iota(jnp.int32, (2 * C_DK, 2 * C_DK), 1) // C_DK
    w2_ref[...] = (wr == wc).astype(BF16)
    n = HG_PAR * 2 * c
    pr = lax.broadcasted_iota(jnp.int32, (n, n), 0)
    pc = lax.broadcasted_iota(jnp.int32, (n, n), 1)
    entry = lambda x: jnp.where(x % c < c // 2, 0, (x % c) // HG_SUB)
    pmask_ref[...] = ((pr // c == pc // c) & (entry(pr) == entry(pc))).astype(F32)

    lax.fori_loop(0, nc // HG_PREP, functools.partial(
        _hgrn_tables, x_refs=(ff_ref, fb_ref), lbs=(lower_bound(0), lower_bound(1)), q_ref=q_ref,
        v_ref=v_ref, tri_ref=tri_ref, bcum_ref=bcum_ref, ckey_ref=ckey_ref, qs_ref=qs_ref,
        vb_ref=vb_ref, upd_ref=upd_ref, dec_ref=dec_ref), 0)
    zero = jnp.zeros((C_DK, C_DK), F32)
    lax.fori_loop(0, nc, functools.partial(
        _hgrn_scan, upd_ref=upd_ref, dec_ref=dec_ref, sbf_ref=sbf_ref, nc=nc), (zero, zero))
    lax.fori_loop(0, nc // HG_PAR, functools.partial(
        _hgrn_output, v_ref=v_ref, o_ref=o_ref, bcum_ref=bcum_ref, ckey_ref=ckey_ref,
        qs_ref=qs_ref, vb_ref=vb_ref, sbf_ref=sbf_ref, w2_ref=w2_ref, pmask_ref=pmask_ref,
        lhs_ref=lhs_ref, rs_ref=rs_ref), 0)


def hgrn2(proj, lower_bounds, layer_idx):
    b, s, _ = proj.shape
    depth = lower_bounds.shape[1]
    dk = C_DK
    nc = s // HG_CHUNK
    direct_rows = HG_PAR * 2 * (HG_CHUNK // HG_SUB) * (len(_direct_pairs(False)) // 2 * _HALF)
    pivot_rows = HG_PAR * 2 * HG_CHUNK
    blk = lambda off: pl.BlockSpec((1, s, dk), lambda bi, h: (bi, 0, off * C_HEADS + h))
    return pl.pallas_call(
        functools.partial(_hgrn_body, depth=depth, layer_idx=layer_idx),
        grid=(b, C_HEADS),
        in_specs=[blk(0), blk(1), blk(2), blk(3),
                  pl.BlockSpec((2 * depth, dk), lambda bi, h: (0, h))],
        out_specs=pl.BlockSpec((1, s, dk), lambda bi, h: (bi, 0, h)),
        out_shape=jax.ShapeDtypeStruct((b, s, C_HEADS * dk), F32),
        scratch_shapes=[pltpu.VMEM((2, s, dk), F32),
                        pltpu.VMEM((2, s, dk), F32),
                        pltpu.VMEM((s, dk), F32),
                        pltpu.VMEM((s, dk), BF16),
                        pltpu.VMEM((nc, dk, 2 * dk), F32),
                        pltpu.VMEM((nc, 1, 2 * dk), F32),
                        pltpu.VMEM((2, nc, dk, dk), BF16),
                        pltpu.VMEM((2, HG_CHUNK, 3 * HG_CHUNK), BF16),
                        pltpu.VMEM((2 * dk, 2 * dk), BF16),
                        pltpu.VMEM((pivot_rows, pivot_rows), F32),
                        pltpu.VMEM((direct_rows, 2 * dk), BF16),
                        pltpu.VMEM((direct_rows, 2 * dk), F32)],
        compiler_params=_params(("parallel", "parallel"), 40),
        name="hgrn2",
    )(proj, proj, proj, proj, lower_bounds.reshape(2 * depth, C_HEADS * dk))


def kernel(x, mix_norm, even_w_in, even_w_out, diff_lq1, diff_lk1, diff_lq2, diff_lk2,
           diff_subln, gmlp_ln_g, gmlp_ln_b, gmlp_w_s, gmlp_b_s, hgrn_w_in, hgrn_w_out,
           hgrn_lower_bounds, hgrn_g_norm, ffn_norm, ffn_w_gate, ffn_w_up, ffn_w_down,
           final_norm):
    b, s, d = x.shape
    depth = mix_norm.shape[0]
    m = b * s
    h = x.reshape(m, d)
    w_gate, w_up, w_down = (w.astype(BF16) for w in (ffn_w_gate, ffn_w_up, ffn_w_down))
    for l in range(depth):
        if l % 2 == 0:
            e = l // 2
            proj = norm_proj(h, mix_norm[l], even_w_in, e).reshape(b, s, -1)
            lam_init = 0.8 - 0.6 * math.exp(-0.3 * l)
            lam_vec = jnp.stack([diff_lq1[e], diff_lk1[e], diff_lq2[e], diff_lk2[e]]).astype(F32)
            oa = diff_attention(proj, lam_vec, diff_subln[e], lam_init)
            u_block = (3 * A_HEADS * A_V_DIM) // (B_GROUPS * B_GROUP_DIM)
            ob = gmlp(proj, gmlp_ln_g[e], gmlp_ln_b[e], gmlp_w_s[e], gmlp_b_s[e], u_block)
            h = out_proj2(oa.reshape(m, -1), ob.reshape(m, -1), even_w_out, e, h)
        else:
            o = l // 2
            proj = norm_proj(h, mix_norm[l], hgrn_w_in, o)
            scan = hgrn2(proj.reshape(b, s, -1), hgrn_lower_bounds, l)
            h = out_proj_gated(scan.reshape(m, -1), proj, 4, hgrn_g_norm[o], hgrn_w_out, o, h)
        last = l == depth - 1
        h = ffn(h, ffn_norm[l], w_gate, w_up, w_down, l,
                final_gain=final_norm if last else None)
    return h.reshape(b, s, d)
```

```python
import functools
import math

import jax
import jax.numpy as jnp
from jax import lax
from jax.experimental import pallas as pl
from jax.experimental.pallas import tpu as pltpu

F32 = jnp.float32
BF16 = jnp.bfloat16
EPS = 1e-6
MIB = 1024 * 1024

A_HEADS = 8
A_QK_DIM = 64
A_V_DIM = 128
ROPE_THETA = 500000.0
ROPE_DIM = A_QK_DIM // 4
B_GROUPS = 8
B_GROUP_DIM = 128
B_CHUNK = 128
C_HEADS = 16
C_DK = 128
HG_CHUNK = 64
HG_SUB = 16
HG_PAR = 2
HG_PREP = 4


def _params(semantics, vmem_mib):
    return pltpu.CompilerParams(dimension_semantics=semantics,
                                vmem_limit_bytes=vmem_mib * MIB)


def _rms(x, gain):
    ms = jnp.mean(x * x, axis=-1, keepdims=True)
    return x * lax.rsqrt(ms + EPS) * gain


def _sigmoid(x):
    return 1.0 / (1.0 + jnp.exp(-x))


def _gelu_tanh(x):
    c = math.sqrt(2.0 / math.pi)
    return x * (0.5 * (1.0 + jnp.tanh(c * (x + 0.044715 * (x * x * x)))))


def _dot(a, b):
    return jnp.dot(a, b, preferred_element_type=F32)


def _dot_nt(a, b):
    return lax.dot_general(a, b, (((1,), (1,)), ((), ())), preferred_element_type=F32)


def _dot_tn(a, b):
    return lax.dot_general(a, b, (((0,), (0,)), ((), ())), preferred_element_type=F32)


def _norm_proj_body(x_ref, g_ref, w_ref, o_ref, xn_ref):
    @pl.when(pl.program_id(1) == 0)
    def _():
        xn_ref[...] = _rms(x_ref[...], g_ref[...]).astype(BF16)

    o_ref[...] = _dot(xn_ref[...], w_ref[...])


def norm_proj(x, gain, w, layer, tm=1024, tn=512):
    m, k = x.shape
    n = w.shape[2]
    tm, tn = min(tm, m), min(tn, n)
    return pl.pallas_call(
        _norm_proj_body,
        grid=(m // tm, n // tn),
        in_specs=[pl.BlockSpec((tm, k), lambda i, j: (i, 0)),
                  pl.BlockSpec((1, k), lambda i, j: (0, 0)),
                  pl.BlockSpec((None, k, tn), lambda i, j: (layer, 0, j))],
        out_specs=pl.BlockSpec((tm, tn), lambda i, j: (i, j)),
        out_shape=jax.ShapeDtypeStruct((m, n), F32),
        scratch_shapes=[pltpu.VMEM((tm, k), BF16)],
        compiler_params=_params(("parallel", "arbitrary"), 48),
        name="norm_proj",
    )(x, gain.reshape(1, k), w)


def _rope(x, cos, sin_lo, sin_hi):
    half = ROPE_DIM // 2
    return (x * cos + pltpu.roll(x, half, 1) * sin_lo
            + pltpu.roll(x, x.shape[1] - half, 1) * sin_hi)


def _attn_body(lam_ref, subln_ref, cos_ref, slo_ref, shi_ref, q_ref, k_ref, v_ref, o_ref,
               kb_ref, vb_ref, *, lam_init, tq):
    s_len = k_ref.shape[1]
    kb_ref[...] = _rope(k_ref[0], cos_ref[...], slo_ref[...], shi_ref[...]).astype(BF16)
    vb_ref[:, 0:A_V_DIM] = v_ref[0].astype(BF16)
    vb_ref[:, A_V_DIM:] = jnp.ones((s_len, A_V_DIM), BF16)
    lv = lam_ref[...]
    lam = (jnp.exp(jnp.sum(lv[0:1] * lv[1:2], axis=-1, keepdims=True))
           - jnp.exp(jnp.sum(lv[2:3] * lv[3:4], axis=-1, keepdims=True)) + lam_init)
    scale = A_QK_DIM ** -0.5 * math.log2(math.e)
    lane = lax.broadcasted_iota(jnp.int32, (tq, A_V_DIM), 1)
    for j in range(s_len // tq):
        rows = slice(j * tq, (j + 1) * tq)
        q = _rope(q_ref[0, rows, :], cos_ref[rows, :], slo_ref[rows, :], shi_ref[rows, :]) * scale
        q2 = jnp.concatenate([jnp.where(lane < A_QK_DIM, q, 0.0),
                              jnp.where(lane >= A_QK_DIM, q, 0.0)], axis=0).astype(BF16)
        s = _dot_nt(q2, kb_ref[...])
        p = jnp.exp2(s - jnp.max(s, axis=-1, keepdims=True))
        ov = _dot(p.astype(BF16), vb_ref[...])
        on = ov[:, 0:A_V_DIM] / ov[:, A_V_DIM:]
        o = on[0:tq] - lam * on[tq:]
        o_ref[0, rows, :] = (_rms(o, subln_ref[...]) * (1.0 - lam_init)).astype(BF16)


def _rope_tables(s):
    half = ROPE_DIM // 2
    pos = jnp.arange(s, dtype=F32)
    inv_freq = ROPE_THETA ** (-jnp.arange(half, dtype=F32) / half)
    ang = pos[:, None] * inv_freq[None, :]
    cos, sin = jnp.cos(ang), jnp.sin(ang)
    pad = A_QK_DIM - ROPE_DIM
    one = jnp.ones((s, pad), F32)
    zero = jnp.zeros((s, pad), F32)
    zh = jnp.zeros((s, half), F32)
    cos_c = jnp.concatenate([cos, cos, one], axis=1)
    lo_c = jnp.concatenate([zh, sin, zero], axis=1)
    hi_c = jnp.concatenate([-sin, zh, zero], axis=1)
    rep = lambda t: jnp.concatenate([t, t], axis=1)
    return rep(cos_c), rep(lo_c), rep(hi_c)


def diff_attention(proj, lam_vec, subln, lam_init, tq=256):
    b, s, _ = proj.shape
    tq = min(tq, s)
    cos, lo, hi = _rope_tables(s)
    hd = A_V_DIM
    table = pl.BlockSpec((s, hd), lambda bi, h: (0, 0))
    head = lambda off: pl.BlockSpec((1, s, hd), lambda bi, h: (bi, 0, off * A_HEADS + h))
    return pl.pallas_call(
        functools.partial(_attn_body, lam_init=lam_init, tq=tq),
        grid=(b, A_HEADS),
        in_specs=[pl.BlockSpec((4, A_QK_DIM), lambda bi, h: (0, 0)),
                  pl.BlockSpec((1, hd), lambda bi, h: (0, 0)),
                  table, table, table, head(0), head(1), head(2)],
        out_specs=pl.BlockSpec((1, s, hd), lambda bi, h: (bi, 0, h)),
        out_shape=jax.ShapeDtypeStruct((b, s, A_HEADS * hd), BF16),
        scratch_shapes=[pltpu.VMEM((s, hd), BF16), pltpu.VMEM((s, 2 * hd), BF16)],
        compiler_params=_params(("parallel", "parallel"), 56),
        name="diff_attention",
    )(lam_vec, subln.reshape(1, hd), cos, lo, hi, proj, proj, proj)


def _gmlp_body(u_ref, v_ref, lng_ref, lnb_ref, ws_ref, bs_ref, o_ref, *, n_chunks):
    gd = B_GROUP_DIM
    for c in range(n_chunks):
        rows = slice(c * B_CHUNK, (c + 1) * B_CHUNK)
        vb = _gelu_tanh(v_ref[0, rows, :])
        xc = vb - jnp.mean(vb, axis=-1, keepdims=True)
        var = jnp.mean(xc * xc, axis=-1, keepdims=True)
        y = (xc * lax.rsqrt(var + EPS) * lng_ref[...] + lnb_ref[...]).astype(BF16)
        u = _gelu_tanh(u_ref[0, rows, :])
        for g in range(B_GROUPS):
            cols = slice(g * gd, (g + 1) * gd)
            sv = _dot(ws_ref[g], y[:, cols]) + bs_ref[:, g:g + 1]
            o_ref[0, rows, cols] = (u[:, cols] * sv).astype(BF16)


def gmlp(proj, ln_g, ln_b, w_s, b_s, u_block, n_chunks=4):
    b, s, _ = proj.shape
    width = B_GROUPS * B_GROUP_DIM
    n_chunks = min(n_chunks, s // B_CHUNK)
    t = n_chunks * B_CHUNK
    return pl.pallas_call(
        functools.partial(_gmlp_body, n_chunks=n_chunks),
        grid=(b, s // t),
        in_specs=[pl.BlockSpec((1, t, width), lambda bi, n: (bi, n, u_block)),
                  pl.BlockSpec((1, t, width), lambda bi, n: (bi, n, u_block + 1)),
                  pl.BlockSpec((1, width), lambda bi, n: (0, 0)),
                  pl.BlockSpec((1, width), lambda bi, n: (0, 0)),
                  pl.BlockSpec((B_GROUPS, B_CHUNK, B_CHUNK), lambda bi, n: (0, 0, 0)),
                  pl.BlockSpec((B_CHUNK, B_GROUPS), lambda bi, n: (0, 0))],
        out_specs=pl.BlockSpec((1, t, width), lambda bi, n: (bi, n, 0)),
        out_shape=jax.ShapeDtypeStruct((b, s, width), BF16),
        compiler_params=_params(("parallel", "parallel"), 32),
        name="gmlp",
    )(proj, proj, ln_g.reshape(1, width), ln_b.reshape(1, width), w_s.astype(BF16), b_s.T)


def _out_proj2_body(a_ref, b_ref, w_ref, r_ref, o_ref):
    ka = a_ref.shape[1]
    acc = _dot(a_ref[...], w_ref[0:ka, :]) + _dot(b_ref[...], w_ref[ka:, :])
    o_ref[...] = r_ref[...] + acc


def out_proj2(a, b, w, layer, res, tm=1024, tn=512):
    m, ka = a.shape
    kb = b.shape[1]
    n = w.shape[2]
    tm, tn = min(tm, m), min(tn, n)
    return pl.pallas_call(
        _out_proj2_body,
        grid=(m // tm, n // tn),
        in_specs=[pl.BlockSpec((tm, ka), lambda i, j: (i, 0)),
                  pl.BlockSpec((tm, kb), lambda i, j: (i, 0)),
                  pl.BlockSpec((None, ka + kb, tn), lambda i, j: (layer, 0, j)),
                  pl.BlockSpec((tm, tn), lambda i, j: (i, j))],
        out_specs=pl.BlockSpec((tm, tn), lambda i, j: (i, j)),
        out_shape=jax.ShapeDtypeStruct((m, n), F32),
        compiler_params=_params(("parallel", "parallel"), 48),
        name="out_proj_even",
    )(a, b, w, res)


def _out_proj_gated_body(o_in_ref, gate_ref, gn_ref, w_ref, r_ref, o_ref, y_ref):
    @pl.when(pl.program_id(1) == 0)
    def _():
        gate = gate_ref[...]
        y = _rms(o_in_ref[...], gn_ref[...]) * (gate * _sigmoid(gate))
        y_ref[...] = y.astype(BF16)

    o_ref[...] = r_ref[...] + _dot(y_ref[...], w_ref[...])


def out_proj_gated(o_in, proj, gate_block, gain, w, layer, res, tm=512, tn=1024):
    m, k = o_in.shape
    n = w.shape[2]
    tm, tn = min(tm, m), min(tn, n)
    return pl.pallas_call(
        _out_proj_gated_body,
        grid=(m // tm, n // tn),
        in_specs=[pl.BlockSpec((tm, k), lambda i, j: (i, 0)),
                  pl.BlockSpec((tm, k), lambda i, j: (i, gate_block)),
                  pl.BlockSpec((1, k), lambda i, j: (0, 0)),
                  pl.BlockSpec((None, k, tn), lambda i, j: (layer, 0, j)),
                  pl.BlockSpec((tm, tn), lambda i, j: (i, j))],
        out_specs=pl.BlockSpec((tm, tn), lambda i, j: (i, j)),
        out_shape=jax.ShapeDtypeStruct((m, n), F32),
        scratch_shapes=[pltpu.VMEM((tm, k), BF16)],
        compiler_params=_params(("parallel", "arbitrary"), 56),
        name="out_proj_odd",
    )(o_in, proj, gain.reshape(1, k), w, res)


def _ffn_body(*refs, final_norm):
    if final_norm:
        x_ref, g_ref, wg_ref, wu_ref, wd_ref, fn_ref, o_ref, xn_ref = refs
    else:
        x_ref, g_ref, wg_ref, wu_ref, wd_ref, o_ref, xn_ref = refs
    f = pl.program_id(1)

    @pl.when(f == 0)
    def _():
        x = x_ref[...]
        xn_ref[...] = _rms(x, g_ref[...]).astype(BF16)
        o_ref[...] = x

    xn = xn_ref[...]
    a = _dot(xn, wg_ref[...])
    u = _dot(xn, wu_ref[...])
    hid = (a * _sigmoid(a) * u).astype(BF16)
    o_ref[...] += _dot(hid, wd_ref[...])

    if final_norm:
        @pl.when(f == pl.num_programs(1) - 1)
        def _():
            o_ref[...] = _rms(o_ref[...], fn_ref[...])


def ffn(x, gain, w_gate, w_up, w_down, layer, final_gain=None, tm=512, tf=512):
    m, k = x.shape
    hdim = w_gate.shape[2]
    tm, tf = min(tm, m), min(tf, hdim)
    vec = pl.BlockSpec((1, k), lambda i, f: (0, 0))
    in_specs = [pl.BlockSpec((tm, k), lambda i, f: (i, 0)), vec,
                pl.BlockSpec((None, k, tf), lambda i, f: (layer, 0, f)),
                pl.BlockSpec((None, k, tf), lambda i, f: (layer, 0, f)),
                pl.BlockSpec((None, tf, k), lambda i, f: (layer, f, 0))]
    args = [x, gain.reshape(1, k), w_gate, w_up, w_down]
    if final_gain is not None:
        in_specs.append(vec)
        args.append(final_gain.reshape(1, k))
    return pl.pallas_call(
        functools.partial(_ffn_body, final_norm=final_gain is not None),
        grid=(m // tm, hdim // tf),
        in_specs=in_specs,
        out_specs=pl.BlockSpec((tm, k), lambda i, f: (i, 0)),
        out_shape=jax.ShapeDtypeStruct((m, k), F32),
        scratch_shapes=[pltpu.VMEM((tm, k), BF16)],
        compiler_params=_params(("parallel", "arbitrary"), 48),
        name="ffn",
    )(*args)


_HALF = 8


def _hgrn_tables(ti, carry, *, x_refs, lbs, q_ref, v_ref, tri_ref, bcum_ref, ckey_ref,
                 qs_ref, vb_ref, upd_ref, dec_ref):
    c = HG_CHUNK
    rows = [pl.multiple_of((ti * HG_PREP + u) * c, c) for u in range(HG_PREP)]
    kdec = [[None, None] for _ in range(HG_PREP)]
    decay = [[None, None] for _ in range(HG_PREP)]
    for d in range(2):
        lb = lbs[d]
        splits, log_k = [], []
        for r in rows:
            f = lb + (1.0 - lb) * _sigmoid(x_refs[d][0, pl.ds(r, c), :])
            logf = jnp.log(f)
            log_k.append(jnp.log(1.0 - f))
            hi = logf.astype(BF16)
            rem = logf - hi.astype(F32)
            mid = rem.astype(BF16)
            low = (rem - mid.astype(F32)).astype(BF16)
            splits.append(jnp.concatenate([hi, mid, low], axis=0))
        bc = _dot(tri_ref[d], jnp.concatenate(splits, axis=1))
        for u, r in enumerate(rows):
            bcum = bc[:, u * C_DK:(u + 1) * C_DK]
            ckey = bcum - log_k[u]
            bcum_ref[d, pl.ds(r, c), :] = bcum
            ckey_ref[d, pl.ds(r, c), :] = ckey
            edge = bcum[c - 1:c] if d == 0 else bcum[0:1]
            kdec[u][d] = jnp.exp(edge - ckey).astype(BF16)
            decay[u][d] = jnp.exp(edge)
    for u, r in enumerate(rows):
        ci = ti * HG_PREP + u
        vb = v_ref[0, pl.ds(r, c), :].astype(BF16)
        vb_ref[pl.ds(r, c), :] = vb
        upd_ref[ci] = _dot_tn(vb, jnp.concatenate(kdec[u], axis=1))
        dec_ref[ci] = jnp.concatenate(decay[u], axis=1)
        q = q_ref[0, pl.ds(r, c), :]
        qs_ref[pl.ds(r, c), :] = q * _sigmoid(q)
    return carry


def _hgrn_scan(ci, carry, *, upd_ref, dec_ref, sbf_ref, nc):
    st_f, st_b = carry
    cb = nc - 1 - ci
    sbf_ref[0, ci] = st_f.astype(BF16)
    sbf_ref[1, cb] = st_b.astype(BF16)
    st_f = st_f * dec_ref[ci, :, 0:C_DK] + upd_ref[ci, :, 0:C_DK]
    st_b = st_b * dec_ref[cb, :, C_DK:2 * C_DK] + upd_ref[cb, :, C_DK:2 * C_DK]
    return st_f, st_b


def _direct_pairs(reverse):
    pairs = []
    for s in range(HG_SUB):
        for h in range(2):
            sh = s // _HALF
            if not ((h > sh) if reverse else (h < sh)):
                pairs.append((s, h))
    return pairs


def _hgrn_direct_pieces(d, r0, *, bcum_ref, ckey_ref, qs_ref):
    reverse = d == 1
    t_idx = lax.broadcasted_iota(jnp.int32, (_HALF, C_DK), 0)
    b = [bcum_ref[d, pl.ds(r0 + _HALF * h, _HALF), :] for h in range(2)]
    q = [qs_ref[pl.ds(r0 + _HALF * h, _HALF), :] for h in range(2)]
    pieces = []
    for s, h in _direct_pairs(reverse):
        p = q[h] * jnp.exp(b[h] - ckey_ref[d, pl.ds(r0 + s, 1), :])
        if h == s // _HALF:
            t0 = s - _HALF * h
            p = jnp.where((t_idx <= t0) if reverse else (t_idx >= t0), p, 0.0)
        pieces.append(p)
    rows = len(pieces) // 2 * _HALF
    lhs = jnp.concatenate(pieces, axis=0).astype(BF16)
    return jnp.concatenate([lhs[:rows], lhs[rows:]], axis=1)


def _hgrn_direct_combine(d, r0, rs_ref, row0, *, v_ref):
    pairs = _direct_pairs(d == 1)
    n_half = len(pairs) // 2
    out = [jnp.zeros((_HALF, C_DK), F32), jnp.zeros((_HALF, C_DK), F32)]
    for p_idx, (s, h) in enumerate(pairs):
        col, row = divmod(p_idx, n_half)
        rp = rs_ref[row0 + row * _HALF:row0 + (row + 1) * _HALF, col * C_DK:(col + 1) * C_DK]
        out[h] = out[h] + rp * v_ref[0, pl.ds(r0 + s, 1), :]
    return jnp.concatenate(out, axis=0)


def _pivot_plan(reverse):
    s, h = HG_SUB, HG_CHUNK // 2
    assert HG_CHUNK == 4 * HG_SUB
    if reverse:
        return [(0, h, h, h, h), (0, s, s, s, s), (h, s, h + s, s, h + s)]
    return [(h, h, 0, h, h - 1), (s, s, 0, s, s - 1), (h + s, s, h, s, h + s - 1)]


def _hgrn_output(ti, carry, *, v_ref, o_ref, bcum_ref, ckey_ref, qs_ref, vb_ref, sbf_ref,
                 w2_ref, pmask_ref, lhs_ref, rs_ref):
    c, sub = HG_CHUNK, HG_SUB
    n_sub = c // sub
    blk_rows = len(_direct_pairs(False)) // 2 * _HALF
    rows = [pl.multiple_of((ti * HG_PAR + u) * c, c) for u in range(HG_PAR)]
    for u, r in enumerate(rows):
        for d in range(2):
            for i in range(n_sub):
                row0 = ((u * 2 + d) * n_sub + i) * blk_rows
                lhs_ref[row0:row0 + blk_rows, :] = _hgrn_direct_pieces(
                    d, r + i * sub, bcum_ref=bcum_ref, ckey_ref=ckey_ref, qs_ref=qs_ref)
    rs_ref[...] = _dot(lhs_ref[...], w2_ref[...])
    q_parts, k_parts, v_parts = [], [], []
    for r in rows:
        for d in range(2):
            for q0, qn, k0, kn, prow in _pivot_plan(d == 1):
                piv = bcum_ref[d, pl.ds(r + prow, 1), :]
                q_parts.append(qs_ref[pl.ds(r + q0, qn), :]
                               * jnp.exp(bcum_ref[d, pl.ds(r + q0, qn), :] - piv))
                k_parts.append(jnp.exp(piv - ckey_ref[d, pl.ds(r + k0, kn), :]))
                v_parts.append(vb_ref[pl.ds(r + k0, kn), :])
    sc = _dot_nt(jnp.concatenate(q_parts, axis=0).astype(BF16),
                 jnp.concatenate(k_parts, axis=0).astype(BF16)) * pmask_ref[...]
    pv = _dot(sc.astype(BF16), jnp.concatenate(v_parts, axis=0))
    for u, r in enumerate(rows):
        ci = ti * HG_PAR + u
        total = None
        for d in range(2):
            blocks = [_hgrn_direct_combine(d, r + i * sub, rs_ref,
                                           ((u * 2 + d) * n_sub + i) * blk_rows, v_ref=v_ref)
                      for i in range(n_sub)]
            off = (u * 2 + d) * c
            for q0, qn, _, _, _ in _pivot_plan(d == 1):
                for j in range(qn // sub):
                    blocks[q0 // sub + j] = (blocks[q0 // sub + j]
                                             + pv[off + j * sub:off + (j + 1) * sub])
                off += qn
            qe = (qs_ref[pl.ds(r, c), :] * jnp.exp(bcum_ref[d, pl.ds(r, c), :])).astype(BF16)
            o_d = _dot_nt(qe, sbf_ref[d, ci]) + jnp.concatenate(blocks, axis=0)
            total = o_d if total is None else total + o_d
        o_ref[0, pl.ds(r, c), :] = total
    return carry


def _hgrn_body(q_ref, ff_ref, fb_ref, v_ref, lbr_ref, o_ref,
               bcum_ref, ckey_ref, qs_ref, vb_ref, upd_ref, dec_ref, sbf_ref, tri_ref, w2_ref,
               pmask_ref, lhs_ref, rs_ref, *, depth, layer_idx):
    s = q_ref.shape[1]
    c = HG_CHUNK
    nc = s // c
    assert nc % HG_PAR == 0 and nc % HG_PREP == 0

    def lower_bound(d):
        rows = lbr_ref[d * depth:(d + 1) * depth, :]
        e = jnp.exp(rows - jnp.max(rows, axis=0, keepdims=True))
        sm = e / jnp.sum(e, axis=0, keepdims=True)
        acc = sm[0:1] * 0.0
        for l in range(1, layer_idx + 1):
            acc = acc + sm[l:l + 1]
        return acc

    row = lax.broadcasted_iota(jnp.int32, (c, 3 * c), 0)
    col = lax.broadcasted_iota(jnp.int32, (c, 3 * c), 1) % c
    tri_ref[0] = (col <= row).astype(BF16)
    tri_ref[1] = (col >= row).astype(BF16)
    wr = lax.broadcasted_iota(jnp.int32, (2 * C_DK, 2 * C_DK), 0) // C_DK
    wc = lax.broadcasted_iota(jnp.int32, (2 * C_DK, 2 * C_DK), 1) // C_DK
    w2_ref[...] = (wr == wc).astype(BF16)
    n = HG_PAR * 2 * c
    pr = lax.broadcasted_iota(jnp.int32, (n, n), 0)
    pc = lax.broadcasted_iota(jnp.int32, (n, n), 1)
    entry = lambda x: jnp.where(x % c < c // 2, 0, (x % c) // HG_SUB)
    pmask_ref[...] = ((pr // c == pc // c) & (entry(pr) == entry(pc))).astype(F32)

    lax.fori_loop(0, nc // HG_PREP, functools.partial(
        _hgrn_tables, x_refs=(ff_ref, fb_ref), lbs=(lower_bound(0), lower_bound(1)), q_ref=q_ref,
        v_ref=v_ref, tri_ref=tri_ref, bcum_ref=bcum_ref, ckey_ref=ckey_ref, qs_ref=qs_ref,
        vb_ref=vb_ref, upd_ref=upd_ref, dec_ref=dec_ref), 0)
    zero = jnp.zeros((C_DK, C_DK), F32)
    lax.fori_loop(0, nc, functools.partial(
        _hgrn_scan, upd_ref=upd_ref, dec_ref=dec_ref, sbf_ref=sbf_ref, nc=nc), (zero, zero))
    lax.fori_loop(0, nc // HG_PAR, functools.partial(
        _hgrn_output, v_ref=v_ref, o_ref=o_ref, bcum_ref=bcum_ref, ckey_ref=ckey_ref,
        qs_ref=qs_ref, vb_ref=vb_ref, sbf_ref=sbf_ref, w2_ref=w2_ref, pmask_ref=pmask_ref,
        lhs_ref=lhs_ref, rs_ref=rs_ref), 0)


def hgrn2(proj, lower_bounds, layer_idx):
    b, s, _ = proj.shape
    depth = lower_bounds.shape[1]
    dk = C_DK
    nc = s // HG_CHUNK
    direct_rows = HG_PAR * 2 * (HG_CHUNK // HG_SUB) * (len(_direct_pairs(False)) // 2 * _HALF)
    pivot_rows = HG_PAR * 2 * HG_CHUNK
    blk = lambda off: pl.BlockSpec((1, s, dk), lambda bi, h: (bi, 0, off * C_HEADS + h))
    return pl.pallas_call(
        functools.partial(_hgrn_body, depth=depth, layer_idx=layer_idx),
        grid=(b, C_HEADS),
        in_specs=[blk(0), blk(1), blk(2), blk(3),
                  pl.BlockSpec((2 * depth, dk), lambda bi, h: (0, h))],
        out_specs=pl.BlockSpec((1, s, dk), lambda bi, h: (bi, 0, h)),
        out_shape=jax.ShapeDtypeStruct((b, s, C_HEADS * dk), F32),
        scratch_shapes=[pltpu.VMEM((2, s, dk), F32),
                        pltpu.VMEM((2, s, dk), F32),
                        pltpu.VMEM((s, dk), F32),
                        pltpu.VMEM((s, dk), BF16),
                        pltpu.VMEM((nc, dk, 2 * dk), F32),
                        pltpu.VMEM((nc, 1, 2 * dk), F32),
                        pltpu.VMEM((2, nc, dk, dk), BF16),
                        pltpu.VMEM((2, HG_CHUNK, 3 * HG_CHUNK), BF16),
                        pltpu.VMEM((2 * dk, 2 * dk), BF16),
                        pltpu.VMEM((pivot_rows, pivot_rows), F32),
                        pltpu.VMEM((direct_rows, 2 * dk), BF16),
                        pltpu.VMEM((direct_rows, 2 * dk), F32)],
        compiler_params=_params(("parallel", "parallel"), 40),
        name="hgrn2",
    )(proj, proj, proj, proj, lower_bounds.reshape(2 * depth, C_HEADS * dk))


def kernel(x, mix_norm, even_w_in, even_w_out, diff_lq1, diff_lk1, diff_lq2, diff_lk2,
           diff_subln, gmlp_ln_g, gmlp_ln_b, gmlp_w_s, gmlp_b_s, hgrn_w_in, hgrn_w_out,
           hgrn_lower_bounds, hgrn_g_norm, ffn_norm, ffn_w_gate, ffn_w_up, ffn_w_down,
           final_norm):
    b, s, d = x.shape
    depth = mix_norm.shape[0]
    m = b * s
    h = x.reshape(m, d)
    (even_w_in, even_w_out, hgrn_w_in, hgrn_w_out, w_gate, w_up, w_down) = (
        w.astype(BF16) for w in (even_w_in, even_w_out, hgrn_w_in, hgrn_w_out,
                                 ffn_w_gate, ffn_w_up, ffn_w_down))
    for l in range(depth):
        if l % 2 == 0:
            e = l // 2
            proj = norm_proj(h, mix_norm[l], even_w_in, e).reshape(b, s, -1)
            lam_init = 0.8 - 0.6 * math.exp(-0.3 * l)
            lam_vec = jnp.stack([diff_lq1[e], diff_lk1[e], diff_lq2[e], diff_lk2[e]]).astype(F32)
            oa = diff_attention(proj, lam_vec, diff_subln[e], lam_init)
            u_block = (3 * A_HEADS * A_V_DIM) // (B_GROUPS * B_GROUP_DIM)
            ob = gmlp(proj, gmlp_ln_g[e], gmlp_ln_b[e], gmlp_w_s[e], gmlp_b_s[e], u_block)
            h = out_proj2(oa.reshape(m, -1), ob.reshape(m, -1), even_w_out, e, h)
        else:
            o = l // 2
            proj = norm_proj(h, mix_norm[l], hgrn_w_in, o)
            scan = hgrn2(proj.reshape(b, s, -1), hgrn_lower_bounds, l)
            h = out_proj_gated(scan.reshape(m, -1), proj, 4, hgrn_g_norm[o], hgrn_w_out, o, h)
        last = l == depth - 1
        h = ffn(h, ffn_norm[l], w_gate, w_up, w_down, l,
                final_gain=final_norm if last else None)
    return h.reshape(b, s, d)
```

```python
import functools
import math

import jax
import jax.numpy as jnp
from jax import lax
from jax.experimental import pallas as pl
from jax.experimental.pallas import tpu as pltpu

F32 = jnp.float32
BF16 = jnp.bfloat16
EPS = 1e-6
MIB = 1024 * 1024

A_HEADS = 8
A_QK_DIM = 64
A_V_DIM = 128
ROPE_THETA = 500000.0
ROPE_DIM = A_QK_DIM // 4
B_GROUPS = 8
B_GROUP_DIM = 128
B_CHUNK = 128
C_HEADS = 16
C_DK = 128
HG_CHUNK = 64
HG_SUB = 16
HG_PAR = 2
HG_PREP = 4


def _params(semantics, vmem_mib):
    return pltpu.CompilerParams(dimension_semantics=semantics,
                                vmem_limit_bytes=vmem_mib * MIB)


def _rms(x, gain):
    ms = jnp.mean(x * x, axis=-1, keepdims=True)
    return x * lax.rsqrt(ms + EPS) * gain


def _sigmoid(x):
    return 1.0 / (1.0 + jnp.exp(-x))


def _gelu_tanh(x):
    c = math.sqrt(2.0 / math.pi)
    return x * (0.5 * (1.0 + jnp.tanh(c * (x + 0.044715 * (x * x * x)))))


def _dot(a, b):
    return jnp.dot(a, b, preferred_element_type=F32)


def _dot_nt(a, b):
    return lax.dot_general(a, b, (((1,), (1,)), ((), ())), preferred_element_type=F32)


def _dot_tn(a, b):
    return lax.dot_general(a, b, (((0,), (0,)), ((), ())), preferred_element_type=F32)


def _norm_proj_body(x_ref, g_ref, w_ref, o_ref, xn_ref):
    @pl.when(pl.program_id(1) == 0)
    def _():
        xn_ref[...] = _rms(x_ref[...], g_ref[...]).astype(BF16)

    o_ref[...] = _dot(xn_ref[...], w_ref[...])


def norm_proj(x, gain, w, layer, tm=1024, tn=1024):
    m, k = x.shape
    n = w.shape[2]
    tm, tn = min(tm, m), min(tn, n)
    return pl.pallas_call(
        _norm_proj_body,
        grid=(m // tm, n // tn),
        in_specs=[pl.BlockSpec((tm, k), lambda i, j: (i, 0)),
                  pl.BlockSpec((1, k), lambda i, j: (0, 0)),
                  pl.BlockSpec((None, k, tn), lambda i, j: (layer, 0, j))],
        out_specs=pl.BlockSpec((tm, tn), lambda i, j: (i, j)),
        out_shape=jax.ShapeDtypeStruct((m, n), F32),
        scratch_shapes=[pltpu.VMEM((tm, k), BF16)],
        compiler_params=_params(("parallel", "arbitrary"), 48),
        name="norm_proj",
    )(x, gain.reshape(1, k), w)


def _rope(x, cos, sin_lo, sin_hi):
    half = ROPE_DIM // 2
    return (x * cos + pltpu.roll(x, half, 1) * sin_lo
            + pltpu.roll(x, x.shape[1] - half, 1) * sin_hi)


def _attn_body(lam_ref, subln_ref, cos_ref, slo_ref, shi_ref, q_ref, k_ref, v_ref, o_ref,
               kb_ref, vb_ref, *, lam_init, tq):
    s_len = k_ref.shape[1]
    kb_ref[...] = _rope(k_ref[0], cos_ref[...], slo_ref[...], shi_ref[...]).astype(BF16)
    vb_ref[:, 0:A_V_DIM] = v_ref[0].astype(BF16)
    vb_ref[:, A_V_DIM:] = jnp.ones((s_len, A_V_DIM), BF16)
    lv = lam_ref[...]
    lam = (jnp.exp(jnp.sum(lv[0:1] * lv[1:2], axis=-1, keepdims=True))
           - jnp.exp(jnp.sum(lv[2:3] * lv[3:4], axis=-1, keepdims=True)) + lam_init)
    scale = A_QK_DIM ** -0.5 * math.log2(math.e)
    lane = lax.broadcasted_iota(jnp.int32, (tq, A_V_DIM), 1)
    for j in range(s_len // tq):
        rows = slice(j * tq, (j + 1) * tq)
        q = _rope(q_ref[0, rows, :], cos_ref[rows, :], slo_ref[rows, :], shi_ref[rows, :]) * scale
        q2 = jnp.concatenate([jnp.where(lane < A_QK_DIM, q, 0.0),
                              jnp.where(lane >= A_QK_DIM, q, 0.0)], axis=0).astype(BF16)
        s = _dot_nt(q2, kb_ref[...])
        p = jnp.exp2(s - jnp.max(s, axis=-1, keepdims=True))
        ov = _dot(p.astype(BF16), vb_ref[...])
        on = ov[:, 0:A_V_DIM] / ov[:, A_V_DIM:]
        o = on[0:tq] - lam * on[tq:]
        o_ref[0, rows, :] = (_rms(o, subln_ref[...]) * (1.0 - lam_init)).astype(BF16)


def _rope_tables(s):
    half = ROPE_DIM // 2
    pos = jnp.arange(s, dtype=F32)
    inv_freq = ROPE_THETA ** (-jnp.arange(half, dtype=F32) / half)
    ang = pos[:, None] * inv_freq[None, :]
    cos, sin = jnp.cos(ang), jnp.sin(ang)
    pad = A_QK_DIM - ROPE_DIM
    one = jnp.ones((s, pad), F32)
    zero = jnp.zeros((s, pad), F32)
    zh = jnp.zeros((s, half), F32)
    cos_c = jnp.concatenate([cos, cos, one], axis=1)
    lo_c = jnp.concatenate([zh, sin, zero], axis=1)
    hi_c = jnp.concatenate([-sin, zh, zero], axis=1)
    rep = lambda t: jnp.concatenate([t, t], axis=1)
    return rep(cos_c), rep(lo_c), rep(hi_c)


def diff_attention(proj, lam_vec, subln, lam_init, tq=256):
    b, s, _ = proj.shape
    tq = min(tq, s)
    cos, lo, hi = _rope_tables(s)
    hd = A_V_DIM
    table = pl.BlockSpec((s, hd), lambda bi, h: (0, 0))
    head = lambda off: pl.BlockSpec((1, s, hd), lambda bi, h: (bi, 0, off * A_HEADS + h))
    return pl.pallas_call(
        functools.partial(_attn_body, lam_init=lam_init, tq=tq),
        grid=(b, A_HEADS),
        in_specs=[pl.BlockSpec((4, A_QK_DIM), lambda bi, h: (0, 0)),
                  pl.BlockSpec((1, hd), lambda bi, h: (0, 0)),
                  table, table, table, head(0), head(1), head(2)],
        out_specs=pl.BlockSpec((1, s, hd), lambda bi, h: (bi, 0, h)),
        out_shape=jax.ShapeDtypeStruct((b, s, A_HEADS * hd), BF16),
        scratch_shapes=[pltpu.VMEM((s, hd), BF16), pltpu.VMEM((s, 2 * hd), BF16)],
        compiler_params=_params(("parallel", "parallel"), 56),
        name="diff_attention",
    )(lam_vec, subln.reshape(1, hd), cos, lo, hi, proj, proj, proj)


def _gmlp_body(u_ref, v_ref, lng_ref, lnb_ref, ws_ref, bs_ref, o_ref, *, n_chunks):
    gd = B_GROUP_DIM
    for c in range(n_chunks):
        rows = slice(c * B_CHUNK, (c + 1) * B_CHUNK)
        vb = _gelu_tanh(v_ref[0, rows, :])
        xc = vb - jnp.mean(vb, axis=-1, keepdims=True)
        var = jnp.mean(xc * xc, axis=-1, keepdims=True)
        y = (xc * lax.rsqrt(var + EPS) * lng_ref[...] + lnb_ref[...]).astype(BF16)
        u = _gelu_tanh(u_ref[0, rows, :])
        for g in range(B_GROUPS):
            cols = slice(g * gd, (g + 1) * gd)
            sv = _dot(ws_ref[g], y[:, cols]) + bs_ref[:, g:g + 1]
            o_ref[0, rows, cols] = (u[:, cols] * sv).astype(BF16)


def gmlp(proj, ln_g, ln_b, w_s, b_s, u_block, n_chunks=4):
    b, s, _ = proj.shape
    width = B_GROUPS * B_GROUP_DIM
    n_chunks = min(n_chunks, s // B_CHUNK)
    t = n_chunks * B_CHUNK
    return pl.pallas_call(
        functools.partial(_gmlp_body, n_chunks=n_chunks),
        grid=(b, s // t),
        in_specs=[pl.BlockSpec((1, t, width), lambda bi, n: (bi, n, u_block)),
                  pl.BlockSpec((1, t, width), lambda bi, n: (bi, n, u_block + 1)),
                  pl.BlockSpec((1, width), lambda bi, n: (0, 0)),
                  pl.BlockSpec((1, width), lambda bi, n: (0, 0)),
                  pl.BlockSpec((B_GROUPS, B_CHUNK, B_CHUNK), lambda bi, n: (0, 0, 0)),
                  pl.BlockSpec((B_CHUNK, B_GROUPS), lambda bi, n: (0, 0))],
        out_specs=pl.BlockSpec((1, t, width), lambda bi, n: (bi, n, 0)),
        out_shape=jax.ShapeDtypeStruct((b, s, width), BF16),
        compiler_params=_params(("parallel", "parallel"), 32),
        name="gmlp",
    )(proj, proj, ln_g.reshape(1, width), ln_b.reshape(1, width), w_s.astype(BF16), b_s.T)


def _out_proj2_body(a_ref, b_ref, w_ref, r_ref, o_ref):
    ka = a_ref.shape[1]
    acc = _dot(a_ref[...], w_ref[0:ka, :]) + _dot(b_ref[...], w_ref[ka:, :])
    o_ref[...] = r_ref[...] + acc


def out_proj2(a, b, w, layer, res, tm=1024, tn=512):
    m, ka = a.shape
    kb = b.shape[1]
    n = w.shape[2]
    tm, tn = min(tm, m), min(tn, n)
    return pl.pallas_call(
        _out_proj2_body,
        grid=(m // tm, n // tn),
        in_specs=[pl.BlockSpec((tm, ka), lambda i, j: (i, 0)),
                  pl.BlockSpec((tm, kb), lambda i, j: (i, 0)),
                  pl.BlockSpec((None, ka + kb, tn), lambda i, j: (layer, 0, j)),
                  pl.BlockSpec((tm, tn), lambda i, j: (i, j))],
        out_specs=pl.BlockSpec((tm, tn), lambda i, j: (i, j)),
        out_shape=jax.ShapeDtypeStruct((m, n), F32),
        compiler_params=_params(("parallel", "parallel"), 48),
        name="out_proj_even",
    )(a, b, w, res)


def _out_proj_gated_body(o_in_ref, gate_ref, gn_ref, w_ref, r_ref, o_ref, y_ref):
    @pl.when(pl.program_id(1) == 0)
    def _():
        gate = gate_ref[...]
        y = _rms(o_in_ref[...], gn_ref[...]) * (gate * _sigmoid(gate))
        y_ref[...] = y.astype(BF16)

    o_ref[...] = r_ref[...] + _dot(y_ref[...], w_ref[...])


def out_proj_gated(o_in, proj, gate_block, gain, w, layer, res, tm=512, tn=2048):
    m, k = o_in.shape
    n = w.shape[2]
    tm, tn = min(tm, m), min(tn, n)
    return pl.pallas_call(
        _out_proj_gated_body,
        grid=(m // tm, n // tn),
        in_specs=[pl.BlockSpec((tm, k), lambda i, j: (i, 0)),
                  pl.BlockSpec((tm, k), lambda i, j: (i, gate_block)),
                  pl.BlockSpec((1, k), lambda i, j: (0, 0)),
                  pl.BlockSpec((None, k, tn), lambda i, j: (layer, 0, j)),
                  pl.BlockSpec((tm, tn), lambda i, j: (i, j))],
        out_specs=pl.BlockSpec((tm, tn), lambda i, j: (i, j)),
        out_shape=jax.ShapeDtypeStruct((m, n), F32),
        scratch_shapes=[pltpu.VMEM((tm, k), BF16)],
        compiler_params=_params(("parallel", "arbitrary"), 56),
        name="out_proj_odd",
    )(o_in, proj, gain.reshape(1, k), w, res)


def _ffn_body(*refs, final_norm):
    if final_norm:
        x_ref, g_ref, wg_ref, wu_ref, wd_ref, fn_ref, o_ref, xn_ref = refs
    else:
        x_ref, g_ref, wg_ref, wu_ref, wd_ref, o_ref, xn_ref = refs
    f = pl.program_id(1)

    @pl.when(f == 0)
    def _():
        x = x_ref[...]
        xn_ref[...] = _rms(x, g_ref[...]).astype(BF16)
        o_ref[...] = x

    xn = xn_ref[...]
    a = _dot(xn, wg_ref[...])
    u = _dot(xn, wu_ref[...])
    hid = (a * _sigmoid(a) * u).astype(BF16)
    o_ref[...] += _dot(hid, wd_ref[...])

    if final_norm:
        @pl.when(f == pl.num_programs(1) - 1)
        def _():
            o_ref[...] = _rms(o_ref[...], fn_ref[...])


def ffn(x, gain, w_gate, w_up, w_down, layer, final_gain=None, tm=1024, tf=512):
    m, k = x.shape
    hdim = w_gate.shape[2]
    tm, tf = min(tm, m), min(tf, hdim)
    vec = pl.BlockSpec((1, k), lambda i, f: (0, 0))
    in_specs = [pl.BlockSpec((tm, k), lambda i, f: (i, 0)), vec,
                pl.BlockSpec((None, k, tf), lambda i, f: (layer, 0, f)),
                pl.BlockSpec((None, k, tf), lambda i, f: (layer, 0, f)),
                pl.BlockSpec((None, tf, k), lambda i, f: (layer, f, 0))]
    args = [x, gain.reshape(1, k), w_gate, w_up, w_down]
    if final_gain is not None:
        in_specs.append(vec)
        args.append(final_gain.reshape(1, k))
    return pl.pallas_call(
        functools.partial(_ffn_body, final_norm=final_gain is not None),
        grid=(m // tm, hdim // tf),
        in_specs=in_specs,
        out_specs=pl.BlockSpec((tm, k), lambda i, f: (i, 0)),
        out_shape=jax.ShapeDtypeStruct((m, k), F32),
        scratch_shapes=[pltpu.VMEM((tm, k), BF16)],
        compiler_params=_params(("parallel", "arbitrary"), 60),
        name="ffn",
    )(*args)


_HALF = 8


def _hgrn_tables(ti, carry, *, x_refs, lbs, q_ref, v_ref, tri_ref, bcum_ref, ckey_ref,
                 qs_ref, vb_ref, upd_ref, dec_ref):
    c = HG_CHUNK
    rows = [pl.multiple_of((ti * HG_PREP + u) * c, c) for u in range(HG_PREP)]
    kdec = [[None, None] for _ in range(HG_PREP)]
    decay = [[None, None] for _ in range(HG_PREP)]
    for d in range(2):
        lb = lbs[d]
        splits, log_k = [], []
        for r in rows:
            f = lb + (1.0 - lb) * _sigmoid(x_refs[d][0, pl.ds(r, c), :])
            logf = jnp.log(f)
            log_k.append(jnp.log(1.0 - f))
            hi = logf.astype(BF16)
            rem = logf - hi.astype(F32)
            mid = rem.astype(BF16)
            low = (rem - mid.astype(F32)).astype(BF16)
            splits.append(jnp.concatenate([hi, mid, low], axis=0))
        bc = _dot(tri_ref[d], jnp.concatenate(splits, axis=1))
        for u, r in enumerate(rows):
            bcum = bc[:, u * C_DK:(u + 1) * C_DK]
            ckey = bcum - log_k[u]
            bcum_ref[d, pl.ds(r, c), :] = bcum
            ckey_ref[d, pl.ds(r, c), :] = ckey
            edge = bcum[c - 1:c] if d == 0 else bcum[0:1]
            kdec[u][d] = jnp.exp(edge - ckey).astype(BF16)
            decay[u][d] = jnp.exp(edge)
    for u, r in enumerate(rows):
        ci = ti * HG_PREP + u
        vb = v_ref[0, pl.ds(r, c), :].astype(BF16)
        vb_ref[pl.ds(r, c), :] = vb
        upd_ref[ci] = _dot_tn(vb, jnp.concatenate(kdec[u], axis=1))
        dec_ref[ci] = jnp.concatenate(decay[u], axis=1)
        q = q_ref[0, pl.ds(r, c), :]
        qs_ref[pl.ds(r, c), :] = q * _sigmoid(q)
    return carry


def _hgrn_scan(ci, carry, *, upd_ref, dec_ref, sbf_ref, nc):
    st_f, st_b = carry
    cb = nc - 1 - ci
    sbf_ref[0, ci] = st_f.astype(BF16)
    sbf_ref[1, cb] = st_b.astype(BF16)
    st_f = st_f * dec_ref[ci, :, 0:C_DK] + upd_ref[ci, :, 0:C_DK]
    st_b = st_b * dec_ref[cb, :, C_DK:2 * C_DK] + upd_ref[cb, :, C_DK:2 * C_DK]
    return st_f, st_b


def _direct_pairs(reverse):
    pairs = []
    for s in range(HG_SUB):
        for h in range(2):
            sh = s // _HALF
            if not ((h > sh) if reverse else (h < sh)):
                pairs.append((s, h))
    return pairs


def _hgrn_direct_pieces(d, r0, *, bcum_ref, ckey_ref, qs_ref):
    reverse = d == 1
    t_idx = lax.broadcasted_iota(jnp.int32, (_HALF, C_DK), 0)
    b = [bcum_ref[d, pl.ds(r0 + _HALF * h, _HALF), :] for h in range(2)]
    q = [qs_ref[pl.ds(r0 + _HALF * h, _HALF), :] for h in range(2)]
    pieces = []
    for s, h in _direct_pairs(reverse):
        p = q[h] * jnp.exp(b[h] - ckey_ref[d, pl.ds(r0 + s, 1), :])
        if h == s // _HALF:
            t0 = s - _HALF * h
            p = jnp.where((t_idx <= t0) if reverse else (t_idx >= t0), p, 0.0)
        pieces.append(p)
    rows = len(pieces) // 2 * _HALF
    lhs = jnp.concatenate(pieces, axis=0).astype(BF16)
    return jnp.concatenate([lhs[:rows], lhs[rows:]], axis=1)


def _hgrn_direct_combine(d, r0, rs_ref, row0, *, v_ref):
    pairs = _direct_pairs(d == 1)
    n_half = len(pairs) // 2
    out = [jnp.zeros((_HALF, C_DK), F32), jnp.zeros((_HALF, C_DK), F32)]
    for p_idx, (s, h) in enumerate(pairs):
        col, row = divmod(p_idx, n_half)
        rp = rs_ref[row0 + row * _HALF:row0 + (row + 1) * _HALF, col * C_DK:(col + 1) * C_DK]
        out[h] = out[h] + rp * v_ref[0, pl.ds(r0 + s, 1), :]
    return jnp.concatenate(out, axis=0)


def _pivot_plan(reverse):
    s, h = HG_SUB, HG_CHUNK // 2
    assert HG_CHUNK == 4 * HG_SUB
    if reverse:
        return [(0, h, h, h, h), (0, s, s, s, s), (h, s, h + s, s, h + s)]
    return [(h, h, 0, h, h - 1), (s, s, 0, s, s - 1), (h + s, s, h, s, h + s - 1)]


def _hgrn_output(ti, carry, *, v_ref, o_ref, bcum_ref, ckey_ref, qs_ref, vb_ref, sbf_ref,
                 w2_ref, pmask_ref, lhs_ref, rs_ref):
    c, sub = HG_CHUNK, HG_SUB
    n_sub = c // sub
    blk_rows = len(_direct_pairs(False)) // 2 * _HALF
    rows = [pl.multiple_of((ti * HG_PAR + u) * c, c) for u in range(HG_PAR)]
    for u, r in enumerate(rows):
        for d in range(2):
            for i in range(n_sub):
                row0 = ((u * 2 + d) * n_sub + i) * blk_rows
                lhs_ref[row0:row0 + blk_rows, :] = _hgrn_direct_pieces(
                    d, r + i * sub, bcum_ref=bcum_ref, ckey_ref=ckey_ref, qs_ref=qs_ref)
    rs_ref[...] = _dot(lhs_ref[...], w2_ref[...])
    q_parts, k_parts, v_parts = [], [], []
    for r in rows:
        for d in range(2):
            for q0, qn, k0, kn, prow in _pivot_plan(d == 1):
                piv = bcum_ref[d, pl.ds(r + prow, 1), :]
                q_parts.append(qs_ref[pl.ds(r + q0, qn), :]
                               * jnp.exp(bcum_ref[d, pl.ds(r + q0, qn), :] - piv))
                k_parts.append(jnp.exp(piv - ckey_ref[d, pl.ds(r + k0, kn), :]))
                v_parts.append(vb_ref[pl.ds(r + k0, kn), :])
    sc = _dot_nt(jnp.concatenate(q_parts, axis=0).astype(BF16),
                 jnp.concatenate(k_parts, axis=0).astype(BF16)) * pmask_ref[...]
    pv = _dot(sc.astype(BF16), jnp.concatenate(v_parts, axis=0))
    for u, r in enumerate(rows):
        ci = ti * HG_PAR + u
        total = None
        for d in range(2):
            blocks = [_hgrn_direct_combine(d, r + i * sub, rs_ref,
                                           ((u * 2 + d) * n_sub + i) * blk_rows, v_ref=v_ref)
                      for i in range(n_sub)]
            off = (u * 2 + d) * c
            for q0, qn, _, _, _ in _pivot_plan(d == 1):
                for j in range(qn // sub):
                    blocks[q0 // sub + j] = (blocks[q0 // sub + j]
                                             + pv[off + j * sub:off + (j + 1) * sub])
                off += qn
            qe = (qs_ref[pl.ds(r, c), :] * jnp.exp(bcum_ref[d, pl.ds(r, c), :])).astype(BF16)
            o_d = _dot_nt(qe, sbf_ref[d, ci]) + jnp.concatenate(blocks, axis=0)
            total = o_d if total is None else total + o_d
        o_ref[0, pl.ds(r, c), :] = total
    return carry


def _hgrn_body(q_ref, ff_ref, fb_ref, v_ref, lbr_ref, o_ref,
               bcum_ref, ckey_ref, qs_ref, vb_ref, upd_ref, dec_ref, sbf_ref, tri_ref, w2_ref,
               pmask_ref, lhs_ref, rs_ref, *, depth, layer_idx):
    s = q_ref.shape[1]
    c = HG_CHUNK
    nc = s // c
    assert nc % HG_PAR == 0 and nc % HG_PREP == 0

    def lower_bound(d):
        rows = lbr_ref[d * depth:(d + 1) * depth, :]
        e = jnp.exp(rows - jnp.max(rows, axis=0, keepdims=True))
        sm = e / jnp.sum(e, axis=0, keepdims=True)
        acc = sm[0:1] * 0.0
        for l in range(1, layer_idx + 1):
            acc = acc + sm[l:l + 1]
        return acc

    row = lax.broadcasted_iota(jnp.int32, (c, 3 * c), 0)
    col = lax.broadcasted_iota(jnp.int32, (c, 3 * c), 1) % c
    tri_ref[0] = (col <= row).astype(BF16)
    tri_ref[1] = (col >= row).astype(BF16)
    wr = lax.broadcasted_iota(jnp.int32, (2 * C_DK, 2 * C_DK), 0) // C_DK
    wc = lax.broadcasted_iota(jnp.int32, (2 * C_DK, 2 * C_DK), 1) // C_DK
    w2_ref[...] = (wr == wc).astype(BF16)
    n = HG_PAR * 2 * c
    pr = lax.broadcasted_iota(jnp.int32, (n, n), 0)
    pc = lax.broadcasted_iota(jnp.int32, (n, n), 1)
    entry = lambda x: jnp.where(x % c < c // 2, 0, (x % c) // HG_SUB)
    pmask_ref[...] = ((pr // c == pc // c) & (entry(pr) == entry(pc))).astype(F32)

    lax.fori_loop(0, nc // HG_PREP, functools.partial(
        _hgrn_tables, x_refs=(ff_ref, fb_ref), lbs=(lower_bound(0), lower_bound(1)), q_ref=q_ref,
        v_ref=v_ref, tri_ref=tri_ref, bcum_ref=bcum_ref, ckey_ref=ckey_ref, qs_ref=qs_ref,
        vb_ref=vb_ref, upd_ref=upd_ref, dec_ref=dec_ref), 0)
    zero = jnp.zeros((C_DK, C_DK), F32)
    lax.fori_loop(0, nc, functools.partial(
        _hgrn_scan, upd_ref=upd_ref, dec_ref=dec_ref, sbf_ref=sbf_ref, nc=nc), (zero, zero))
    lax.fori_loop(0, nc // HG_PAR, functools.partial(
        _hgrn_output, v_ref=v_ref, o_ref=o_ref, bcum_ref=bcum_ref, ckey_ref=ckey_ref,
        qs_ref=qs_ref, vb_ref=vb_ref, sbf_ref=sbf_ref, w2_ref=w2_ref, pmask_ref=pmask_ref,
        lhs_ref=lhs_ref, rs_ref=rs_ref), 0)


def hgrn2(proj, lower_bounds, layer_idx):
    b, s, _ = proj.shape
    depth = lower_bounds.shape[1]
    dk = C_DK
    nc = s // HG_CHUNK
    direct_rows = HG_PAR * 2 * (HG_CHUNK // HG_SUB) * (len(_direct_pairs(False)) // 2 * _HALF)
    pivot_rows = HG_PAR * 2 * HG_CHUNK
    blk = lambda off: pl.BlockSpec((1, s, dk), lambda bi, h: (bi, 0, off * C_HEADS + h))
    return pl.pallas_call(
        functools.partial(_hgrn_body, depth=depth, layer_idx=layer_idx),
        grid=(b, C_HEADS),
        in_specs=[blk(0), blk(1), blk(2), blk(3),
                  pl.BlockSpec((2 * depth, dk), lambda bi, h: (0, h))],
        out_specs=pl.BlockSpec((1, s, dk), lambda bi, h: (bi, 0, h)),
        out_shape=jax.ShapeDtypeStruct((b, s, C_HEADS * dk), F32),
        scratch_shapes=[pltpu.VMEM((2, s, dk), F32),
                        pltpu.VMEM((2, s, dk), F32),
                        pltpu.VMEM((s, dk), F32),
                        pltpu.VMEM((s, dk), BF16),
                        pltpu.VMEM((nc, dk, 2 * dk), F32),
                        pltpu.VMEM((nc, 1, 2 * dk), F32),
                        pltpu.VMEM((2, nc, dk, dk), BF16),
                        pltpu.VMEM((2, HG_CHUNK, 3 * HG_CHUNK), BF16),
                        pltpu.VMEM((2 * dk, 2 * dk), BF16),
                        pltpu.VMEM((pivot_rows, pivot_rows), F32),
                        pltpu.VMEM((direct_rows, 2 * dk), BF16),
                        pltpu.VMEM((direct_rows, 2 * dk), F32)],
        compiler_params=_params(("parallel", "parallel"), 40),
        name="hgrn2",
    )(proj, proj, proj, proj, lower_bounds.reshape(2 * depth, C_HEADS * dk))


def kernel(x, mix_norm, even_w_in, even_w_out, diff_lq1, diff_lk1, diff_lq2, diff_lk2,
           diff_subln, gmlp_ln_g, gmlp_ln_b, gmlp_w_s, gmlp_b_s, hgrn_w_in, hgrn_w_out,
           hgrn_lower_bounds, hgrn_g_norm, ffn_norm, ffn_w_gate, ffn_w_up, ffn_w_down,
           final_norm):
    b, s, d = x.shape
    depth = mix_norm.shape[0]
    m = b * s
    h = x.reshape(m, d)
    (even_w_in, even_w_out, hgrn_w_in, hgrn_w_out, w_gate, w_up, w_down) = (
        w.astype(BF16) for w in (even_w_in, even_w_out, hgrn_w_in, hgrn_w_out,
                                 ffn_w_gate, ffn_w_up, ffn_w_down))
    for l in range(depth):
        if l % 2 == 0:
            e = l // 2
            proj = norm_proj(h, mix_norm[l], even_w_in, e).reshape(b, s, -1)
            lam_init = 0.8 - 0.6 * math.exp(-0.3 * l)
            lam_vec = jnp.stack([diff_lq1[e], diff_lk1[e], diff_lq2[e], diff_lk2[e]]).astype(F32)
            oa = diff_attention(proj, lam_vec, diff_subln[e], lam_init)
            u_block = (3 * A_HEADS * A_V_DIM) // (B_GROUPS * B_GROUP_DIM)
            ob = gmlp(proj, gmlp_ln_g[e], gmlp_ln_b[e], gmlp_w_s[e], gmlp_b_s[e], u_block)
            h = out_proj2(oa.reshape(m, -1), ob.reshape(m, -1), even_w_out, e, h)
        else:
            o = l // 2
            proj = norm_proj(h, mix_norm[l], hgrn_w_in, o)
            scan = hgrn2(proj.reshape(b, s, -1), hgrn_lower_bounds, l)
            h = out_proj_gated(scan.reshape(m, -1), proj, 4, hgrn_g_norm[o], hgrn_w_out, o, h)
        last = l == depth - 1
        h = ffn(h, ffn_norm[l], w_gate, w_up, w_down, l,
                final_gain=final_norm if last else None)
    return h.reshape(b, s, d)
```

```python
import functools
import math

import jax
import jax.numpy as jnp
from jax import lax
from jax.experimental import pallas as pl
from jax.experimental.pallas import tpu as pltpu

F32 = jnp.float32
BF16 = jnp.bfloat16
EPS = 1e-6
MIB = 1024 * 1024

A_HEADS = 8
A_QK_DIM = 64
A_V_DIM = 128
ROPE_THETA = 500000.0
ROPE_DIM = A_QK_DIM // 4
B_GROUPS = 8
B_GROUP_DIM = 128
B_CHUNK = 128
C_HEADS = 16
C_DK = 128
HG_CHUNK = 64
HG_SUB = 16
HG_PAR = 4
HG_PIV = 2
HG_PREP = 8


def _params(semantics, vmem_mib):
    return pltpu.CompilerParams(dimension_semantics=semantics,
                                vmem_limit_bytes=vmem_mib * MIB)


def _rms(x, gain):
    ms = jnp.mean(x * x, axis=-1, keepdims=True)
    return x * lax.rsqrt(ms + EPS) * gain


def _sigmoid(x):
    return 1.0 / (1.0 + jnp.exp(-x))


def _gelu_tanh(x):
    c = math.sqrt(2.0 / math.pi)
    return x * (0.5 * (1.0 + jnp.tanh(c * (x + 0.044715 * (x * x * x)))))


def _dot(a, b):
    return jnp.dot(a, b, preferred_element_type=F32)


def _dot_nt(a, b):
    return lax.dot_general(a, b, (((1,), (1,)), ((), ())), preferred_element_type=F32)


def _dot_tn(a, b):
    return lax.dot_general(a, b, (((0,), (0,)), ((), ())), preferred_element_type=F32)


def _norm_proj_body(x_ref, g_ref, w_ref, o_ref, xn_ref):
    @pl.when(pl.program_id(1) == 0)
    def _():
        xn_ref[...] = _rms(x_ref[...], g_ref[...]).astype(BF16)

    o_ref[...] = _dot(xn_ref[...], w_ref[...])


def norm_proj(x, gain, w, layer, tm=1024, tn=1024):
    m, k = x.shape
    n = w.shape[2]
    tm, tn = min(tm, m), min(tn, n)
    return pl.pallas_call(
        _norm_proj_body,
        grid=(m // tm, n // tn),
        in_specs=[pl.BlockSpec((tm, k), lambda i, j: (i, 0)),
                  pl.BlockSpec((1, k), lambda i, j: (0, 0)),
                  pl.BlockSpec((None, k, tn), lambda i, j: (layer, 0, j))],
        out_specs=pl.BlockSpec((tm, tn), lambda i, j: (i, j)),
        out_shape=jax.ShapeDtypeStruct((m, n), F32),
        scratch_shapes=[pltpu.VMEM((tm, k), BF16)],
        compiler_params=_params(("parallel", "arbitrary"), 48),
        name="norm_proj",
    )(x, gain.reshape(1, k), w)


def _rope(x, cos, sin_lo, sin_hi):
    half = ROPE_DIM // 2
    return (x * cos + pltpu.roll(x, half, 1) * sin_lo
            + pltpu.roll(x, x.shape[1] - half, 1) * sin_hi)


def _attn_body(lam_ref, subln_ref, cos_ref, slo_ref, shi_ref, q_ref, k_ref, v_ref, o_ref,
               kb_ref, vb_ref, *, lam_init, tq):
    s_len = k_ref.shape[1]
    kb_ref[...] = _rope(k_ref[0], cos_ref[...], slo_ref[...], shi_ref[...]).astype(BF16)
    vb_ref[:, 0:A_V_DIM] = v_ref[0].astype(BF16)
    vb_ref[:, A_V_DIM:] = jnp.ones((s_len, A_V_DIM), BF16)
    lv = lam_ref[...]
    lam = (jnp.exp(jnp.sum(lv[0:1] * lv[1:2], axis=-1, keepdims=True))
           - jnp.exp(jnp.sum(lv[2:3] * lv[3:4], axis=-1, keepdims=True)) + lam_init)
    scale = A_QK_DIM ** -0.5 * math.log2(math.e)
    lane = lax.broadcasted_iota(jnp.int32, (tq, A_V_DIM), 1)
    for j in range(s_len // tq):
        rows = slice(j * tq, (j + 1) * tq)
        q = _rope(q_ref[0, rows, :], cos_ref[rows, :], slo_ref[rows, :], shi_ref[rows, :]) * scale
        q2 = jnp.concatenate([jnp.where(lane < A_QK_DIM, q, 0.0),
                              jnp.where(lane >= A_QK_DIM, q, 0.0)], axis=0).astype(BF16)
        s = _dot_nt(q2, kb_ref[...])
        p = jnp.exp2(s - jnp.max(s, axis=-1, keepdims=True))
        ov = _dot(p.astype(BF16), vb_ref[...])
        on = ov[:, 0:A_V_DIM] / ov[:, A_V_DIM:]
        o = on[0:tq] - lam * on[tq:]
        o_ref[0, rows, :] = (_rms(o, subln_ref[...]) * (1.0 - lam_init)).astype(BF16)


def _rope_tables(s):
    half = ROPE_DIM // 2
    pos = jnp.arange(s, dtype=F32)
    inv_freq = ROPE_THETA ** (-jnp.arange(half, dtype=F32) / half)
    ang = pos[:, None] * inv_freq[None, :]
    cos, sin = jnp.cos(ang), jnp.sin(ang)
    pad = A_QK_DIM - ROPE_DIM
    one = jnp.ones((s, pad), F32)
    zero = jnp.zeros((s, pad), F32)
    zh = jnp.zeros((s, half), F32)
    cos_c = jnp.concatenate([cos, cos, one], axis=1)
    lo_c = jnp.concatenate([zh, sin, zero], axis=1)
    hi_c = jnp.concatenate([-sin, zh, zero], axis=1)
    rep = lambda t: jnp.concatenate([t, t], axis=1)
    return rep(cos_c), rep(lo_c), rep(hi_c)


def diff_attention(proj, lam_vec, subln, lam_init, tq=256):
    b, s, _ = proj.shape
    tq = min(tq, s)
    cos, lo, hi = _rope_tables(s)
    hd = A_V_DIM
    table = pl.BlockSpec((s, hd), lambda bi, h: (0, 0))
    head = lambda off: pl.BlockSpec((1, s, hd), lambda bi, h: (bi, 0, off * A_HEADS + h))
    return pl.pallas_call(
        functools.partial(_attn_body, lam_init=lam_init, tq=tq),
        grid=(b, A_HEADS),
        in_specs=[pl.BlockSpec((4, A_QK_DIM), lambda bi, h: (0, 0)),
                  pl.BlockSpec((1, hd), lambda bi, h: (0, 0)),
                  table, table, table, head(0), head(1), head(2)],
        out_specs=pl.BlockSpec((1, s, hd), lambda bi, h: (bi, 0, h)),
        out_shape=jax.ShapeDtypeStruct((b, s, A_HEADS * hd), BF16),
        scratch_shapes=[pltpu.VMEM((s, hd), BF16), pltpu.VMEM((s, 2 * hd), BF16)],
        compiler_params=_params(("parallel", "parallel"), 56),
        name="diff_attention",
    )(lam_vec, subln.reshape(1, hd), cos, lo, hi, proj, proj, proj)


def _gmlp_body(u_ref, v_ref, lng_ref, lnb_ref, ws_ref, bs_ref, o_ref, *, n_chunks):
    gd = B_GROUP_DIM
    for c in range(n_chunks):
        rows = slice(c * B_CHUNK, (c + 1) * B_CHUNK)
        vb = _gelu_tanh(v_ref[0, rows, :])
        xc = vb - jnp.mean(vb, axis=-1, keepdims=True)
        var = jnp.mean(xc * xc, axis=-1, keepdims=True)
        y = (xc * lax.rsqrt(var + EPS) * lng_ref[...] + lnb_ref[...]).astype(BF16)
        u = _gelu_tanh(u_ref[0, rows, :])
        for g in range(B_GROUPS):
            cols = slice(g * gd, (g + 1) * gd)
            sv = _dot(ws_ref[g], y[:, cols]) + bs_ref[:, g:g + 1]
            o_ref[0, rows, cols] = (u[:, cols] * sv).astype(BF16)


def gmlp(proj, ln_g, ln_b, w_s, b_s, u_block, n_chunks=4):
    b, s, _ = proj.shape
    width = B_GROUPS * B_GROUP_DIM
    n_chunks = min(n_chunks, s // B_CHUNK)
    t = n_chunks * B_CHUNK
    return pl.pallas_call(
        functools.partial(_gmlp_body, n_chunks=n_chunks),
        grid=(b, s // t),
        in_specs=[pl.BlockSpec((1, t, width), lambda bi, n: (bi, n, u_block)),
                  pl.BlockSpec((1, t, width), lambda bi, n: (bi, n, u_block + 1)),
                  pl.BlockSpec((1, width), lambda bi, n: (0, 0)),
                  pl.BlockSpec((1, width), lambda bi, n: (0, 0)),
                  pl.BlockSpec((B_GROUPS, B_CHUNK, B_CHUNK), lambda bi, n: (0, 0, 0)),
                  pl.BlockSpec((B_CHUNK, B_GROUPS), lambda bi, n: (0, 0))],
        out_specs=pl.BlockSpec((1, t, width), lambda bi, n: (bi, n, 0)),
        out_shape=jax.ShapeDtypeStruct((b, s, width), BF16),
        compiler_params=_params(("parallel", "parallel"), 32),
        name="gmlp",
    )(proj, proj, ln_g.reshape(1, width), ln_b.reshape(1, width), w_s.astype(BF16), b_s.T)


def _out_proj2_body(a_ref, b_ref, w_ref, r_ref, o_ref):
    ka = a_ref.shape[1]
    acc = _dot(a_ref[...], w_ref[0:ka, :]) + _dot(b_ref[...], w_ref[ka:, :])
    o_ref[...] = r_ref[...] + acc


def out_proj2(a, b, w, layer, res, tm=1024, tn=512):
    m, ka = a.shape
    kb = b.shape[1]
    n = w.shape[2]
    tm, tn = min(tm, m), min(tn, n)
    return pl.pallas_call(
        _out_proj2_body,
        grid=(m // tm, n // tn),
        in_specs=[pl.BlockSpec((tm, ka), lambda i, j: (i, 0)),
                  pl.BlockSpec((tm, kb), lambda i, j: (i, 0)),
                  pl.BlockSpec((None, ka + kb, tn), lambda i, j: (layer, 0, j)),
                  pl.BlockSpec((tm, tn), lambda i, j: (i, j))],
        out_specs=pl.BlockSpec((tm, tn), lambda i, j: (i, j)),
        out_shape=jax.ShapeDtypeStruct((m, n), F32),
        compiler_params=_params(("parallel", "parallel"), 48),
        name="out_proj_even",
    )(a, b, w, res)


def _out_proj_gated_body(o_in_ref, gate_ref, gn_ref, w_ref, r_ref, o_ref, y_ref):
    @pl.when(pl.program_id(1) == 0)
    def _():
        gate = gate_ref[...]
        y = _rms(o_in_ref[...], gn_ref[...]) * (gate * _sigmoid(gate))
        y_ref[...] = y.astype(BF16)

    o_ref[...] = r_ref[...] + _dot(y_ref[...], w_ref[...])


def out_proj_gated(o_in, proj, gate_block, gain, w, layer, res, tm=512, tn=2048):
    m, k = o_in.shape
    n = w.shape[2]
    tm, tn = min(tm, m), min(tn, n)
    return pl.pallas_call(
        _out_proj_gated_body,
        grid=(m // tm, n // tn),
        in_specs=[pl.BlockSpec((tm, k), lambda i, j: (i, 0)),
                  pl.BlockSpec((tm, k), lambda i, j: (i, gate_block)),
                  pl.BlockSpec((1, k), lambda i, j: (0, 0)),
                  pl.BlockSpec((None, k, tn), lambda i, j: (layer, 0, j)),
                  pl.BlockSpec((tm, tn), lambda i, j: (i, j))],
        out_specs=pl.BlockSpec((tm, tn), lambda i, j: (i, j)),
        out_shape=jax.ShapeDtypeStruct((m, n), F32),
        scratch_shapes=[pltpu.VMEM((tm, k), BF16)],
        compiler_params=_params(("parallel", "arbitrary"), 56),
        name="out_proj_odd",
    )(o_in, proj, gain.reshape(1, k), w, res)


def _ffn_body(*refs, final_norm):
    if final_norm:
        x_ref, g_ref, wg_ref, wu_ref, wd_ref, fn_ref, o_ref, xn_ref = refs
    else:
        x_ref, g_ref, wg_ref, wu_ref, wd_ref, o_ref, xn_ref = refs
    f = pl.program_id(1)

    @pl.when(f == 0)
    def _():
        x = x_ref[...]
        xn_ref[...] = _rms(x, g_ref[...]).astype(BF16)
        o_ref[...] = x

    xn = xn_ref[...]
    a = _dot(xn, wg_ref[...])
    u = _dot(xn, wu_ref[...])
    hid = (a * _sigmoid(a) * u).astype(BF16)
    o_ref[...] += _dot(hid, wd_ref[...])

    if final_norm:
        @pl.when(f == pl.num_programs(1) - 1)
        def _():
            o_ref[...] = _rms(o_ref[...], fn_ref[...])


def ffn(x, gain, w_gate, w_up, w_down, layer, final_gain=None, tm=1024, tf=512):
    m, k = x.shape
    hdim = w_gate.shape[2]
    tm, tf = min(tm, m), min(tf, hdim)
    vec = pl.BlockSpec((1, k), lambda i, f: (0, 0))
    in_specs = [pl.BlockSpec((tm, k), lambda i, f: (i, 0)), vec,
                pl.BlockSpec((None, k, tf), lambda i, f: (layer, 0, f)),
                pl.BlockSpec((None, k, tf), lambda i, f: (layer, 0, f)),
                pl.BlockSpec((None, tf, k), lambda i, f: (layer, f, 0))]
    args = [x, gain.reshape(1, k), w_gate, w_up, w_down]
    if final_gain is not None:
        in_specs.append(vec)
        args.append(final_gain.reshape(1, k))
    return pl.pallas_call(
        functools.partial(_ffn_body, final_norm=final_gain is not None),
        grid=(m // tm, hdim // tf),
        in_specs=in_specs,
        out_specs=pl.BlockSpec((tm, k), lambda i, f: (i, 0)),
        out_shape=jax.ShapeDtypeStruct((m, k), F32),
        scratch_shapes=[pltpu.VMEM((tm, k), BF16)],
        compiler_params=_params(("parallel", "arbitrary"), 60),
        name="ffn",
    )(*args)


_HALF = 8


def _hgrn_tables(ti, carry, *, x_refs, lbs, q_ref, v_ref, tri_ref, bcum_ref, ckey_ref,
                 qs_ref, vb_ref, upd_ref, dec_ref):
    c = HG_CHUNK
    rows = [pl.multiple_of((ti * HG_PREP + u) * c, c) for u in range(HG_PREP)]
    kdec = [[None, None] for _ in range(HG_PREP)]
    decay = [[None, None] for _ in range(HG_PREP)]
    for d in range(2):
        lb = lbs[d]
        splits, log_k = [], []
        for r in rows:
            f = lb + (1.0 - lb) * _sigmoid(x_refs[d][0, pl.ds(r, c), :])
            logf = jnp.log2(f)
            log_k.append(jnp.log2(1.0 - f))
            hi = logf.astype(BF16)
            rem = logf - hi.astype(F32)
            mid = rem.astype(BF16)
            low = (rem - mid.astype(F32)).astype(BF16)
            splits.append(jnp.concatenate([hi, mid, low], axis=0))
        bc = _dot(tri_ref[d], jnp.concatenate(splits, axis=1))
        for u, r in enumerate(rows):
            bcum = bc[:, u * C_DK:(u + 1) * C_DK]
            ckey = bcum - log_k[u]
            bcum_ref[d, pl.ds(r, c), :] = bcum
            ckey_ref[d, pl.ds(r, c), :] = ckey
            edge = bcum[c - 1:c] if d == 0 else bcum[0:1]
            kdec[u][d] = jnp.exp2(edge - ckey).astype(BF16)
            decay[u][d] = jnp.exp2(edge)
    for u, r in enumerate(rows):
        ci = ti * HG_PREP + u
        vb = v_ref[0, pl.ds(r, c), :].astype(BF16)
        vb_ref[pl.ds(r, c), :] = vb
        upd_ref[ci] = _dot_tn(vb, jnp.concatenate(kdec[u], axis=1))
        dec_ref[ci] = jnp.concatenate(decay[u], axis=1)
        q = q_ref[0, pl.ds(r, c), :]
        qs_ref[pl.ds(r, c), :] = q * _sigmoid(q)
    return carry


def _hgrn_scan(ci, carry, *, upd_ref, dec_ref, sbf_ref, nc):
    st_f, st_b = carry
    cb = nc - 1 - ci
    sbf_ref[0, ci] = st_f.astype(BF16)
    sbf_ref[1, cb] = st_b.astype(BF16)
    st_f = st_f * dec_ref[ci, :, 0:C_DK] + upd_ref[ci, :, 0:C_DK]
    st_b = st_b * dec_ref[cb, :, C_DK:2 * C_DK] + upd_ref[cb, :, C_DK:2 * C_DK]
    return st_f, st_b


def _direct_pairs(reverse):
    pairs = []
    for s in range(HG_SUB):
        for h in range(2):
            sh = s // _HALF
            if not ((h > sh) if reverse else (h < sh)):
                pairs.append((s, h))
    return pairs


def _hgrn_direct_pieces(d, r0, *, bcum_ref, ckey_ref, qs_ref):
    reverse = d == 1
    t_idx = lax.broadcasted_iota(jnp.int32, (_HALF, C_DK), 0)
    b = [bcum_ref[d, pl.ds(r0 + _HALF * h, _HALF), :] for h in range(2)]
    q = [qs_ref[pl.ds(r0 + _HALF * h, _HALF), :] for h in range(2)]
    pieces = []
    for s, h in _direct_pairs(reverse):
        p = q[h] * jnp.exp2(b[h] - ckey_ref[d, pl.ds(r0 + s, 1), :])
        if h == s // _HALF:
            t0 = s - _HALF * h
            p = jnp.where((t_idx <= t0) if reverse else (t_idx >= t0), p, 0.0)
        pieces.append(p)
    rows = len(pieces) // 2 * _HALF
    lhs = jnp.concatenate(pieces, axis=0).astype(BF16)
    return jnp.concatenate([lhs[:rows], lhs[rows:]], axis=1)


def _hgrn_direct_combine(d, r0, rs_ref, row0, *, v_ref):
    pairs = _direct_pairs(d == 1)
    n_half = len(pairs) // 2
    out = [jnp.zeros((_HALF, C_DK), F32), jnp.zeros((_HALF, C_DK), F32)]
    for p_idx, (s, h) in enumerate(pairs):
        col, row = divmod(p_idx, n_half)
        rp = rs_ref[row0 + row * _HALF:row0 + (row + 1) * _HALF, col * C_DK:(col + 1) * C_DK]
        out[h] = out[h] + rp * v_ref[0, pl.ds(r0 + s, 1), :]
    return jnp.concatenate(out, axis=0)


def _pivot_plan(reverse):
    s, h = HG_SUB, HG_CHUNK // 2
    assert HG_CHUNK == 4 * HG_SUB
    if reverse:
        return [(0, h, h, h, h), (0, s, s, s, s), (h, s, h + s, s, h + s)]
    return [(h, h, 0, h, h - 1), (s, s, 0, s, s - 1), (h + s, s, h, s, h + s - 1)]


def _hgrn_output(ti, carry, *, v_ref, o_ref, bcum_ref, ckey_ref, qs_ref, vb_ref, sbf_ref,
                 w2_ref, pmask_ref, lhs_ref, rs_ref):
    c, sub = HG_CHUNK, HG_SUB
    n_sub = c // sub
    blk_rows = len(_direct_pairs(False)) // 2 * _HALF
    rows = [pl.multiple_of((ti * HG_PAR + u) * c, c) for u in range(HG_PAR)]
    for u, r in enumerate(rows):
        for d in range(2):
            for i in range(n_sub):
                row0 = ((u * 2 + d) * n_sub + i) * blk_rows
                lhs_ref[row0:row0 + blk_rows, :] = _hgrn_direct_pieces(
                    d, r + i * sub, bcum_ref=bcum_ref, ckey_ref=ckey_ref, qs_ref=qs_ref)
    rs_ref[...] = _dot(lhs_ref[...], w2_ref[...])
    pv_groups = []
    for g in range(HG_PAR // HG_PIV):
        q_parts, k_parts, v_parts = [], [], []
        for r in rows[g * HG_PIV:(g + 1) * HG_PIV]:
            for d in range(2):
                for q0, qn, k0, kn, prow in _pivot_plan(d == 1):
                    piv = bcum_ref[d, pl.ds(r + prow, 1), :]
                    q_parts.append(qs_ref[pl.ds(r + q0, qn), :]
                                   * jnp.exp2(bcum_ref[d, pl.ds(r + q0, qn), :] - piv))
                    k_parts.append(jnp.exp2(piv - ckey_ref[d, pl.ds(r + k0, kn), :]))
                    v_parts.append(vb_ref[pl.ds(r + k0, kn), :])
        sc = _dot_nt(jnp.concatenate(q_parts, axis=0).astype(BF16),
                     jnp.concatenate(k_parts, axis=0).astype(BF16)) * pmask_ref[...]
        pv_groups.append(_dot(sc.astype(BF16), jnp.concatenate(v_parts, axis=0)))
    for u, r in enumerate(rows):
        ci = ti * HG_PAR + u
        total = None
        for d in range(2):
            blocks = [_hgrn_direct_combine(d, r + i * sub, rs_ref,
                                           ((u * 2 + d) * n_sub + i) * blk_rows, v_ref=v_ref)
                      for i in range(n_sub)]
            pv = pv_groups[u // HG_PIV]
            off = ((u % HG_PIV) * 2 + d) * c
            for q0, qn, _, _, _ in _pivot_plan(d == 1):
                for j in range(qn // sub):
                    blocks[q0 // sub + j] = (blocks[q0 // sub + j]
                                             + pv[off + j * sub:off + (j + 1) * sub])
                off += qn
            qe = (qs_ref[pl.ds(r, c), :] * jnp.exp2(bcum_ref[d, pl.ds(r, c), :])).astype(BF16)
            o_d = _dot_nt(qe, sbf_ref[d, ci]) + jnp.concatenate(blocks, axis=0)
            total = o_d if total is None else total + o_d
        o_ref[0, pl.ds(r, c), :] = total
    return carry


def _hgrn_body(q_ref, ff_ref, fb_ref, v_ref, lbr_ref, o_ref,
               bcum_ref, ckey_ref, qs_ref, vb_ref, upd_ref, dec_ref, sbf_ref, tri_ref, w2_ref,
               pmask_ref, lhs_ref, rs_ref, *, depth, layer_idx):
    s = q_ref.shape[1]
    c = HG_CHUNK
    nc = s // c
    assert nc % HG_PAR == 0 and nc % HG_PREP == 0 and HG_PAR % HG_PIV == 0

    def lower_bound(d):
        rows = lbr_ref[d * depth:(d + 1) * depth, :]
        e = jnp.exp(rows - jnp.max(rows, axis=0, keepdims=True))
        sm = e / jnp.sum(e, axis=0, keepdims=True)
        acc = sm[0:1] * 0.0
        for l in range(1, layer_idx + 1):
            acc = acc + sm[l:l + 1]
        return acc

    row = lax.broadcasted_iota(jnp.int32, (c, 3 * c), 0)
    col = lax.broadcasted_iota(jnp.int32, (c, 3 * c), 1) % c
    tri_ref[0] = (col <= row).astype(BF16)
    tri_ref[1] = (col >= row).astype(BF16)
    wr = lax.broadcasted_iota(jnp.int32, (2 * C_DK, 2 * C_DK), 0) // C_DK
    wc = lax.broadcasted_iota(jnp.int32, (2 * C_DK, 2 * C_DK), 1) // C_DK
    w2_ref[...] = (wr == wc).astype(BF16)
    n = HG_PIV * 2 * c
    pr = lax.broadcasted_iota(jnp.int32, (n, n), 0)
    pc = lax.broadcasted_iota(jnp.int32, (n, n), 1)
    entry = lambda x: jnp.where(x % c < c // 2, 0, (x % c) // HG_SUB)
    pmask_ref[...] = ((pr // c == pc // c) & (entry(pr) == entry(pc))).astype(F32)

    lax.fori_loop(0, nc // HG_PREP, functools.partial(
        _hgrn_tables, x_refs=(ff_ref, fb_ref), lbs=(lower_bound(0), lower_bound(1)), q_ref=q_ref,
        v_ref=v_ref, tri_ref=tri_ref, bcum_ref=bcum_ref, ckey_ref=ckey_ref, qs_ref=qs_ref,
        vb_ref=vb_ref, upd_ref=upd_ref, dec_ref=dec_ref), 0)
    zero = jnp.zeros((C_DK, C_DK), F32)
    lax.fori_loop(0, nc, functools.partial(
        _hgrn_scan, upd_ref=upd_ref, dec_ref=dec_ref, sbf_ref=sbf_ref, nc=nc), (zero, zero))
    lax.fori_loop(0, nc // HG_PAR, functools.partial(
        _hgrn_output, v_ref=v_ref, o_ref=o_ref, bcum_ref=bcum_ref, ckey_ref=ckey_ref,
        qs_ref=qs_ref, vb_ref=vb_ref, sbf_ref=sbf_ref, w2_ref=w2_ref, pmask_ref=pmask_ref,
        lhs_ref=lhs_ref, rs_ref=rs_ref), 0)


def hgrn2(proj, lower_bounds, layer_idx):
    b, s, _ = proj.shape
    depth = lower_bounds.shape[1]
    dk = C_DK
    nc = s // HG_CHUNK
    direct_rows = HG_PAR * 2 * (HG_CHUNK // HG_SUB) * (len(_direct_pairs(False)) // 2 * _HALF)
    pivot_rows = HG_PIV * 2 * HG_CHUNK
    blk = lambda off: pl.BlockSpec((1, s, dk), lambda bi, h: (bi, 0, off * C_HEADS + h))
    return pl.pallas_call(
        functools.partial(_hgrn_body, depth=depth, layer_idx=layer_idx),
        grid=(b, C_HEADS),
        in_specs=[blk(0), blk(1), blk(2), blk(3),
                  pl.BlockSpec((2 * depth, dk), lambda bi, h: (0, h))],
        out_specs=pl.BlockSpec((1, s, dk), lambda bi, h: (bi, 0, h)),
        out_shape=jax.ShapeDtypeStruct((b, s, C_HEADS * dk), F32),
        scratch_shapes=[pltpu.VMEM((2, s, dk), F32),
                        pltpu.VMEM((2, s, dk), F32),
                        pltpu.VMEM((s, dk), F32),
                        pltpu.VMEM((s, dk), BF16),
                        pltpu.VMEM((nc, dk, 2 * dk), F32),
                        pltpu.VMEM((nc, 1, 2 * dk), F32),
                        pltpu.VMEM((2, nc, dk, dk), BF16),
                        pltpu.VMEM((2, HG_CHUNK, 3 * HG_CHUNK), BF16),
                        pltpu.VMEM((2 * dk, 2 * dk), BF16),
                        pltpu.VMEM((pivot_rows, pivot_rows), F32),
                        pltpu.VMEM((direct_rows, 2 * dk), BF16),
                        pltpu.VMEM((direct_rows, 2 * dk), F32)],
        compiler_params=_params(("parallel", "parallel"), 40),
        name="hgrn2",
    )(proj, proj, proj, proj, lower_bounds.reshape(2 * depth, C_HEADS * dk))


def kernel(x, mix_norm, even_w_in, even_w_out, diff_lq1, diff_lk1, diff_lq2, diff_lk2,
           diff_subln, gmlp_ln_g, gmlp_ln_b, gmlp_w_s, gmlp_b_s, hgrn_w_in, hgrn_w_out,
           hgrn_lower_bounds, hgrn_g_norm, ffn_norm, ffn_w_gate, ffn_w_up, ffn_w_down,
           final_norm):
    b, s, d = x.shape
    depth = mix_norm.shape[0]
    m = b * s
    h = x.reshape(m, d)
    (even_w_in, even_w_out, hgrn_w_in, hgrn_w_out, w_gate, w_up, w_down) = (
        w.astype(BF16) for w in (even_w_in, even_w_out, hgrn_w_in, hgrn_w_out,
                                 ffn_w_gate, ffn_w_up, ffn_w_down))
    for l in range(depth):
        if l % 2 == 0:
            e = l // 2
            proj = norm_proj(h, mix_norm[l], even_w_in, e).reshape(b, s, -1)
            lam_init = 0.8 - 0.6 * math.exp(-0.3 * l)
            lam_vec = jnp.stack([diff_lq1[e], diff_lk1[e], diff_lq2[e], diff_lk2[e]]).astype(F32)
            oa = diff_attention(proj, lam_vec, diff_subln[e], lam_init)
            u_block = (3 * A_HEADS * A_V_DIM) // (B_GROUPS * B_GROUP_DIM)
            ob = gmlp(proj, gmlp_ln_g[e], gmlp_ln_b[e], gmlp_w_s[e], gmlp_b_s[e], u_block)
            h = out_proj2(oa.reshape(m, -1), ob.reshape(m, -1), even_w_out, e, h)
        else:
            o = l // 2
            proj = norm_proj(h, mix_norm[l], hgrn_w_in, o)
            scan = hgrn2(proj.reshape(b, s, -1), hgrn_lower_bounds, l)
            h = out_proj_gated(scan.reshape(m, -1), proj, 4, hgrn_g_norm[o], hgrn_w_out, o, h)
        last = l == depth - 1
        h = ffn(h, ffn_norm[l], w_gate, w_up, w_down, l,
                final_gain=final_norm if last else None)
    return h.reshape(b, s, d)
```

```python
import functools
import math

import jax
import jax.numpy as jnp
from jax import lax
from jax.experimental import pallas as pl
from jax.experimental.pallas import tpu as pltpu

F32 = jnp.float32
BF16 = jnp.bfloat16
EPS = 1e-6
MIB = 1024 * 1024

A_HEADS = 8
A_QK_DIM = 64
A_V_DIM = 128
ROPE_THETA = 500000.0
ROPE_DIM = A_QK_DIM // 4
B_GROUPS = 8
B_GROUP_DIM = 128
B_CHUNK = 128
C_HEADS = 16
C_DK = 128
HG_CHUNK = 64
HG_SUB = 16
HG_PAR = 8
HG_PIV = 2
HG_PREP = 8


def _params(semantics, vmem_mib):
    return pltpu.CompilerParams(dimension_semantics=semantics,
                                vmem_limit_bytes=vmem_mib * MIB)


def _rms(x, gain):
    ms = jnp.mean(x * x, axis=-1, keepdims=True)
    return x * lax.rsqrt(ms + EPS) * gain


def _sigmoid(x):
    return 1.0 / (1.0 + jnp.exp(-x))


def _gelu_tanh(x):
    c = math.sqrt(2.0 / math.pi)
    return x * (0.5 * (1.0 + jnp.tanh(c * (x + 0.044715 * (x * x * x)))))


def _dot(a, b):
    return jnp.dot(a, b, preferred_element_type=F32)


def _dot_nt(a, b):
    return lax.dot_general(a, b, (((1,), (1,)), ((), ())), preferred_element_type=F32)


def _dot_tn(a, b):
    return lax.dot_general(a, b, (((0,), (0,)), ((), ())), preferred_element_type=F32)


def _norm_proj_body(x_ref, g_ref, w_ref, o_ref, xn_ref):
    @pl.when(pl.program_id(1) == 0)
    def _():
        xn_ref[...] = _rms(x_ref[...], g_ref[...]).astype(BF16)

    o_ref[...] = _dot(xn_ref[...], w_ref[...])


def norm_proj(x, gain, w, layer, tm=1024, tn=1024):
    m, k = x.shape
    n = w.shape[2]
    tm, tn = min(tm, m), min(tn, n)
    return pl.pallas_call(
        _norm_proj_body,
        grid=(m // tm, n // tn),
        in_specs=[pl.BlockSpec((tm, k), lambda i, j: (i, 0)),
                  pl.BlockSpec((1, k), lambda i, j: (0, 0)),
                  pl.BlockSpec((None, k, tn), lambda i, j: (layer, 0, j))],
        out_specs=pl.BlockSpec((tm, tn), lambda i, j: (i, j)),
        out_shape=jax.ShapeDtypeStruct((m, n), F32),
        scratch_shapes=[pltpu.VMEM((tm, k), BF16)],
        compiler_params=_params(("parallel", "arbitrary"), 48),
        name="norm_proj",
    )(x, gain.reshape(1, k), w)


def _rope(x, cos, sin_lo, sin_hi):
    half = ROPE_DIM // 2
    return (x * cos + pltpu.roll(x, half, 1) * sin_lo
            + pltpu.roll(x, x.shape[1] - half, 1) * sin_hi)


def _attn_body(lam_ref, subln_ref, cos_ref, slo_ref, shi_ref, q_ref, k_ref, v_ref, o_ref,
               kb_ref, vb_ref, *, lam_init, tq):
    s_len = k_ref.shape[1]
    kb_ref[...] = _rope(k_ref[0], cos_ref[...], slo_ref[...], shi_ref[...]).astype(BF16)
    vb_ref[:, 0:A_V_DIM] = v_ref[0].astype(BF16)
    vb_ref[:, A_V_DIM:] = jnp.ones((s_len, A_V_DIM), BF16)
    lv = lam_ref[...]
    lam = (jnp.exp(jnp.sum(lv[0:1] * lv[1:2], axis=-1, keepdims=True))
           - jnp.exp(jnp.sum(lv[2:3] * lv[3:4], axis=-1, keepdims=True)) + lam_init)
    scale = A_QK_DIM ** -0.5 * math.log2(math.e)
    lane = lax.broadcasted_iota(jnp.int32, (tq, A_V_DIM), 1)
    for j in range(s_len // tq):
        rows = slice(j * tq, (j + 1) * tq)
        q = _rope(q_ref[0, rows, :], cos_ref[rows, :], slo_ref[rows, :], shi_ref[rows, :]) * scale
        q2 = jnp.concatenate([jnp.where(lane < A_QK_DIM, q, 0.0),
                              jnp.where(lane >= A_QK_DIM, q, 0.0)], axis=0).astype(BF16)
        s = _dot_nt(q2, kb_ref[...])
        p = jnp.exp2(s - jnp.max(s, axis=-1, keepdims=True))
        ov = _dot(p.astype(BF16), vb_ref[...])
        on = ov[:, 0:A_V_DIM] / ov[:, A_V_DIM:]
        o = on[0:tq] - lam * on[tq:]
        o_ref[0, rows, :] = (_rms(o, subln_ref[...]) * (1.0 - lam_init)).astype(BF16)


def _rope_tables(s):
    half = ROPE_DIM // 2
    pos = jnp.arange(s, dtype=F32)
    inv_freq = ROPE_THETA ** (-jnp.arange(half, dtype=F32) / half)
    ang = pos[:, None] * inv_freq[None, :]
    cos, sin = jnp.cos(ang), jnp.sin(ang)
    pad = A_QK_DIM - ROPE_DIM
    one = jnp.ones((s, pad), F32)
    zero = jnp.zeros((s, pad), F32)
    zh = jnp.zeros((s, half), F32)
    cos_c = jnp.concatenate([cos, cos, one], axis=1)
    lo_c = jnp.concatenate([zh, sin, zero], axis=1)
    hi_c = jnp.concatenate([-sin, zh, zero], axis=1)
    rep = lambda t: jnp.concatenate([t, t], axis=1)
    return rep(cos_c), rep(lo_c), rep(hi_c)


def diff_attention(proj, lam_vec, subln, lam_init, tq=128):
    b, s, _ = proj.shape
    tq = min(tq, s)
    cos, lo, hi = _rope_tables(s)
    hd = A_V_DIM
    table = pl.BlockSpec((s, hd), lambda bi, h: (0, 0))
    head = lambda off: pl.BlockSpec((1, s, hd), lambda bi, h: (bi, 0, off * A_HEADS + h))
    return pl.pallas_call(
        functools.partial(_attn_body, lam_init=lam_init, tq=tq),
        grid=(b, A_HEADS),
        in_specs=[pl.BlockSpec((4, A_QK_DIM), lambda bi, h: (0, 0)),
                  pl.BlockSpec((1, hd), lambda bi, h: (0, 0)),
                  table, table, table, head(0), head(1), head(2)],
        out_specs=pl.BlockSpec((1, s, hd), lambda bi, h: (bi, 0, h)),
        out_shape=jax.ShapeDtypeStruct((b, s, A_HEADS * hd), BF16),
        scratch_shapes=[pltpu.VMEM((s, hd), BF16), pltpu.VMEM((s, 2 * hd), BF16)],
        compiler_params=_params(("parallel", "parallel"), 56),
        name="diff_attention",
    )(lam_vec, subln.reshape(1, hd), cos, lo, hi, proj, proj, proj)


def _gmlp_body(u_ref, v_ref, lng_ref, lnb_ref, ws_ref, bs_ref, o_ref, *, n_chunks):
    gd = B_GROUP_DIM
    for c in range(n_chunks):
        rows = slice(c * B_CHUNK, (c + 1) * B_CHUNK)
        vb = _gelu_tanh(v_ref[0, rows, :])
        xc = vb - jnp.mean(vb, axis=-1, keepdims=True)
        var = jnp.mean(xc * xc, axis=-1, keepdims=True)
        y = (xc * lax.rsqrt(var + EPS) * lng_ref[...] + lnb_ref[...]).astype(BF16)
        u = _gelu_tanh(u_ref[0, rows, :])
        for g in range(B_GROUPS):
            cols = slice(g * gd, (g + 1) * gd)
            sv = _dot(ws_ref[g], y[:, cols]) + bs_ref[:, g:g + 1]
            o_ref[0, rows, cols] = (u[:, cols] * sv).astype(BF16)


def gmlp(proj, ln_g, ln_b, w_s, b_s, u_block, n_chunks=4):
    b, s, _ = proj.shape
    width = B_GROUPS * B_GROUP_DIM
    n_chunks = min(n_chunks, s // B_CHUNK)
    t = n_chunks * B_CHUNK
    return pl.pallas_call(
        functools.partial(_gmlp_body, n_chunks=n_chunks),
        grid=(b, s // t),
        in_specs=[pl.BlockSpec((1, t, width), lambda bi, n: (bi, n, u_block)),
                  pl.BlockSpec((1, t, width), lambda bi, n: (bi, n, u_block + 1)),
                  pl.BlockSpec((1, width), lambda bi, n: (0, 0)),
                  pl.BlockSpec((1, width), lambda bi, n: (0, 0)),
                  pl.BlockSpec((B_GROUPS, B_CHUNK, B_CHUNK), lambda bi, n: (0, 0, 0)),
                  pl.BlockSpec((B_CHUNK, B_GROUPS), lambda bi, n: (0, 0))],
        out_specs=pl.BlockSpec((1, t, width), lambda bi, n: (bi, n, 0)),
        out_shape=jax.ShapeDtypeStruct((b, s, width), BF16),
        compiler_params=_params(("parallel", "parallel"), 32),
        name="gmlp",
    )(proj, proj, ln_g.reshape(1, width), ln_b.reshape(1, width), w_s.astype(BF16), b_s.T)


def _out_proj2_body(a_ref, b_ref, w_ref, r_ref, o_ref):
    ka = a_ref.shape[1]
    acc = _dot(a_ref[...], w_ref[0:ka, :]) + _dot(b_ref[...], w_ref[ka:, :])
    o_ref[...] = r_ref[...] + acc


def out_proj2(a, b, w, layer, res, tm=512, tn=2048):
    m, ka = a.shape
    kb = b.shape[1]
    n = w.shape[2]
    tm, tn = min(tm, m), min(tn, n)
    return pl.pallas_call(
        _out_proj2_body,
        grid=(m // tm, n // tn),
        in_specs=[pl.BlockSpec((tm, ka), lambda i, j: (i, 0)),
                  pl.BlockSpec((tm, kb), lambda i, j: (i, 0)),
                  pl.BlockSpec((None, ka + kb, tn), lambda i, j: (layer, 0, j)),
                  pl.BlockSpec((tm, tn), lambda i, j: (i, j))],
        out_specs=pl.BlockSpec((tm, tn), lambda i, j: (i, j)),
        out_shape=jax.ShapeDtypeStruct((m, n), F32),
        compiler_params=_params(("parallel", "parallel"), 48),
        name="out_proj_even",
    )(a, b, w, res)


def _out_proj_gated_body(o_in_ref, gate_ref, gn_ref, w_ref, r_ref, o_ref, y_ref):
    @pl.when(pl.program_id(1) == 0)
    def _():
        gate = gate_ref[...]
        y = _rms(o_in_ref[...], gn_ref[...]) * (gate * _sigmoid(gate))
        y_ref[...] = y.astype(BF16)

    o_ref[...] = r_ref[...] + _dot(y_ref[...], w_ref[...])


def out_proj_gated(o_in, proj, gate_block, gain, w, layer, res, tm=512, tn=2048):
    m, k = o_in.shape
    n = w.shape[2]
    tm, tn = min(tm, m), min(tn, n)
    return pl.pallas_call(
        _out_proj_gated_body,
        grid=(m // tm, n // tn),
        in_specs=[pl.BlockSpec((tm, k), lambda i, j: (i, 0)),
                  pl.BlockSpec((tm, k), lambda i, j: (i, gate_block)),
                  pl.BlockSpec((1, k), lambda i, j: (0, 0)),
                  pl.BlockSpec((None, k, tn), lambda i, j: (layer, 0, j)),
                  pl.BlockSpec((tm, tn), lambda i, j: (i, j))],
        out_specs=pl.BlockSpec((tm, tn), lambda i, j: (i, j)),
        out_shape=jax.ShapeDtypeStruct((m, n), F32),
        scratch_shapes=[pltpu.VMEM((tm, k), BF16)],
        compiler_params=_params(("parallel", "arbitrary"), 56),
        name="out_proj_odd",
    )(o_in, proj, gain.reshape(1, k), w, res)


def _ffn_body(*refs, final_norm):
    if final_norm:
        x_ref, g_ref, wg_ref, wu_ref, wd_ref, fn_ref, o_ref, xn_ref = refs
    else:
        x_ref, g_ref, wg_ref, wu_ref, wd_ref, o_ref, xn_ref = refs
    f = pl.program_id(1)

    @pl.when(f == 0)
    def _():
        x = x_ref[...]
        xn_ref[...] = _rms(x, g_ref[...]).astype(BF16)
        o_ref[...] = x

    xn = xn_ref[...]
    a = _dot(xn, wg_ref[...])
    u = _dot(xn, wu_ref[...])
    hid = (a * _sigmoid(a) * u).astype(BF16)
    o_ref[...] += _dot(hid, wd_ref[...])

    if final_norm:
        @pl.when(f == pl.num_programs(1) - 1)
        def _():
            o_ref[...] = _rms(o_ref[...], fn_ref[...])


def ffn(x, gain, w_gate, w_up, w_down, layer, final_gain=None, tm=1024, tf=512):
    m, k = x.shape
    hdim = w_gate.shape[2]
    tm, tf = min(tm, m), min(tf, hdim)
    vec = pl.BlockSpec((1, k), lambda i, f: (0, 0))
    in_specs = [pl.BlockSpec((tm, k), lambda i, f: (i, 0)), vec,
                pl.BlockSpec((None, k, tf), lambda i, f: (layer, 0, f)),
                pl.BlockSpec((None, k, tf), lambda i, f: (layer, 0, f)),
                pl.BlockSpec((None, tf, k), lambda i, f: (layer, f, 0))]
    args = [x, gain.reshape(1, k), w_gate, w_up, w_down]
    if final_gain is not None:
        in_specs.append(vec)
        args.append(final_gain.reshape(1, k))
    return pl.pallas_call(
        functools.partial(_ffn_body, final_norm=final_gain is not None),
        grid=(m // tm, hdim // tf),
        in_specs=in_specs,
        out_specs=pl.BlockSpec((tm, k), lambda i, f: (i, 0)),
        out_shape=jax.ShapeDtypeStruct((m, k), F32),
        scratch_shapes=[pltpu.VMEM((tm, k), BF16)],
        compiler_params=_params(("parallel", "arbitrary"), 60),
        name="ffn",
    )(*args)


_HALF = 8


def _hgrn_tables(ti, carry, *, x_refs, lbs, q_ref, v_ref, tri_ref, bcum_ref, ckey_ref,
                 qs_ref, vb_ref, upd_ref, dec_ref):
    c = HG_CHUNK
    rows = [pl.multiple_of((ti * HG_PREP + u) * c, c) for u in range(HG_PREP)]
    kdec = [[None, None] for _ in range(HG_PREP)]
    decay = [[None, None] for _ in range(HG_PREP)]
    for d in range(2):
        lb = lbs[d]
        splits, log_k = [], []
        for r in rows:
            f = lb + (1.0 - lb) * _sigmoid(x_refs[d][0, pl.ds(r, c), :])
            logf = jnp.log2(f)
            log_k.append(jnp.log2(1.0 - f))
            hi = logf.astype(BF16)
            rem = logf - hi.astype(F32)
            mid = rem.astype(BF16)
            low = (rem - mid.astype(F32)).astype(BF16)
            splits.append(jnp.concatenate([hi, mid, low], axis=0))
        bc = _dot(tri_ref[d], jnp.concatenate(splits, axis=1))
        for u, r in enumerate(rows):
            bcum = bc[:, u * C_DK:(u + 1) * C_DK]
            ckey = bcum - log_k[u]
            bcum_ref[d, pl.ds(r, c), :] = bcum
            ckey_ref[d, pl.ds(r, c), :] = ckey
            edge = bcum[c - 1:c] if d == 0 else bcum[0:1]
            kdec[u][d] = jnp.exp2(edge - ckey).astype(BF16)
            decay[u][d] = jnp.exp2(edge)
    for u, r in enumerate(rows):
        ci = ti * HG_PREP + u
        vb = v_ref[0, pl.ds(r, c), :].astype(BF16)
        vb_ref[pl.ds(r, c), :] = vb
        upd_ref[ci] = _dot_tn(vb, jnp.concatenate(kdec[u], axis=1))
        dec_ref[ci] = jnp.concatenate(decay[u], axis=1)
        q = q_ref[0, pl.ds(r, c), :]
        qs_ref[pl.ds(r, c), :] = q * _sigmoid(q)
    return carry


def _hgrn_scan(ci, carry, *, upd_ref, dec_ref, sbf_ref, nc):
    st_f, st_b = carry
    cb = nc - 1 - ci
    sbf_ref[0, ci] = st_f.astype(BF16)
    sbf_ref[1, cb] = st_b.astype(BF16)
    st_f = st_f * dec_ref[ci, :, 0:C_DK] + upd_ref[ci, :, 0:C_DK]
    st_b = st_b * dec_ref[cb, :, C_DK:2 * C_DK] + upd_ref[cb, :, C_DK:2 * C_DK]
    return st_f, st_b


def _direct_pairs(reverse):
    pairs = []
    for s in range(HG_SUB):
        for h in range(2):
            sh = s // _HALF
            if not ((h > sh) if reverse else (h < sh)):
                pairs.append((s, h))
    return pairs


def _hgrn_direct_pieces(d, r0, *, bcum_ref, ckey_ref, qs_ref):
    reverse = d == 1
    t_idx = lax.broadcasted_iota(jnp.int32, (_HALF, C_DK), 0)
    b = [bcum_ref[d, pl.ds(r0 + _HALF * h, _HALF), :] for h in range(2)]
    q = [qs_ref[pl.ds(r0 + _HALF * h, _HALF), :] for h in range(2)]
    pieces = []
    for s, h in _direct_pairs(reverse):
        p = q[h] * jnp.exp2(b[h] - ckey_ref[d, pl.ds(r0 + s, 1), :])
        if h == s // _HALF:
            t0 = s - _HALF * h
            p = jnp.where((t_idx <= t0) if reverse else (t_idx >= t0), p, 0.0)
        pieces.append(p)
    rows = len(pieces) // 2 * _HALF
    lhs = jnp.concatenate(pieces, axis=0).astype(BF16)
    return jnp.concatenate([lhs[:rows], lhs[rows:]], axis=1)


def _hgrn_direct_combine(d, r0, rs_ref, row0, *, v_ref):
    pairs = _direct_pairs(d == 1)
    n_half = len(pairs) // 2
    out = [jnp.zeros((_HALF, C_DK), F32), jnp.zeros((_HALF, C_DK), F32)]
    for p_idx, (s, h) in enumerate(pairs):
        col, row = divmod(p_idx, n_half)
        rp = rs_ref[row0 + row * _HALF:row0 + (row + 1) * _HALF, col * C_DK:(col + 1) * C_DK]
        out[h] = out[h] + rp * v_ref[0, pl.ds(r0 + s, 1), :]
    return jnp.concatenate(out, axis=0)


def _pivot_plan(reverse):
    s, h = HG_SUB, HG_CHUNK // 2
    assert HG_CHUNK == 4 * HG_SUB
    if reverse:
        return [(0, h, h, h, h), (0, s, s, s, s), (h, s, h + s, s, h + s)]
    return [(h, h, 0, h, h - 1), (s, s, 0, s, s - 1), (h + s, s, h, s, h + s - 1)]


def _hgrn_output(ti, carry, *, v_ref, o_ref, bcum_ref, ckey_ref, qs_ref, vb_ref, sbf_ref,
                 w2_ref, pmask_ref, lhs_ref, rs_ref):
    c, sub = HG_CHUNK, HG_SUB
    n_sub = c // sub
    blk_rows = len(_direct_pairs(False)) // 2 * _HALF
    rows = [pl.multiple_of((ti * HG_PAR + u) * c, c) for u in range(HG_PAR)]
    for u, r in enumerate(rows):
        for d in range(2):
            for i in range(n_sub):
                row0 = ((u * 2 + d) * n_sub + i) * blk_rows
                lhs_ref[row0:row0 + blk_rows, :] = _hgrn_direct_pieces(
                    d, r + i * sub, bcum_ref=bcum_ref, ckey_ref=ckey_ref, qs_ref=qs_ref)
    rs_ref[...] = _dot(lhs_ref[...], w2_ref[...])
    pv_groups = []
    for g in range(HG_PAR // HG_PIV):
        q_parts, k_parts, v_parts = [], [], []
        for r in rows[g * HG_PIV:(g + 1) * HG_PIV]:
            for d in range(2):
                for q0, qn, k0, kn, prow in _pivot_plan(d == 1):
                    piv = bcum_ref[d, pl.ds(r + prow, 1), :]
                    q_parts.append(qs_ref[pl.ds(r + q0, qn), :]
                                   * jnp.exp2(bcum_ref[d, pl.ds(r + q0, qn), :] - piv))
                    k_parts.append(jnp.exp2(piv - ckey_ref[d, pl.ds(r + k0, kn), :]))
                    v_parts.append(vb_ref[pl.ds(r + k0, kn), :])
        sc = _dot_nt(jnp.concatenate(q_parts, axis=0).astype(BF16),
                     jnp.concatenate(k_parts, axis=0).astype(BF16)) * pmask_ref[...]
        pv_groups.append(_dot(sc.astype(BF16), jnp.concatenate(v_parts, axis=0)))
    for u, r in enumerate(rows):
        ci = ti * HG_PAR + u
        total = None
        for d in range(2):
            blocks = [_hgrn_direct_combine(d, r + i * sub, rs_ref,
                                           ((u * 2 + d) * n_sub + i) * blk_rows, v_ref=v_ref)
                      for i in range(n_sub)]
            pv = pv_groups[u // HG_PIV]
            off = ((u % HG_PIV) * 2 + d) * c
            for q0, qn, _, _, _ in _pivot_plan(d == 1):
                for j in range(qn // sub):
                    blocks[q0 // sub + j] = (blocks[q0 // sub + j]
                                             + pv[off + j * sub:off + (j + 1) * sub])
                off += qn
            qe = (qs_ref[pl.ds(r, c), :] * jnp.exp2(bcum_ref[d, pl.ds(r, c), :])).astype(BF16)
            o_d = _dot_nt(qe, sbf_ref[d, ci]) + jnp.concatenate(blocks, axis=0)
            total = o_d if total is None else total + o_d
        o_ref[0, pl.ds(r, c), :] = total
    return carry


def _hgrn_body(q_ref, ff_ref, fb_ref, v_ref, lbr_ref, o_ref,
               bcum_ref, ckey_ref, qs_ref, vb_ref, upd_ref, dec_ref, sbf_ref, tri_ref, w2_ref,
               pmask_ref, lhs_ref, rs_ref, *, depth, layer_idx):
    s = q_ref.shape[1]
    c = HG_CHUNK
    nc = s // c
    assert nc % HG_PAR == 0 and nc % HG_PREP == 0 and HG_PAR % HG_PIV == 0

    def lower_bound(d):
        rows = lbr_ref[d * depth:(d + 1) * depth, :]
        e = jnp.exp(rows - jnp.max(rows, axis=0, keepdims=True))
        sm = e / jnp.sum(e, axis=0, keepdims=True)
        acc = sm[0:1] * 0.0
        for l in range(1, layer_idx + 1):
            acc = acc + sm[l:l + 1]
        return acc

    row = lax.broadcasted_iota(jnp.int32, (c, 3 * c), 0)
    col = lax.broadcasted_iota(jnp.int32, (c, 3 * c), 1) % c
    tri_ref[0] = (col <= row).astype(BF16)
    tri_ref[1] = (col >= row).astype(BF16)
    wr = lax.broadcasted_iota(jnp.int32, (2 * C_DK, 2 * C_DK), 0) // C_DK
    wc = lax.broadcasted_iota(jnp.int32, (2 * C_DK, 2 * C_DK), 1) // C_DK
    w2_ref[...] = (wr == wc).astype(BF16)
    n = HG_PIV * 2 * c
    pr = lax.broadcasted_iota(jnp.int32, (n, n), 0)
    pc = lax.broadcasted_iota(jnp.int32, (n, n), 1)
    entry = lambda x: jnp.where(x % c < c // 2, 0, (x % c) // HG_SUB)
    pmask_ref[...] = ((pr // c == pc // c) & (entry(pr) == entry(pc))).astype(F32)

    lax.fori_loop(0, nc // HG_PREP, functools.partial(
        _hgrn_tables, x_refs=(ff_ref, fb_ref), lbs=(lower_bound(0), lower_bound(1)), q_ref=q_ref,
        v_ref=v_ref, tri_ref=tri_ref, bcum_ref=bcum_ref, ckey_ref=ckey_ref, qs_ref=qs_ref,
        vb_ref=vb_ref, upd_ref=upd_ref, dec_ref=dec_ref), 0)
    zero = jnp.zeros((C_DK, C_DK), F32)
    lax.fori_loop(0, nc, functools.partial(
        _hgrn_scan, upd_ref=upd_ref, dec_ref=dec_ref, sbf_ref=sbf_ref, nc=nc), (zero, zero))
    lax.fori_loop(0, nc // HG_PAR, functools.partial(
        _hgrn_output, v_ref=v_ref, o_ref=o_ref, bcum_ref=bcum_ref, ckey_ref=ckey_ref,
        qs_ref=qs_ref, vb_ref=vb_ref, sbf_ref=sbf_ref, w2_ref=w2_ref, pmask_ref=pmask_ref,
        lhs_ref=lhs_ref, rs_ref=rs_ref), 0)


def hgrn2(proj, lower_bounds, layer_idx):
    b, s, _ = proj.shape
    depth = lower_bounds.shape[1]
    dk = C_DK
    nc = s // HG_CHUNK
    direct_rows = HG_PAR * 2 * (HG_CHUNK // HG_SUB) * (len(_direct_pairs(False)) // 2 * _HALF)
    pivot_rows = HG_PIV * 2 * HG_CHUNK
    blk = lambda off: pl.BlockSpec((1, s, dk), lambda bi, h: (bi, 0, off * C_HEADS + h))
    return pl.pallas_call(
        functools.partial(_hgrn_body, depth=depth, layer_idx=layer_idx),
        grid=(b, C_HEADS),
        in_specs=[blk(0), blk(1), blk(2), blk(3),
                  pl.BlockSpec((2 * depth, dk), lambda bi, h: (0, h))],
        out_specs=pl.BlockSpec((1, s, dk), lambda bi, h: (bi, 0, h)),
        out_shape=jax.ShapeDtypeStruct((b, s, C_HEADS * dk), F32),
        scratch_shapes=[pltpu.VMEM((2, s, dk), F32),
                        pltpu.VMEM((2, s, dk), F32),
                        pltpu.VMEM((s, dk), F32),
                        pltpu.VMEM((s, dk), BF16),
                        pltpu.VMEM((nc, dk, 2 * dk), F32),
                        pltpu.VMEM((nc, 1, 2 * dk), F32),
                        pltpu.VMEM((2, nc, dk, dk), BF16),
                        pltpu.VMEM((2, HG_CHUNK, 3 * HG_CHUNK), BF16),
                        pltpu.VMEM((2 * dk, 2 * dk), BF16),
                        pltpu.VMEM((pivot_rows, pivot_rows), F32),
                        pltpu.VMEM((direct_rows, 2 * dk), BF16),
                        pltpu.VMEM((direct_rows, 2 * dk), F32)],
        compiler_params=_params(("parallel", "parallel"), 40),
        name="hgrn2",
    )(proj, proj, proj, proj, lower_bounds.reshape(2 * depth, C_HEADS * dk))


def kernel(x, mix_norm, even_w_in, even_w_out, diff_lq1, diff_lk1, diff_lq2, diff_lk2,
           diff_subln, gmlp_ln_g, gmlp_ln_b, gmlp_w_s, gmlp_b_s, hgrn_w_in, hgrn_w_out,
           hgrn_lower_bounds, hgrn_g_norm, ffn_norm, ffn_w_gate, ffn_w_up, ffn_w_down,
           final_norm):
    b, s, d = x.shape
    depth = mix_norm.shape[0]
    m = b * s
    h = x.reshape(m, d)
    (even_w_in, even_w_out, hgrn_w_in, hgrn_w_out, w_gate, w_up, w_down) = (
        w.astype(BF16) for w in (even_w_in, even_w_out, hgrn_w_in, hgrn_w_out,
                                 ffn_w_gate, ffn_w_up, ffn_w_down))
    for l in range(depth):
        if l % 2 == 0:
            e = l // 2
            proj = norm_proj(h, mix_norm[l], even_w_in, e).reshape(b, s, -1)
            lam_init = 0.8 - 0.6 * math.exp(-0.3 * l)
            lam_vec = jnp.stack([diff_lq1[e], diff_lk1[e], diff_lq2[e], diff_lk2[e]]).astype(F32)
            oa = diff_attention(proj, lam_vec, diff_subln[e], lam_init)
            u_block = (3 * A_HEADS * A_V_DIM) // (B_GROUPS * B_GROUP_DIM)
            ob = gmlp(proj, gmlp_ln_g[e], gmlp_ln_b[e], gmlp_w_s[e], gmlp_b_s[e], u_block)
            h = out_proj2(oa.reshape(m, -1), ob.reshape(m, -1), even_w_out, e, h)
        else:
            o = l // 2
            proj = norm_proj(h, mix_norm[l], hgrn_w_in, o)
            scan = hgrn2(proj.reshape(b, s, -1), hgrn_lower_bounds, l)
            h = out_proj_gated(scan.reshape(m, -1), proj, 4, hgrn_g_norm[o], hgrn_w_out, o, h)
        last = l == depth - 1
        h = ffn(h, ffn_norm[l], w_gate, w_up, w_down, l,
                final_gain=final_norm if last else None)
    return h.reshape(b, s, d)
```

```python
import functools
import math

import jax
import jax.numpy as jnp
from jax import lax
from jax.experimental import pallas as pl
from jax.experimental.pallas import tpu as pltpu

F32 = jnp.float32
BF16 = jnp.bfloat16
EPS = 1e-6
MIB = 1024 * 1024

A_HEADS = 8
A_QK_DIM = 64
A_V_DIM = 128
ROPE_THETA = 500000.0
ROPE_DIM = A_QK_DIM // 4
B_GROUPS = 8
B_GROUP_DIM = 128
B_CHUNK = 128
C_HEADS = 16
C_DK = 128
HG_CHUNK = 64
HG_SUB = 16
HG_PAR = 8
HG_PIV = 2
HG_PREP = 8


def _params(semantics, vmem_mib):
    return pltpu.CompilerParams(dimension_semantics=semantics,
                                vmem_limit_bytes=vmem_mib * MIB)


def _rms(x, gain):
    ms = jnp.mean(x * x, axis=-1, keepdims=True)
    return x * lax.rsqrt(ms + EPS) * gain


def _sigmoid(x):
    return 1.0 / (1.0 + jnp.exp(-x))


def _gelu_tanh(x):
    c = math.sqrt(2.0 / math.pi)
    return x * (0.5 * (1.0 + jnp.tanh(c * (x + 0.044715 * (x * x * x)))))


def _dot(a, b):
    return jnp.dot(a, b, preferred_element_type=F32)


def _dot_nt(a, b):
    return lax.dot_general(a, b, (((1,), (1,)), ((), ())), preferred_element_type=F32)


def _dot_tn(a, b):
    return lax.dot_general(a, b, (((0,), (0,)), ((), ())), preferred_element_type=F32)


def _norm_proj_body(x_ref, g_ref, w_ref, o_ref, xn_ref):
    @pl.when(pl.program_id(1) == 0)
    def _():
        xn_ref[...] = _rms(x_ref[...], g_ref[...]).astype(BF16)

    o_ref[...] = _dot(xn_ref[...], w_ref[...])


def norm_proj(x, gain, w, layer, tm=1024, tn=1024):
    m, k = x.shape
    n = w.shape[2]
    tm, tn = min(tm, m), min(tn, n)
    return pl.pallas_call(
        _norm_proj_body,
        grid=(m // tm, n // tn),
        in_specs=[pl.BlockSpec((tm, k), lambda i, j: (i, 0)),
                  pl.BlockSpec((1, k), lambda i, j: (0, 0)),
                  pl.BlockSpec((None, k, tn), lambda i, j: (layer, 0, j))],
        out_specs=pl.BlockSpec((tm, tn), lambda i, j: (i, j)),
        out_shape=jax.ShapeDtypeStruct((m, n), F32),
        scratch_shapes=[pltpu.VMEM((tm, k), BF16)],
        compiler_params=_params(("parallel", "arbitrary"), 48),
        name="norm_proj",
    )(x, gain.reshape(1, k), w)


def _rope(x, cos, sin_lo, sin_hi):
    half = ROPE_DIM // 2
    return (x * cos + pltpu.roll(x, half, 1) * sin_lo
            + pltpu.roll(x, x.shape[1] - half, 1) * sin_hi)


def _attn_body(*refs, lam_init, tq, n_cast):
    lam_ref, subln_ref, cos_ref, slo_ref, shi_ref, q_ref, k_ref, v_ref = refs[:8]
    w_refs = refs[8:8 + n_cast]
    o_ref = refs[8 + n_cast]
    wb_refs = refs[9 + n_cast:9 + 2 * n_cast]
    kb_ref, vb_ref = refs[9 + 2 * n_cast:]
    for w_ref, wb_ref in zip(w_refs, wb_refs):
        wb_ref[...] = w_ref[...].astype(BF16)
    s_len = k_ref.shape[1]
    kb_ref[...] = _rope(k_ref[0], cos_ref[...], slo_ref[...], shi_ref[...]).astype(BF16)
    vb_ref[:, 0:A_V_DIM] = v_ref[0].astype(BF16)
    vb_ref[:, A_V_DIM:] = jnp.ones((s_len, A_V_DIM), BF16)
    lv = lam_ref[...]
    lam = (jnp.exp(jnp.sum(lv[0:1] * lv[1:2], axis=-1, keepdims=True))
           - jnp.exp(jnp.sum(lv[2:3] * lv[3:4], axis=-1, keepdims=True)) + lam_init)
    scale = A_QK_DIM ** -0.5 * math.log2(math.e)
    lane = lax.broadcasted_iota(jnp.int32, (tq, A_V_DIM), 1)
    for j in range(s_len // tq):
        rows = slice(j * tq, (j + 1) * tq)
        q = _rope(q_ref[0, rows, :], cos_ref[rows, :], slo_ref[rows, :], shi_ref[rows, :]) * scale
        q2 = jnp.concatenate([jnp.where(lane < A_QK_DIM, q, 0.0),
                              jnp.where(lane >= A_QK_DIM, q, 0.0)], axis=0).astype(BF16)
        s = _dot_nt(q2, kb_ref[...])
        p = jnp.exp2(s - jnp.max(s, axis=-1, keepdims=True))
        ov = _dot(p.astype(BF16), vb_ref[...])
        on = ov[:, 0:A_V_DIM] / ov[:, A_V_DIM:]
        o = on[0:tq] - lam * on[tq:]
        o_ref[0, rows, :] = (_rms(o, subln_ref[...]) * (1.0 - lam_init)).astype(BF16)


def _rope_tables(s):
    half = ROPE_DIM // 2
    pos = jnp.arange(s, dtype=F32)
    inv_freq = ROPE_THETA ** (-jnp.arange(half, dtype=F32) / half)
    ang = pos[:, None] * inv_freq[None, :]
    cos, sin = jnp.cos(ang), jnp.sin(ang)
    pad = A_QK_DIM - ROPE_DIM
    one = jnp.ones((s, pad), F32)
    zero = jnp.zeros((s, pad), F32)
    zh = jnp.zeros((s, half), F32)
    cos_c = jnp.concatenate([cos, cos, one], axis=1)
    lo_c = jnp.concatenate([zh, sin, zero], axis=1)
    hi_c = jnp.concatenate([-sin, zh, zero], axis=1)
    rep = lambda t: jnp.concatenate([t, t], axis=1)
    return rep(cos_c), rep(lo_c), rep(hi_c)


def diff_attention(proj, lam_vec, subln, lam_init, cast_weights=(), tq=128):
    b, s, _ = proj.shape
    tq = min(tq, s)
    cos, lo, hi = _rope_tables(s)
    hd = A_V_DIM
    steps = b * A_HEADS
    table = pl.BlockSpec((s, hd), lambda bi, h: (0, 0))
    head = lambda off: pl.BlockSpec((1, s, hd), lambda bi, h: (bi, 0, off * A_HEADS + h))

    def slab(w):
        layers, rows, cols = w.shape
        per_layer = steps // layers
        assert steps % layers == 0 and rows % (16 * per_layer) == 0
        return pl.BlockSpec((1, rows // per_layer, cols), lambda bi, h: (
            (bi * A_HEADS + h) // per_layer, (bi * A_HEADS + h) % per_layer, 0))

    slabs = [slab(w) for w in cast_weights]
    outs = pl.pallas_call(
        functools.partial(_attn_body, lam_init=lam_init, tq=tq, n_cast=len(cast_weights)),
        grid=(b, A_HEADS),
        in_specs=[pl.BlockSpec((4, A_QK_DIM), lambda bi, h: (0, 0)),
                  pl.BlockSpec((1, hd), lambda bi, h: (0, 0)),
                  table, table, table, head(0), head(1), head(2)] + slabs,
        out_specs=[pl.BlockSpec((1, s, hd), lambda bi, h: (bi, 0, h))] + slabs,
        out_shape=[jax.ShapeDtypeStruct((b, s, A_HEADS * hd), BF16)]
        + [jax.ShapeDtypeStruct(w.shape, BF16) for w in cast_weights],
        scratch_shapes=[pltpu.VMEM((s, hd), BF16), pltpu.VMEM((s, 2 * hd), BF16)],
        compiler_params=_params(("parallel", "parallel"), 56),
        name="diff_attention",
    )(lam_vec, subln.reshape(1, hd), cos, lo, hi, proj, proj, proj, *cast_weights)
    return outs[0], tuple(outs[1:])


def _gmlp_body(u_ref, v_ref, lng_ref, lnb_ref, ws_ref, bs_ref, o_ref, *, n_chunks):
    gd = B_GROUP_DIM
    for c in range(n_chunks):
        rows = slice(c * B_CHUNK, (c + 1) * B_CHUNK)
        vb = _gelu_tanh(v_ref[0, rows, :])
        xc = vb - jnp.mean(vb, axis=-1, keepdims=True)
        var = jnp.mean(xc * xc, axis=-1, keepdims=True)
        y = (xc * lax.rsqrt(var + EPS) * lng_ref[...] + lnb_ref[...]).astype(BF16)
        u = _gelu_tanh(u_ref[0, rows, :])
        for g in range(B_GROUPS):
            cols = slice(g * gd, (g + 1) * gd)
            sv = _dot(ws_ref[g], y[:, cols]) + bs_ref[:, g:g + 1]
            o_ref[0, rows, cols] = (u[:, cols] * sv).astype(BF16)


def gmlp(proj, ln_g, ln_b, w_s, b_s, u_block, n_chunks=4):
    b, s, _ = proj.shape
    width = B_GROUPS * B_GROUP_DIM
    n_chunks = min(n_chunks, s // B_CHUNK)
    t = n_chunks * B_CHUNK
    return pl.pallas_call(
        functools.partial(_gmlp_body, n_chunks=n_chunks),
        grid=(b, s // t),
        in_specs=[pl.BlockSpec((1, t, width), lambda bi, n: (bi, n, u_block)),
                  pl.BlockSpec((1, t, width), lambda bi, n: (bi, n, u_block + 1)),
                  pl.BlockSpec((1, width), lambda bi, n: (0, 0)),
                  pl.BlockSpec((1, width), lambda bi, n: (0, 0)),
                  pl.BlockSpec((B_GROUPS, B_CHUNK, B_CHUNK), lambda bi, n: (0, 0, 0)),
                  pl.BlockSpec((B_CHUNK, B_GROUPS), lambda bi, n: (0, 0))],
        out_specs=pl.BlockSpec((1, t, width), lambda bi, n: (bi, n, 0)),
        out_shape=jax.ShapeDtypeStruct((b, s, width), BF16),
        compiler_params=_params(("parallel", "parallel"), 32),
        name="gmlp",
    )(proj, proj, ln_g.reshape(1, width), ln_b.reshape(1, width), w_s.astype(BF16), b_s.T)


def _out_proj2_body(a_ref, b_ref, w_ref, r_ref, o_ref):
    ka = a_ref.shape[1]
    acc = _dot(a_ref[...], w_ref[0:ka, :]) + _dot(b_ref[...], w_ref[ka:, :])
    o_ref[...] = r_ref[...] + acc


def out_proj2(a, b, w, layer, res, tm=512, tn=2048):
    m, ka = a.shape
    kb = b.shape[1]
    n = w.shape[2]
    tm, tn = min(tm, m), min(tn, n)
    return pl.pallas_call(
        _out_proj2_body,
        grid=(m // tm, n // tn),
        in_specs=[pl.BlockSpec((tm, ka), lambda i, j: (i, 0)),
                  pl.BlockSpec((tm, kb), lambda i, j: (i, 0)),
                  pl.BlockSpec((None, ka + kb, tn), lambda i, j: (layer, 0, j)),
                  pl.BlockSpec((tm, tn), lambda i, j: (i, j))],
        out_specs=pl.BlockSpec((tm, tn), lambda i, j: (i, j)),
        out_shape=jax.ShapeDtypeStruct((m, n), F32),
        compiler_params=_params(("parallel", "parallel"), 48),
        name="out_proj_even",
    )(a, b, w, res)


def _out_proj_gated_body(o_in_ref, gate_ref, gn_ref, w_ref, r_ref, o_ref, y_ref):
    @pl.when(pl.program_id(1) == 0)
    def _():
        gate = gate_ref[...]
        y = _rms(o_in_ref[...], gn_ref[...]) * (gate * _sigmoid(gate))
        y_ref[...] = y.astype(BF16)

    o_ref[...] = r_ref[...] + _dot(y_ref[...], w_ref[...])


def out_proj_gated(o_in, proj, gate_block, gain, w, layer, res, tm=512, tn=2048):
    m, k = o_in.shape
    n = w.shape[2]
    tm, tn = min(tm, m), min(tn, n)
    return pl.pallas_call(
        _out_proj_gated_body,
        grid=(m // tm, n // tn),
        in_specs=[pl.BlockSpec((tm, k), lambda i, j: (i, 0)),
                  pl.BlockSpec((tm, k), lambda i, j: (i, gate_block)),
                  pl.BlockSpec((1, k), lambda i, j: (0, 0)),
                  pl.BlockSpec((None, k, tn), lambda i, j: (layer, 0, j)),
                  pl.BlockSpec((tm, tn), lambda i, j: (i, j))],
        out_specs=pl.BlockSpec((tm, tn), lambda i, j: (i, j)),
        out_shape=jax.ShapeDtypeStruct((m, n), F32),
        scratch_shapes=[pltpu.VMEM((tm, k), BF16)],
        compiler_params=_params(("parallel", "arbitrary"), 56),
        name="out_proj_odd",
    )(o_in, proj, gain.reshape(1, k), w, res)


def _ffn_body(*refs, final_norm):
    if final_norm:
        x_ref, g_ref, wg_ref, wu_ref, wd_ref, fn_ref, o_ref, xn_ref = refs
    else:
        x_ref, g_ref, wg_ref, wu_ref, wd_ref, o_ref, xn_ref = refs
    f = pl.program_id(1)

    @pl.when(f == 0)
    def _():
        x = x_ref[...]
        xn_ref[...] = _rms(x, g_ref[...]).astype(BF16)
        o_ref[...] = x

    xn = xn_ref[...]
    a = _dot(xn, wg_ref[...])
    u = _dot(xn, wu_ref[...])
    hid = (a * _sigmoid(a) * u).astype(BF16)
    o_ref[...] += _dot(hid, wd_ref[...])

    if final_norm:
        @pl.when(f == pl.num_programs(1) - 1)
        def _():
            o_ref[...] = _rms(o_ref[...], fn_ref[...])


def ffn(x, gain, w_gate, w_up, w_down, layer, final_gain=None, tm=1024, tf=512):
    m, k = x.shape
    hdim = w_gate.shape[2]
    tm, tf = min(tm, m), min(tf, hdim)
    vec = pl.BlockSpec((1, k), lambda i, f: (0, 0))
    in_specs = [pl.BlockSpec((tm, k), lambda i, f: (i, 0)), vec,
                pl.BlockSpec((None, k, tf), lambda i, f: (layer, 0, f)),
                pl.BlockSpec((None, k, tf), lambda i, f: (layer, 0, f)),
                pl.BlockSpec((None, tf, k), lambda i, f: (layer, f, 0))]
    args = [x, gain.reshape(1, k), w_gate, w_up, w_down]
    if final_gain is not None:
        in_specs.append(vec)
        args.append(final_gain.reshape(1, k))
    return pl.pallas_call(
        functools.partial(_ffn_body, final_norm=final_gain is not None),
        grid=(m // tm, hdim // tf),
        in_specs=in_specs,
        out_specs=pl.BlockSpec((tm, k), lambda i, f: (i, 0)),
        out_shape=jax.ShapeDtypeStruct((m, k), F32),
        scratch_shapes=[pltpu.VMEM((tm, k), BF16)],
        compiler_params=_params(("parallel", "arbitrary"), 60),
        name="ffn",
    )(*args)


_HALF = 8


def _hgrn_tables(ti, carry, *, x_refs, lbs, q_ref, v_ref, tri_ref, bcum_ref, ckey_ref,
                 qs_ref, vb_ref, upd_ref, dec_ref):
    c = HG_CHUNK
    rows = [pl.multiple_of((ti * HG_PREP + u) * c, c) for u in range(HG_PREP)]
    kdec = [[None, None] for _ in range(HG_PREP)]
    decay = [[None, None] for _ in range(HG_PREP)]
    for d in range(2):
        lb = lbs[d]
        splits, log_k = [], []
        for r in rows:
            f = lb + (1.0 - lb) * _sigmoid(x_refs[d][0, pl.ds(r, c), :])
            logf = jnp.log2(f)
            log_k.append(jnp.log2(1.0 - f))
            hi = logf.astype(BF16)
            rem = logf - hi.astype(F32)
            mid = rem.astype(BF16)
            low = (rem - mid.astype(F32)).astype(BF16)
            splits.append(jnp.concatenate([hi, mid, low], axis=0))
        bc = _dot(tri_ref[d], jnp.concatenate(splits, axis=1))
        for u, r in enumerate(rows):
            bcum = bc[:, u * C_DK:(u + 1) * C_DK]
            ckey = bcum - log_k[u]
            bcum_ref[d, pl.ds(r, c), :] = bcum
            ckey_ref[d, pl.ds(r, c), :] = ckey
            edge = bcum[c - 1:c] if d == 0 else bcum[0:1]
            kdec[u][d] = jnp.exp2(edge - ckey).astype(BF16)
            decay[u][d] = jnp.exp2(edge)
    for u, r in enumerate(rows):
        ci = ti * HG_PREP + u
        vb = v_ref[0, pl.ds(r, c), :].astype(BF16)
        vb_ref[pl.ds(r, c), :] = vb
        upd_ref[ci] = _dot_tn(vb, jnp.concatenate(kdec[u], axis=1))
        dec_ref[ci] = jnp.concatenate(decay[u], axis=1)
        q = q_ref[0, pl.ds(r, c), :]
        qs_ref[pl.ds(r, c), :] = q * _sigmoid(q)
    return carry


def _hgrn_scan(ci, carry, *, upd_ref, dec_ref, sbf_ref, nc):
    st_f, st_b = carry
    cb = nc - 1 - ci
    sbf_ref[0, ci] = st_f.astype(BF16)
    sbf_ref[1, cb] = st_b.astype(BF16)
    st_f = st_f * dec_ref[ci, :, 0:C_DK] + upd_ref[ci, :, 0:C_DK]
    st_b = st_b * dec_ref[cb, :, C_DK:2 * C_DK] + upd_ref[cb, :, C_DK:2 * C_DK]
    return st_f, st_b


def _direct_pairs(reverse):
    pairs = []
    for s in range(HG_SUB):
        for h in range(2):
            sh = s // _HALF
            if not ((h > sh) if reverse else (h < sh)):
                pairs.append((s, h))
    return pairs


def _hgrn_direct_pieces(d, r0, *, bcum_ref, ckey_ref, qs_ref):
    reverse = d == 1
    t_idx = lax.broadcasted_iota(jnp.int32, (_HALF, C_DK), 0)
    b = [bcum_ref[d, pl.ds(r0 + _HALF * h, _HALF), :] for h in range(2)]
    q = [qs_ref[pl.ds(r0 + _HALF * h, _HALF), :] for h in range(2)]
    pieces = []
    for s, h in _direct_pairs(reverse):
        p = q[h] * jnp.exp2(b[h] - ckey_ref[d, pl.ds(r0 + s, 1), :])
        if h == s // _HALF:
            t0 = s - _HALF * h
            p = jnp.where((t_idx <= t0) if reverse else (t_idx >= t0), p, 0.0)
        pieces.append(p)
    rows = len(pieces) // 2 * _HALF
    lhs = jnp.concatenate(pieces, axis=0).astype(BF16)
    return jnp.concatenate([lhs[:rows], lhs[rows:]], axis=1)


def _hgrn_direct_combine(d, r0, rs_ref, row0, *, v_ref):
    pairs = _direct_pairs(d == 1)
    n_half = len(pairs) // 2
    out = [jnp.zeros((_HALF, C_DK), F32), jnp.zeros((_HALF, C_DK), F32)]
    for p_idx, (s, h) in enumerate(pairs):
        col, row = divmod(p_idx, n_half)
        rp = rs_ref[row0 + row * _HALF:row0 + (row + 1) * _HALF, col * C_DK:(col + 1) * C_DK]
        out[h] = out[h] + rp * v_ref[0, pl.ds(r0 + s, 1), :]
    return jnp.concatenate(out, axis=0)


def _pivot_plan(reverse):
    s, h = HG_SUB, HG_CHUNK // 2
    assert HG_CHUNK == 4 * HG_SUB
    if reverse:
        return [(0, h, h, h, h), (0, s, s, s, s), (h, s, h + s, s, h + s)]
    return [(h, h, 0, h, h - 1), (s, s, 0, s, s - 1), (h + s, s, h, s, h + s - 1)]


def _hgrn_output(ti, carry, *, v_ref, o_ref, bcum_ref, ckey_ref, qs_ref, vb_ref, sbf_ref,
                 w2_ref, pmask_ref, lhs_ref, rs_ref):
    c, sub = HG_CHUNK, HG_SUB
    n_sub = c // sub
    blk_rows = len(_direct_pairs(False)) // 2 * _HALF
    rows = [pl.multiple_of((ti * HG_PAR + u) * c, c) for u in range(HG_PAR)]
    for u, r in enumerate(rows):
        for d in range(2):
            for i in range(n_sub):
                row0 = ((u * 2 + d) * n_sub + i) * blk_rows
                lhs_ref[row0:row0 + blk_rows, :] = _hgrn_direct_pieces(
                    d, r + i * sub, bcum_ref=bcum_ref, ckey_ref=ckey_ref, qs_ref=qs_ref)
    rs_ref[...] = _dot(lhs_ref[...], w2_ref[...])
    pv_groups = []
    for g in range(HG_PAR // HG_PIV):
        q_parts, k_parts, v_parts = [], [], []
        for r in rows[g * HG_PIV:(g + 1) * HG_PIV]:
            for d in range(2):
                for q0, qn, k0, kn, prow in _pivot_plan(d == 1):
                    piv = bcum_ref[d, pl.ds(r + prow, 1), :]
                    q_parts.append(qs_ref[pl.ds(r + q0, qn), :]
                                   * jnp.exp2(bcum_ref[d, pl.ds(r + q0, qn), :] - piv))
                    k_parts.append(jnp.exp2(piv - ckey_ref[d, pl.ds(r + k0, kn), :]))
                    v_parts.append(vb_ref[pl.ds(r + k0, kn), :])
        sc = _dot_nt(jnp.concatenate(q_parts, axis=0).astype(BF16),
                     jnp.concatenate(k_parts, axis=0).astype(BF16)) * pmask_ref[...]
        pv_groups.append(_dot(sc.astype(BF16), jnp.concatenate(v_parts, axis=0)))
    for u, r in enumerate(rows):
        ci = ti * HG_PAR + u
        total = None
        for d in range(2):
            blocks = [_hgrn_direct_combine(d, r + i * sub, rs_ref,
                                           ((u * 2 + d) * n_sub + i) * blk_rows, v_ref=v_ref)
                      for i in range(n_sub)]
            pv = pv_groups[u // HG_PIV]
            off = ((u % HG_PIV) * 2 + d) * c
            for q0, qn, _, _, _ in _pivot_plan(d == 1):
                for j in range(qn // sub):
                    blocks[q0 // sub + j] = (blocks[q0 // sub + j]
                                             + pv[off + j * sub:off + (j + 1) * sub])
                off += qn
            qe = (qs_ref[pl.ds(r, c), :] * jnp.exp2(bcum_ref[d, pl.ds(r, c), :])).astype(BF16)
            o_d = _dot_nt(qe, sbf_ref[d, ci]) + jnp.concatenate(blocks, axis=0)
            total = o_d if total is None else total + o_d
        o_ref[0, pl.ds(r, c), :] = total
    return carry


def _hgrn_body(q_ref, ff_ref, fb_ref, v_ref, lbr_ref, o_ref,
               bcum_ref, ckey_ref, qs_ref, vb_ref, upd_ref, dec_ref, sbf_ref, tri_ref, w2_ref,
               pmask_ref, lhs_ref, rs_ref, *, depth, layer_idx):
    s = q_ref.shape[1]
    c = HG_CHUNK
    nc = s // c
    assert nc % HG_PAR == 0 and nc % HG_PREP == 0 and HG_PAR % HG_PIV == 0

    def lower_bound(d):
        rows = lbr_ref[d * depth:(d + 1) * depth, :]
        e = jnp.exp(rows - jnp.max(rows, axis=0, keepdims=True))
        sm = e / jnp.sum(e, axis=0, keepdims=True)
        acc = sm[0:1] * 0.0
        for l in range(1, layer_idx + 1):
            acc = acc + sm[l:l + 1]
        return acc

    row = lax.broadcasted_iota(jnp.int32, (c, 3 * c), 0)
    col = lax.broadcasted_iota(jnp.int32, (c, 3 * c), 1) % c
    tri_ref[0] = (col <= row).astype(BF16)
    tri_ref[1] = (col >= row).astype(BF16)
    wr = lax.broadcasted_iota(jnp.int32, (2 * C_DK, 2 * C_DK), 0) // C_DK
    wc = lax.broadcasted_iota(jnp.int32, (2 * C_DK, 2 * C_DK), 1) // C_DK
    w2_ref[...] = (wr == wc).astype(BF16)
    n = HG_PIV * 2 * c
    pr = lax.broadcasted_iota(jnp.int32, (n, n), 0)
    pc = lax.broadcasted_iota(jnp.int32, (n, n), 1)
    entry = lambda x: jnp.where(x % c < c // 2, 0, (x % c) // HG_SUB)
    pmask_ref[...] = ((pr // c == pc // c) & (entry(pr) == entry(pc))).astype(F32)

    lax.fori_loop(0, nc // HG_PREP, functools.partial(
        _hgrn_tables, x_refs=(ff_ref, fb_ref), lbs=(lower_bound(0), lower_bound(1)), q_ref=q_ref,
        v_ref=v_ref, tri_ref=tri_ref, bcum_ref=bcum_ref, ckey_ref=ckey_ref, qs_ref=qs_ref,
        vb_ref=vb_ref, upd_ref=upd_ref, dec_ref=dec_ref), 0)
    zero = jnp.zeros((C_DK, C_DK), F32)
    lax.fori_loop(0, nc, functools.partial(
        _hgrn_scan, upd_ref=upd_ref, dec_ref=dec_ref, sbf_ref=sbf_ref, nc=nc), (zero, zero))
    lax.fori_loop(0, nc // HG_PAR, functools.partial(
        _hgrn_output, v_ref=v_ref, o_ref=o_ref, bcum_ref=bcum_ref, ckey_ref=ckey_ref,
        qs_ref=qs_ref, vb_ref=vb_ref, sbf_ref=sbf_ref, w2_ref=w2_ref, pmask_ref=pmask_ref,
        lhs_ref=lhs_ref, rs_ref=rs_ref), 0)


def hgrn2(proj, lower_bounds, layer_idx):
    b, s, _ = proj.shape
    depth = lower_bounds.shape[1]
    dk = C_DK
    nc = s // HG_CHUNK
    direct_rows = HG_PAR * 2 * (HG_CHUNK // HG_SUB) * (len(_direct_pairs(False)) // 2 * _HALF)
    pivot_rows = HG_PIV * 2 * HG_CHUNK
    blk = lambda off: pl.BlockSpec((1, s, dk), lambda bi, h: (bi, 0, off * C_HEADS + h))
    return pl.pallas_call(
        functools.partial(_hgrn_body, depth=depth, layer_idx=layer_idx),
        grid=(b, C_HEADS),
        in_specs=[blk(0), blk(1), blk(2), blk(3),
                  pl.BlockSpec((2 * depth, dk), lambda bi, h: (0, h))],
        out_specs=pl.BlockSpec((1, s, dk), lambda bi, h: (bi, 0, h)),
        out_shape=jax.ShapeDtypeStruct((b, s, C_HEADS * dk), F32),
        scratch_shapes=[pltpu.VMEM((2, s, dk), F32),
                        pltpu.VMEM((2, s, dk), F32),
                        pltpu.VMEM((s, dk), F32),
                        pltpu.VMEM((s, dk), BF16),
                        pltpu.VMEM((nc, dk, 2 * dk), F32),
                        pltpu.VMEM((nc, 1, 2 * dk), F32),
                        pltpu.VMEM((2, nc, dk, dk), BF16),
                        pltpu.VMEM((2, HG_CHUNK, 3 * HG_CHUNK), BF16),
                        pltpu.VMEM((2 * dk, 2 * dk), BF16),
                        pltpu.VMEM((pivot_rows, pivot_rows), F32),
                        pltpu.VMEM((direct_rows, 2 * dk), BF16),
                        pltpu.VMEM((direct_rows, 2 * dk), F32)],
        compiler_params=_params(("parallel", "parallel"), 40),
        name="hgrn2",
    )(proj, proj, proj, proj, lower_bounds.reshape(2 * depth, C_HEADS * dk))


def kernel(x, mix_norm, even_w_in, even_w_out, diff_lq1, diff_lk1, diff_lq2, diff_lk2,
           diff_subln, gmlp_ln_g, gmlp_ln_b, gmlp_w_s, gmlp_b_s, hgrn_w_in, hgrn_w_out,
           hgrn_lower_bounds, hgrn_g_norm, ffn_norm, ffn_w_gate, ffn_w_up, ffn_w_down,
           final_norm):
    b, s, d = x.shape
    depth = mix_norm.shape[0]
    m = b * s
    h = x.reshape(m, d)
    even_w_in, even_w_out = (w.astype(BF16) for w in (even_w_in, even_w_out))
    ffn_ws = None
    for l in range(depth):
        if l % 2 == 0:
            e = l // 2
            proj = norm_proj(h, mix_norm[l], even_w_in, e).reshape(b, s, -1)
            lam_init = 0.8 - 0.6 * math.exp(-0.3 * l)
            lam_vec = jnp.stack([diff_lq1[e], diff_lk1[e], diff_lq2[e], diff_lk2[e]]).astype(F32)
            to_cast = ((ffn_w_gate, ffn_w_up, ffn_w_down, hgrn_w_in, hgrn_w_out)
                       if ffn_ws is None else ())
            oa, cast = diff_attention(proj, lam_vec, diff_subln[e], lam_init, to_cast)
            if to_cast:
                ffn_ws, (hgrn_w_in, hgrn_w_out) = cast[:3], cast[3:]
            u_block = (3 * A_HEADS * A_V_DIM) // (B_GROUPS * B_GROUP_DIM)
            ob = gmlp(proj, gmlp_ln_g[e], gmlp_ln_b[e], gmlp_w_s[e], gmlp_b_s[e], u_block)
            h = out_proj2(oa.reshape(m, -1), ob.reshape(m, -1), even_w_out, e, h)
        else:
            o = l // 2
            proj = norm_proj(h, mix_norm[l], hgrn_w_in, o)
            scan = hgrn2(proj.reshape(b, s, -1), hgrn_lower_bounds, l)
            h = out_proj_gated(scan.reshape(m, -1), proj, 4, hgrn_g_norm[o], hgrn_w_out, o, h)
        last = l == depth - 1
        w_gate, w_up, w_down = ffn_ws
        h = ffn(h, ffn_norm[l], w_gate, w_up, w_down, l,
                final_gain=final_norm if last else None)
    return h.reshape(b, s, d)
```

```python
import functools
import math

import jax
import jax.numpy as jnp
from jax import lax
from jax.experimental import pallas as pl
from jax.experimental.pallas import tpu as pltpu

F32 = jnp.float32
BF16 = jnp.bfloat16
EPS = 1e-6
MIB = 1024 * 1024

A_HEADS = 8
A_QK_DIM = 64
A_V_DIM = 128
ROPE_THETA = 500000.0
ROPE_DIM = A_QK_DIM // 4
B_GROUPS = 8
B_GROUP_DIM = 128
B_CHUNK = 128
C_HEADS = 16
C_DK = 128
HG_CHUNK = 64
HG_SUB = 16
HG_PAR = 8
HG_PIV = 2
HG_PREP = 8


def _params(semantics, vmem_mib):
    return pltpu.CompilerParams(dimension_semantics=semantics,
                                vmem_limit_bytes=vmem_mib * MIB)


def _rms(x, gain):
    ms = jnp.mean(x * x, axis=-1, keepdims=True)
    return x * lax.rsqrt(ms + EPS) * gain


def _sigmoid(x):
    return 1.0 / (1.0 + jnp.exp(-x))


def _gelu_tanh(x):
    c = math.sqrt(2.0 / math.pi)
    return x * (0.5 * (1.0 + jnp.tanh(c * (x + 0.044715 * (x * x * x)))))


def _dot(a, b):
    return jnp.dot(a, b, preferred_element_type=F32)


def _dot_nt(a, b):
    return lax.dot_general(a, b, (((1,), (1,)), ((), ())), preferred_element_type=F32)


def _dot_tn(a, b):
    return lax.dot_general(a, b, (((0,), (0,)), ((), ())), preferred_element_type=F32)


def _norm_proj_body(x_ref, g_ref, w_ref, o_ref, xn_ref):
    @pl.when(pl.program_id(1) == 0)
    def _():
        xn_ref[...] = _rms(x_ref[...], g_ref[...]).astype(BF16)

    o_ref[...] = _dot(xn_ref[...], w_ref[...])


def norm_proj(x, gain, w, layer, col0=0, n=None, emit_norm=False, tm=1024, tn=1024):
    m, k = x.shape
    n = w.shape[2] - col0 if n is None else n
    tm, tn = min(tm, m), min(tn, n)
    assert col0 % tn == 0
    out_specs = pl.BlockSpec((tm, tn), lambda i, j: (i, j))
    out_shape = jax.ShapeDtypeStruct((m, n), F32)
    scratch = [pltpu.VMEM((tm, k), BF16)]
    if emit_norm:
        out_specs = [out_specs, pl.BlockSpec((tm, k), lambda i, j: (i, 0))]
        out_shape = [out_shape, jax.ShapeDtypeStruct((m, k), BF16)]
        scratch = []
    return pl.pallas_call(
        _norm_proj_body,
        grid=(m // tm, n // tn),
        in_specs=[pl.BlockSpec((tm, k), lambda i, j: (i, 0)),
                  pl.BlockSpec((1, k), lambda i, j: (0, 0)),
                  pl.BlockSpec((None, k, tn), lambda i, j: (layer, 0, col0 // tn + j))],
        out_specs=out_specs,
        out_shape=out_shape,
        scratch_shapes=scratch,
        compiler_params=_params(("parallel", "arbitrary"), 48),
        name="norm_proj",
    )(x, gain.reshape(1, k), w)


def _rope(x, cos, sin_lo, sin_hi):
    half = ROPE_DIM // 2
    return (x * cos + pltpu.roll(x, half, 1) * sin_lo
            + pltpu.roll(x, x.shape[1] - half, 1) * sin_hi)


def _attn_body(*refs, lam_init, tq, n_cast):
    lam_ref, subln_ref, cos_ref, slo_ref, shi_ref, q_ref, k_ref, v_ref = refs[:8]
    w_refs = refs[8:8 + n_cast]
    o_ref = refs[8 + n_cast]
    wb_refs = refs[9 + n_cast:9 + 2 * n_cast]
    kb_ref, vb_ref = refs[9 + 2 * n_cast:]
    for w_ref, wb_ref in zip(w_refs, wb_refs):
        wb_ref[...] = w_ref[...].astype(BF16)
    s_len = k_ref.shape[1]
    kb_ref[...] = _rope(k_ref[0], cos_ref[...], slo_ref[...], shi_ref[...]).astype(BF16)
    vb_ref[:, 0:A_V_DIM] = v_ref[0].astype(BF16)
    vb_ref[:, A_V_DIM:] = jnp.ones((s_len, A_V_DIM), BF16)
    lv = lam_ref[...]
    lam = (jnp.exp(jnp.sum(lv[0:1] * lv[1:2], axis=-1, keepdims=True))
           - jnp.exp(jnp.sum(lv[2:3] * lv[3:4], axis=-1, keepdims=True)) + lam_init)
    scale = A_QK_DIM ** -0.5 * math.log2(math.e)
    lane = lax.broadcasted_iota(jnp.int32, (tq, A_V_DIM), 1)
    for j in range(s_len // tq):
        rows = slice(j * tq, (j + 1) * tq)
        q = _rope(q_ref[0, rows, :], cos_ref[rows, :], slo_ref[rows, :], shi_ref[rows, :]) * scale
        q2 = jnp.concatenate([jnp.where(lane < A_QK_DIM, q, 0.0),
                              jnp.where(lane >= A_QK_DIM, q, 0.0)], axis=0).astype(BF16)
        s = _dot_nt(q2, kb_ref[...])
        p = jnp.exp2(s - jnp.max(s, axis=-1, keepdims=True))
        ov = _dot(p.astype(BF16), vb_ref[...])
        on = ov[:, 0:A_V_DIM] / ov[:, A_V_DIM:]
        o = on[0:tq] - lam * on[tq:]
        o_ref[0, rows, :] = (_rms(o, subln_ref[...]) * (1.0 - lam_init)).astype(BF16)


def _rope_tables(s):
    half = ROPE_DIM // 2
    pos = jnp.arange(s, dtype=F32)
    inv_freq = ROPE_THETA ** (-jnp.arange(half, dtype=F32) / half)
    ang = pos[:, None] * inv_freq[None, :]
    cos, sin = jnp.cos(ang), jnp.sin(ang)
    pad = A_QK_DIM - ROPE_DIM
    one = jnp.ones((s, pad), F32)
    zero = jnp.zeros((s, pad), F32)
    zh = jnp.zeros((s, half), F32)
    cos_c = jnp.concatenate([cos, cos, one], axis=1)
    lo_c = jnp.concatenate([zh, sin, zero], axis=1)
    hi_c = jnp.concatenate([-sin, zh, zero], axis=1)
    rep = lambda t: jnp.concatenate([t, t], axis=1)
    return rep(cos_c), rep(lo_c), rep(hi_c)


def diff_attention(proj, lam_vec, subln, lam_init, cast_weights=(), tq=128):
    b, s, _ = proj.shape
    tq = min(tq, s)
    cos, lo, hi = _rope_tables(s)
    hd = A_V_DIM
    steps = b * A_HEADS
    table = pl.BlockSpec((s, hd), lambda bi, h: (0, 0))
    head = lambda off: pl.BlockSpec((1, s, hd), lambda bi, h: (bi, 0, off * A_HEADS + h))

    def slab(w):
        layers, rows, cols = w.shape
        per_layer = steps // layers
        assert steps % layers == 0 and rows % (16 * per_layer) == 0
        return pl.BlockSpec((1, rows // per_layer, cols), lambda bi, h: (
            (bi * A_HEADS + h) // per_layer, (bi * A_HEADS + h) % per_layer, 0))

    slabs = [slab(w) for w in cast_weights]
    outs = pl.pallas_call(
        functools.partial(_attn_body, lam_init=lam_init, tq=tq, n_cast=len(cast_weights)),
        grid=(b, A_HEADS),
        in_specs=[pl.BlockSpec((4, A_QK_DIM), lambda bi, h: (0, 0)),
                  pl.BlockSpec((1, hd), lambda bi, h: (0, 0)),
                  table, table, table, head(0), head(1), head(2)] + slabs,
        out_specs=[pl.BlockSpec((1, s, hd), lambda bi, h: (bi, 0, h))] + slabs,
        out_shape=[jax.ShapeDtypeStruct((b, s, A_HEADS * hd), BF16)]
        + [jax.ShapeDtypeStruct(w.shape, BF16) for w in cast_weights],
        scratch_shapes=[pltpu.VMEM((s, hd), BF16), pltpu.VMEM((s, 2 * hd), BF16)],
        compiler_params=_params(("parallel", "parallel"), 56),
        name="diff_attention",
    )(lam_vec, subln.reshape(1, hd), cos, lo, hi, proj, proj, proj, *cast_weights)
    return outs[0], tuple(outs[1:])


def _gmlp_body(u_ref, v_ref, lng_ref, lnb_ref, ws_ref, bs_ref, o_ref, *, n_chunks):
    gd = B_GROUP_DIM
    for c in range(n_chunks):
        rows = slice(c * B_CHUNK, (c + 1) * B_CHUNK)
        vb = _gelu_tanh(v_ref[0, rows, :])
        xc = vb - jnp.mean(vb, axis=-1, keepdims=True)
        var = jnp.mean(xc * xc, axis=-1, keepdims=True)
        y = (xc * lax.rsqrt(var + EPS) * lng_ref[...] + lnb_ref[...]).astype(BF16)
        u = _gelu_tanh(u_ref[0, rows, :])
        for g in range(B_GROUPS):
            cols = slice(g * gd, (g + 1) * gd)
            sv = _dot(ws_ref[g], y[:, cols]) + bs_ref[:, g:g + 1]
            o_ref[0, rows, cols] = (u[:, cols] * sv).astype(BF16)


def gmlp(proj, ln_g, ln_b, w_s, b_s, u_block, n_chunks=4):
    b, s, _ = proj.shape
    width = B_GROUPS * B_GROUP_DIM
    n_chunks = min(n_chunks, s // B_CHUNK)
    t = n_chunks * B_CHUNK
    return pl.pallas_call(
        functools.partial(_gmlp_body, n_chunks=n_chunks),
        grid=(b, s // t),
        in_specs=[pl.BlockSpec((1, t, width), lambda bi, n: (bi, n, u_block)),
                  pl.BlockSpec((1, t, width), lambda bi, n: (bi, n, u_block + 1)),
                  pl.BlockSpec((1, width), lambda bi, n: (0, 0)),
                  pl.BlockSpec((1, width), lambda bi, n: (0, 0)),
                  pl.BlockSpec((B_GROUPS, B_CHUNK, B_CHUNK), lambda bi, n: (0, 0, 0)),
                  pl.BlockSpec((B_CHUNK, B_GROUPS), lambda bi, n: (0, 0))],
        out_specs=pl.BlockSpec((1, t, width), lambda bi, n: (bi, n, 0)),
        out_shape=jax.ShapeDtypeStruct((b, s, width), BF16),
        compiler_params=_params(("parallel", "parallel"), 32),
        name="gmlp",
    )(proj, proj, ln_g.reshape(1, width), ln_b.reshape(1, width), w_s.astype(BF16), b_s.T)


def _out_proj2_body(a_ref, b_ref, w_ref, r_ref, o_ref):
    ka = a_ref.shape[1]
    acc = _dot(a_ref[...], w_ref[0:ka, :]) + _dot(b_ref[...], w_ref[ka:, :])
    o_ref[...] = r_ref[...] + acc


def out_proj2(a, b, w, layer, res, tm=512, tn=2048):
    m, ka = a.shape
    kb = b.shape[1]
    n = w.shape[2]
    tm, tn = min(tm, m), min(tn, n)
    return pl.pallas_call(
        _out_proj2_body,
        grid=(m // tm, n // tn),
        in_specs=[pl.BlockSpec((tm, ka), lambda i, j: (i, 0)),
                  pl.BlockSpec((tm, kb), lambda i, j: (i, 0)),
                  pl.BlockSpec((None, ka + kb, tn), lambda i, j: (layer, 0, j)),
                  pl.BlockSpec((tm, tn), lambda i, j: (i, j))],
        out_specs=pl.BlockSpec((tm, tn), lambda i, j: (i, j)),
        out_shape=jax.ShapeDtypeStruct((m, n), F32),
        compiler_params=_params(("parallel", "parallel"), 48),
        name="out_proj_even",
    )(a, b, w, res)


def _out_proj_gated_body(o_in_ref, gate_ref, gn_ref, w_ref, r_ref, o_ref, y_ref):
    @pl.when(pl.program_id(1) == 0)
    def _():
        gate = gate_ref[...]
        y = _rms(o_in_ref[...], gn_ref[...]) * (gate * _sigmoid(gate))
        y_ref[...] = y.astype(BF16)

    o_ref[...] = r_ref[...] + _dot(y_ref[...], w_ref[...])


def out_proj_gated(o_in, proj, gate_block, gain, w, layer, res, tm=512, tn=2048):
    m, k = o_in.shape
    n = w.shape[2]
    tm, tn = min(tm, m), min(tn, n)
    return pl.pallas_call(
        _out_proj_gated_body,
        grid=(m // tm, n // tn),
        in_specs=[pl.BlockSpec((tm, k), lambda i, j: (i, 0)),
                  pl.BlockSpec((tm, k), lambda i, j: (i, gate_block)),
                  pl.BlockSpec((1, k), lambda i, j: (0, 0)),
                  pl.BlockSpec((None, k, tn), lambda i, j: (layer, 0, j)),
                  pl.BlockSpec((tm, tn), lambda i, j: (i, j))],
        out_specs=pl.BlockSpec((tm, tn), lambda i, j: (i, j)),
        out_shape=jax.ShapeDtypeStruct((m, n), F32),
        scratch_shapes=[pltpu.VMEM((tm, k), BF16)],
        compiler_params=_params(("parallel", "arbitrary"), 56),
        name="out_proj_odd",
    )(o_in, proj, gain.reshape(1, k), w, res)


def _ffn_body(*refs, final_norm):
    if final_norm:
        x_ref, g_ref, wg_ref, wu_ref, wd_ref, fn_ref, o_ref, xn_ref = refs
    else:
        x_ref, g_ref, wg_ref, wu_ref, wd_ref, o_ref, xn_ref = refs
    f = pl.program_id(1)

    @pl.when(f == 0)
    def _():
        x = x_ref[...]
        xn_ref[...] = _rms(x, g_ref[...]).astype(BF16)
        o_ref[...] = x

    xn = xn_ref[...]
    a = _dot(xn, wg_ref[...])
    u = _dot(xn, wu_ref[...])
    hid = (a * _sigmoid(a) * u).astype(BF16)
    o_ref[...] += _dot(hid, wd_ref[...])

    if final_norm:
        @pl.when(f == pl.num_programs(1) - 1)
        def _():
            o_ref[...] = _rms(o_ref[...], fn_ref[...])


def ffn(x, gain, w_gate, w_up, w_down, layer, final_gain=None, tm=1024, tf=512):
    m, k = x.shape
    hdim = w_gate.shape[2]
    tm, tf = min(tm, m), min(tf, hdim)
    vec = pl.BlockSpec((1, k), lambda i, f: (0, 0))
    in_specs = [pl.BlockSpec((tm, k), lambda i, f: (i, 0)), vec,
                pl.BlockSpec((None, k, tf), lambda i, f: (layer, 0, f)),
                pl.BlockSpec((None, k, tf), lambda i, f: (layer, 0, f)),
                pl.BlockSpec((None, tf, k), lambda i, f: (layer, f, 0))]
    args = [x, gain.reshape(1, k), w_gate, w_up, w_down]
    if final_gain is not None:
        in_specs.append(vec)
        args.append(final_gain.reshape(1, k))
    return pl.pallas_call(
        functools.partial(_ffn_body, final_norm=final_gain is not None),
        grid=(m // tm, hdim // tf),
        in_specs=in_specs,
        out_specs=pl.BlockSpec((tm, k), lambda i, f: (i, 0)),
        out_shape=jax.ShapeDtypeStruct((m, k), F32),
        scratch_shapes=[pltpu.VMEM((tm, k), BF16)],
        compiler_params=_params(("parallel", "arbitrary"), 60),
        name="ffn",
    )(*args)


_HALF = 8


def _hgrn_tables(ti, *, proj_ref, lbs, tri_ref, bcum_ref, ckey_ref, qs_ref, val_ref, vb_ref,
                 upd_ref, dec_ref):
    c = HG_CHUNK
    rows = [(ti * HG_PREP + u) * c for u in range(HG_PREP)]
    kdec = [[None, None] for _ in range(HG_PREP)]
    decay = [[None, None] for _ in range(HG_PREP)]
    for d in range(2):
        lb = lbs[d]
        splits, log_k = [], []
        for r in rows:
            f = lb + (1.0 - lb) * _sigmoid(proj_ref[r:r + c, (1 + d) * C_DK:(2 + d) * C_DK])
            logf = jnp.log2(f)
            log_k.append(jnp.log2(1.0 - f))
            hi = logf.astype(BF16)
            rem = logf - hi.astype(F32)
            mid = rem.astype(BF16)
            low = (rem - mid.astype(F32)).astype(BF16)
            splits.append(jnp.concatenate([hi, mid, low], axis=0))
        bc = _dot(tri_ref[d], jnp.concatenate(splits, axis=1))
        for u, r in enumerate(rows):
            bcum = bc[:, u * C_DK:(u + 1) * C_DK]
            ckey = bcum - log_k[u]
            bcum_ref[d, pl.ds(r, c), :] = bcum
            ckey_ref[d, pl.ds(r, c), :] = ckey
            edge = bcum[c - 1:c] if d == 0 else bcum[0:1]
            kdec[u][d] = jnp.exp2(edge - ckey).astype(BF16)
            decay[u][d] = jnp.exp2(edge)
    for u, r in enumerate(rows):
        ci = ti * HG_PREP + u
        val = proj_ref[r:r + c, 3 * C_DK:4 * C_DK]
        val_ref[pl.ds(r, c), :] = val
        vb = val.astype(BF16)
        vb_ref[pl.ds(r, c), :] = vb
        upd_ref[ci] = _dot_tn(vb, jnp.concatenate(kdec[u], axis=1))
        dec_ref[ci] = jnp.concatenate(decay[u], axis=1)
        q = proj_ref[r:r + c, 0:C_DK]
        qs_ref[pl.ds(r, c), :] = q * _sigmoid(q)


def _hgrn_scan(ci, carry, *, upd_ref, dec_ref, sbf_ref, nc):
    st_f, st_b = carry
    cb = nc - 1 - ci
    sbf_ref[0, ci] = st_f.astype(BF16)
    sbf_ref[1, cb] = st_b.astype(BF16)
    st_f = st_f * dec_ref[ci, :, 0:C_DK] + upd_ref[ci, :, 0:C_DK]
    st_b = st_b * dec_ref[cb, :, C_DK:2 * C_DK] + upd_ref[cb, :, C_DK:2 * C_DK]
    return st_f, st_b


def _direct_pairs(reverse):
    pairs = []
    for s in range(HG_SUB):
        for h in range(2):
            sh = s // _HALF
            if not ((h > sh) if reverse else (h < sh)):
                pairs.append((s, h))
    return pairs


def _hgrn_direct_pieces(d, r0, *, bcum_ref, ckey_ref, qs_ref):
    reverse = d == 1
    t_idx = lax.broadcasted_iota(jnp.int32, (_HALF, C_DK), 0)
    b = [bcum_ref[d, pl.ds(r0 + _HALF * h, _HALF), :] for h in range(2)]
    q = [qs_ref[pl.ds(r0 + _HALF * h, _HALF), :] for h in range(2)]
    pieces = []
    for s, h in _direct_pairs(reverse):
        p = q[h] * jnp.exp2(b[h] - ckey_ref[d, pl.ds(r0 + s, 1), :])
        if h == s // _HALF:
            t0 = s - _HALF * h
            p = jnp.where((t_idx <= t0) if reverse else (t_idx >= t0), p, 0.0)
        pieces.append(p)
    rows = len(pieces) // 2 * _HALF
    lhs = jnp.concatenate(pieces, axis=0).astype(BF16)
    return jnp.concatenate([lhs[:rows], lhs[rows:]], axis=1)


def _hgrn_direct_combine(d, r0, rs_ref, row0, *, val_ref):
    pairs = _direct_pairs(d == 1)
    n_half = len(pairs) // 2
    out = [jnp.zeros((_HALF, C_DK), F32), jnp.zeros((_HALF, C_DK), F32)]
    for p_idx, (s, h) in enumerate(pairs):
        col, row = divmod(p_idx, n_half)
        rp = rs_ref[row0 + row * _HALF:row0 + (row + 1) * _HALF, col * C_DK:(col + 1) * C_DK]
        out[h] = out[h] + rp * val_ref[pl.ds(r0 + s, 1), :]
    return jnp.concatenate(out, axis=0)


def _pivot_plan(reverse):
    s, h = HG_SUB, HG_CHUNK // 2
    assert HG_CHUNK == 4 * HG_SUB
    if reverse:
        return [(0, h, h, h, h), (0, s, s, s, s), (h, s, h + s, s, h + s)]
    return [(h, h, 0, h, h - 1), (s, s, 0, s, s - 1), (h + s, s, h, s, h + s - 1)]


def _hgrn_output(ti, carry, *, val_ref, o_ref, bcum_ref, ckey_ref, qs_ref, vb_ref, sbf_ref,
                 w2_ref, pmask_ref, lhs_ref, rs_ref):
    c, sub = HG_CHUNK, HG_SUB
    n_sub = c // sub
    blk_rows = len(_direct_pairs(False)) // 2 * _HALF
    rows = [pl.multiple_of((ti * HG_PAR + u) * c, c) for u in range(HG_PAR)]
    for u, r in enumerate(rows):
        for d in range(2):
            for i in range(n_sub):
                row0 = ((u * 2 + d) * n_sub + i) * blk_rows
                lhs_ref[row0:row0 + blk_rows, :] = _hgrn_direct_pieces(
                    d, r + i * sub, bcum_ref=bcum_ref, ckey_ref=ckey_ref, qs_ref=qs_ref)
    rs_ref[...] = _dot(lhs_ref[...], w2_ref[...])
    pv_groups = []
    for g in range(HG_PAR // HG_PIV):
        q_parts, k_parts, v_parts = [], [], []
        for r in rows[g * HG_PIV:(g + 1) * HG_PIV]:
            for d in range(2):
                for q0, qn, k0, kn, prow in _pivot_plan(d == 1):
                    piv = bcum_ref[d, pl.ds(r + prow, 1), :]
                    q_parts.append(qs_ref[pl.ds(r + q0, qn), :]
                                   * jnp.exp2(bcum_ref[d, pl.ds(r + q0, qn), :] - piv))
                    k_parts.append(jnp.exp2(piv - ckey_ref[d, pl.ds(r + k0, kn), :]))
                    v_parts.append(vb_ref[pl.ds(r + k0, kn), :])
        sc = _dot_nt(jnp.concatenate(q_parts, axis=0).astype(BF16),
                     jnp.concatenate(k_parts, axis=0).astype(BF16)) * pmask_ref[...]
        pv_groups.append(_dot(sc.astype(BF16), jnp.concatenate(v_parts, axis=0)))
    for u, r in enumerate(rows):
        ci = ti * HG_PAR + u
        total = None
        for d in range(2):
            blocks = [_hgrn_direct_combine(d, r + i * sub, rs_ref,
                                           ((u * 2 + d) * n_sub + i) * blk_rows, val_ref=val_ref)
                      for i in range(n_sub)]
            pv = pv_groups[u // HG_PIV]
            off = ((u % HG_PIV) * 2 + d) * c
            for q0, qn, _, _, _ in _pivot_plan(d == 1):
                for j in range(qn // sub):
                    blocks[q0 // sub + j] = (blocks[q0 // sub + j]
                                             + pv[off + j * sub:off + (j + 1) * sub])
                off += qn
            qe = (qs_ref[pl.ds(r, c), :] * jnp.exp2(bcum_ref[d, pl.ds(r, c), :])).astype(BF16)
            o_d = _dot_nt(qe, sbf_ref[d, ci]) + jnp.concatenate(blocks, axis=0)
            total = o_d if total is None else total + o_d
        o_ref[0, pl.ds(r, c), :] = total
    return carry


def _hgrn_body(xn_ref, wq_ref, wf_ref, wb_ref, wi_ref, lbr_ref, o_ref,
               bcum_ref, ckey_ref, qs_ref, val_ref, vb_ref, upd_ref, dec_ref, sbf_ref, tri_ref,
               w2_ref, pmask_ref, lhs_ref, rs_ref, proj_ref, wcat_ref, *, depth, layer_idx):
    s = xn_ref.shape[1]
    c = HG_CHUNK
    nc = s // c
    assert nc % HG_PAR == 0 and nc % HG_PREP == 0 and HG_PAR % HG_PIV == 0

    def lower_bound(d):
        rows = lbr_ref[d * depth:(d + 1) * depth, :]
        e = jnp.exp(rows - jnp.max(rows, axis=0, keepdims=True))
        sm = e / jnp.sum(e, axis=0, keepdims=True)
        acc = sm[0:1] * 0.0
        for l in range(1, layer_idx + 1):
            acc = acc + sm[l:l + 1]
        return acc

    row = lax.broadcasted_iota(jnp.int32, (c, 3 * c), 0)
    col = lax.broadcasted_iota(jnp.int32, (c, 3 * c), 1) % c
    tri_ref[0] = (col <= row).astype(BF16)
    tri_ref[1] = (col >= row).astype(BF16)
    wr = lax.broadcasted_iota(jnp.int32, (2 * C_DK, 2 * C_DK), 0) // C_DK
    wc = lax.broadcasted_iota(jnp.int32, (2 * C_DK, 2 * C_DK), 1) // C_DK
    w2_ref[...] = (wr == wc).astype(BF16)
    n = HG_PIV * 2 * c
    pr = lax.broadcasted_iota(jnp.int32, (n, n), 0)
    pc = lax.broadcasted_iota(jnp.int32, (n, n), 1)
    entry = lambda x: jnp.where(x % c < c // 2, 0, (x % c) // HG_SUB)
    pmask_ref[...] = ((pr // c == pc // c) & (entry(pr) == entry(pc))).astype(F32)

    for g, w_ref in enumerate((wq_ref, wf_ref, wb_ref, wi_ref)):
        wcat_ref[:, g * C_DK:(g + 1) * C_DK] = w_ref[...]
    blk = HG_PREP * c

    def project(t):
        proj_ref[t * blk:(t + 1) * blk, :] = _dot(xn_ref[0, t * blk:(t + 1) * blk, :], wcat_ref[...])

    lbs = (lower_bound(0), lower_bound(1))
    project(0)
    for t in range(nc // HG_PREP):
        if t + 1 < nc // HG_PREP:
            project(t + 1)
        _hgrn_tables(t, proj_ref=proj_ref, lbs=lbs, tri_ref=tri_ref, bcum_ref=bcum_ref,
                     ckey_ref=ckey_ref, qs_ref=qs_ref, val_ref=val_ref, vb_ref=vb_ref,
                     upd_ref=upd_ref, dec_ref=dec_ref)
    zero = jnp.zeros((C_DK, C_DK), F32)
    lax.fori_loop(0, nc, functools.partial(
        _hgrn_scan, upd_ref=upd_ref, dec_ref=dec_ref, sbf_ref=sbf_ref, nc=nc), (zero, zero))
    lax.fori_loop(0, nc // HG_PAR, functools.partial(
        _hgrn_output, val_ref=val_ref, o_ref=o_ref, bcum_ref=bcum_ref, ckey_ref=ckey_ref,
        qs_ref=qs_ref, vb_ref=vb_ref, sbf_ref=sbf_ref, w2_ref=w2_ref, pmask_ref=pmask_ref,
        lhs_ref=lhs_ref, rs_ref=rs_ref), 0)


def hgrn2(xn, w_in, layer, lower_bounds, layer_idx):
    b, s, k = xn.shape
    depth = lower_bounds.shape[1]
    dk = C_DK
    nc = s // HG_CHUNK
    direct_rows = HG_PAR * 2 * (HG_CHUNK // HG_SUB) * (len(_direct_pairs(False)) // 2 * _HALF)
    pivot_rows = HG_PIV * 2 * HG_CHUNK
    wblk = lambda off: pl.BlockSpec((None, k, dk), lambda bi, h: (layer, 0, off * C_HEADS + h))
    return pl.pallas_call(
        functools.partial(_hgrn_body, depth=depth, layer_idx=layer_idx),
        grid=(b, C_HEADS),
        in_specs=[pl.BlockSpec((1, s, k), lambda bi, h: (bi, 0, 0)),
                  wblk(0), wblk(1), wblk(2), wblk(3),
                  pl.BlockSpec((2 * depth, dk), lambda bi, h: (0, h))],
        out_specs=pl.BlockSpec((1, s, dk), lambda bi, h: (bi, 0, h)),
        out_shape=jax.ShapeDtypeStruct((b, s, C_HEADS * dk), F32),
        scratch_shapes=[pltpu.VMEM((2, s, dk), F32),
                        pltpu.VMEM((2, s, dk), F32),
                        pltpu.VMEM((s, dk), F32),
                        pltpu.VMEM((s, dk), F32),
                        pltpu.VMEM((s, dk), BF16),
                        pltpu.VMEM((nc, dk, 2 * dk), F32),
                        pltpu.VMEM((nc, 1, 2 * dk), F32),
                        pltpu.VMEM((2, nc, dk, dk), BF16),
                        pltpu.VMEM((2, HG_CHUNK, 3 * HG_CHUNK), BF16),
                        pltpu.VMEM((2 * dk, 2 * dk), BF16),
                        pltpu.VMEM((pivot_rows, pivot_rows), F32),
                        pltpu.VMEM((direct_rows, 2 * dk), BF16),
                        pltpu.VMEM((direct_rows, 2 * dk), F32),
                        pltpu.VMEM((s, 4 * dk), F32),
                        pltpu.VMEM((k, 4 * dk), BF16)],
        compiler_params=_params(("parallel", "parallel"), 56),
        name="hgrn2",
    )(xn, w_in, w_in, w_in, w_in, lower_bounds.reshape(2 * depth, C_HEADS * dk))


def kernel(x, mix_norm, even_w_in, even_w_out, diff_lq1, diff_lk1, diff_lq2, diff_lk2,
           diff_subln, gmlp_ln_g, gmlp_ln_b, gmlp_w_s, gmlp_b_s, hgrn_w_in, hgrn_w_out,
           hgrn_lower_bounds, hgrn_g_norm, ffn_norm, ffn_w_gate, ffn_w_up, ffn_w_down,
           final_norm):
    b, s, d = x.shape
    depth = mix_norm.shape[0]
    m = b * s
    h = x.reshape(m, d)
    even_w_in, even_w_out = (w.astype(BF16) for w in (even_w_in, even_w_out))
    ffn_ws = None
    for l in range(depth):
        if l % 2 == 0:
            e = l // 2
            proj = norm_proj(h, mix_norm[l], even_w_in, e).reshape(b, s, -1)
            lam_init = 0.8 - 0.6 * math.exp(-0.3 * l)
            lam_vec = jnp.stack([diff_lq1[e], diff_lk1[e], diff_lq2[e], diff_lk2[e]]).astype(F32)
            to_cast = ((ffn_w_gate, ffn_w_up, ffn_w_down, hgrn_w_in, hgrn_w_out)
                       if ffn_ws is None else ())
            oa, cast = diff_attention(proj, lam_vec, diff_subln[e], lam_init, to_cast)
            if to_cast:
                ffn_ws, (hgrn_w_in, hgrn_w_out) = cast[:3], cast[3:]
            u_block = (3 * A_HEADS * A_V_DIM) // (B_GROUPS * B_GROUP_DIM)
            ob = gmlp(proj, gmlp_ln_g[e], gmlp_ln_b[e], gmlp_w_s[e], gmlp_b_s[e], u_block)
            h = out_proj2(oa.reshape(m, -1), ob.reshape(m, -1), even_w_out, e, h)
        else:
            o = l // 2
            width = C_HEADS * C_DK
            gate, xn = norm_proj(h, mix_norm[l], hgrn_w_in, o, col0=4 * width, emit_norm=True)
            scan = hgrn2(xn.reshape(b, s, d), hgrn_w_in, o, hgrn_lower_bounds, l)
            h = out_proj_gated(scan.reshape(m, -1), gate, 0, hgrn_g_norm[o], hgrn_w_out, o, h)
        last = l == depth - 1
        w_gate, w_up, w_down = ffn_ws
        h = ffn(h, ffn_norm[l], w_gate, w_up, w_down, l,
                final_gain=final_norm if last else None)
    return h.reshape(b, s, d)
```

```python
import functools
import math

import jax
import jax.numpy as jnp
from jax import lax
from jax.experimental import pallas as pl
from jax.experimental.pallas import tpu as pltpu

F32 = jnp.float32
BF16 = jnp.bfloat16
EPS = 1e-6
MIB = 1024 * 1024

A_HEADS = 8
A_QK_DIM = 64
A_V_DIM = 128
ROPE_THETA = 500000.0
ROPE_DIM = A_QK_DIM // 4
B_GROUPS = 8
B_GROUP_DIM = 128
B_CHUNK = 128
C_HEADS = 16
C_DK = 128
HG_CHUNK = 64
HG_SUB = 16
HG_PAR = 8
HG_PIV = 2
HG_PREP = 8


def _params(semantics, vmem_mib):
    return pltpu.CompilerParams(dimension_semantics=semantics,
                                vmem_limit_bytes=vmem_mib * MIB)


def _rms(x, gain):
    ms = jnp.mean(x * x, axis=-1, keepdims=True)
    return x * lax.rsqrt(ms + EPS) * gain


def _sigmoid(x):
    return 1.0 / (1.0 + jnp.exp(-x))


def _gelu_tanh(x):
    c = math.sqrt(2.0 / math.pi)
    return x * (0.5 * (1.0 + jnp.tanh(c * (x + 0.044715 * (x * x * x)))))


def _dot(a, b):
    return jnp.dot(a, b, preferred_element_type=F32)


def _dot_nt(a, b):
    return lax.dot_general(a, b, (((1,), (1,)), ((), ())), preferred_element_type=F32)


def _dot_tn(a, b):
    return lax.dot_general(a, b, (((0,), (0,)), ((), ())), preferred_element_type=F32)


def _norm_proj_body(x_ref, g_ref, w_ref, o_ref, xn_ref):
    @pl.when(pl.program_id(1) == 0)
    def _():
        xn_ref[...] = _rms(x_ref[...], g_ref[...]).astype(BF16)

    o_ref[...] = _dot(xn_ref[...], w_ref[...])


def norm_proj(x, gain, w, layer, col0=0, n=None, emit_norm=False, tm=1024, tn=1024):
    m, k = x.shape
    n = w.shape[2] - col0 if n is None else n
    tm, tn = min(tm, m), min(tn, n)
    assert col0 % tn == 0
    out_specs = pl.BlockSpec((tm, tn), lambda i, j: (i, j))
    out_shape = jax.ShapeDtypeStruct((m, n), F32)
    scratch = [pltpu.VMEM((tm, k), BF16)]
    if emit_norm:
        out_specs = [out_specs, pl.BlockSpec((tm, k), lambda i, j: (i, 0))]
        out_shape = [out_shape, jax.ShapeDtypeStruct((m, k), BF16)]
        scratch = []
    return pl.pallas_call(
        _norm_proj_body,
        grid=(m // tm, n // tn),
        in_specs=[pl.BlockSpec((tm, k), lambda i, j: (i, 0)),
                  pl.BlockSpec((1, k), lambda i, j: (0, 0)),
                  pl.BlockSpec((None, k, tn), lambda i, j: (layer, 0, col0 // tn + j))],
        out_specs=out_specs,
        out_shape=out_shape,
        scratch_shapes=scratch,
        compiler_params=_params(("parallel", "arbitrary"), 48),
        name="norm_proj",
    )(x, gain.reshape(1, k), w)


def _rope(x, cos, sin_lo, sin_hi):
    half = ROPE_DIM // 2
    return (x * cos + pltpu.roll(x, half, 1) * sin_lo
            + pltpu.roll(x, x.shape[1] - half, 1) * sin_hi)


def _attn_body(*refs, lam_init, tq, n_cast):
    lam_ref, subln_ref, cos_ref, slo_ref, shi_ref, q_ref, k_ref, v_ref = refs[:8]
    w_refs = refs[8:8 + n_cast]
    o_ref = refs[8 + n_cast]
    wb_refs = refs[9 + n_cast:9 + 2 * n_cast]
    kb_ref, vb_ref = refs[9 + 2 * n_cast:]
    for w_ref, wb_ref in zip(w_refs, wb_refs):
        wb_ref[...] = w_ref[...].astype(BF16)
    s_len = k_ref.shape[1]
    kb_ref[...] = _rope(k_ref[0], cos_ref[...], slo_ref[...], shi_ref[...]).astype(BF16)
    vb_ref[:, 0:A_V_DIM] = v_ref[0].astype(BF16)
    vb_ref[:, A_V_DIM:] = jnp.ones((s_len, A_V_DIM), BF16)
    lv = lam_ref[...]
    lam = (jnp.exp(jnp.sum(lv[0:1] * lv[1:2], axis=-1, keepdims=True))
           - jnp.exp(jnp.sum(lv[2:3] * lv[3:4], axis=-1, keepdims=True)) + lam_init)
    scale = A_QK_DIM ** -0.5 * math.log2(math.e)
    lane = lax.broadcasted_iota(jnp.int32, (tq, A_V_DIM), 1)
    for j in range(s_len // tq):
        rows = slice(j * tq, (j + 1) * tq)
        q = _rope(q_ref[0, rows, :], cos_ref[rows, :], slo_ref[rows, :], shi_ref[rows, :]) * scale
        q2 = jnp.concatenate([jnp.where(lane < A_QK_DIM, q, 0.0),
                              jnp.where(lane >= A_QK_DIM, q, 0.0)], axis=0).astype(BF16)
        s = _dot_nt(q2, kb_ref[...])
        p = jnp.exp2(s - jnp.max(s, axis=-1, keepdims=True))
        ov = _dot(p.astype(BF16), vb_ref[...])
        on = ov[:, 0:A_V_DIM] / ov[:, A_V_DIM:]
        o = on[0:tq] - lam * on[tq:]
        o_ref[0, rows, :] = (_rms(o, subln_ref[...]) * (1.0 - lam_init)).astype(BF16)


def _rope_tables(s):
    half = ROPE_DIM // 2
    pos = jnp.arange(s, dtype=F32)
    inv_freq = ROPE_THETA ** (-jnp.arange(half, dtype=F32) / half)
    ang = pos[:, None] * inv_freq[None, :]
    cos, sin = jnp.cos(ang), jnp.sin(ang)
    pad = A_QK_DIM - ROPE_DIM
    one = jnp.ones((s, pad), F32)
    zero = jnp.zeros((s, pad), F32)
    zh = jnp.zeros((s, half), F32)
    cos_c = jnp.concatenate([cos, cos, one], axis=1)
    lo_c = jnp.concatenate([zh, sin, zero], axis=1)
    hi_c = jnp.concatenate([-sin, zh, zero], axis=1)
    rep = lambda t: jnp.concatenate([t, t], axis=1)
    return rep(cos_c), rep(lo_c), rep(hi_c)


def diff_attention(proj, lam_vec, subln, lam_init, cast_weights=(), tq=128):
    b, s, _ = proj.shape
    tq = min(tq, s)
    cos, lo, hi = _rope_tables(s)
    hd = A_V_DIM
    steps = b * A_HEADS
    table = pl.BlockSpec((s, hd), lambda bi, h: (0, 0))
    head = lambda off: pl.BlockSpec((1, s, hd), lambda bi, h: (bi, 0, off * A_HEADS + h))

    def slab(w):
        layers, rows, cols = w.shape
        per_layer = steps // layers
        assert steps % layers == 0 and rows % (16 * per_layer) == 0
        return pl.BlockSpec((1, rows // per_layer, cols), lambda bi, h: (
            (bi * A_HEADS + h) // per_layer, (bi * A_HEADS + h) % per_layer, 0))

    slabs = [slab(w) for w in cast_weights]
    outs = pl.pallas_call(
        functools.partial(_attn_body, lam_init=lam_init, tq=tq, n_cast=len(cast_weights)),
        grid=(b, A_HEADS),
        in_specs=[pl.BlockSpec((4, A_QK_DIM), lambda bi, h: (0, 0)),
                  pl.BlockSpec((1, hd), lambda bi, h: (0, 0)),
                  table, table, table, head(0), head(1), head(2)] + slabs,
        out_specs=[pl.BlockSpec((1, s, hd), lambda bi, h: (bi, 0, h))] + slabs,
        out_shape=[jax.ShapeDtypeStruct((b, s, A_HEADS * hd), BF16)]
        + [jax.ShapeDtypeStruct(w.shape, BF16) for w in cast_weights],
        scratch_shapes=[pltpu.VMEM((s, hd), BF16), pltpu.VMEM((s, 2 * hd), BF16)],
        compiler_params=_params(("parallel", "parallel"), 56),
        name="diff_attention",
    )(lam_vec, subln.reshape(1, hd), cos, lo, hi, proj, proj, proj, *cast_weights)
    return outs[0], tuple(outs[1:])


def _gmlp_body(u_ref, v_ref, lng_ref, lnb_ref, ws_ref, bs_ref, o_ref, *, n_chunks):
    gd = B_GROUP_DIM
    for c in range(n_chunks):
        rows = slice(c * B_CHUNK, (c + 1) * B_CHUNK)
        vb = _gelu_tanh(v_ref[0, rows, :])
        xc = vb - jnp.mean(vb, axis=-1, keepdims=True)
        var = jnp.mean(xc * xc, axis=-1, keepdims=True)
        y = (xc * lax.rsqrt(var + EPS) * lng_ref[...] + lnb_ref[...]).astype(BF16)
        u = _gelu_tanh(u_ref[0, rows, :])
        for g in range(B_GROUPS):
            cols = slice(g * gd, (g + 1) * gd)
            sv = _dot(ws_ref[g], y[:, cols]) + bs_ref[:, g:g + 1]
            o_ref[0, rows, cols] = (u[:, cols] * sv).astype(BF16)


def gmlp(proj, ln_g, ln_b, w_s, b_s, u_block, n_chunks=4):
    b, s, _ = proj.shape
    width = B_GROUPS * B_GROUP_DIM
    n_chunks = min(n_chunks, s // B_CHUNK)
    t = n_chunks * B_CHUNK
    return pl.pallas_call(
        functools.partial(_gmlp_body, n_chunks=n_chunks),
        grid=(b, s // t),
        in_specs=[pl.BlockSpec((1, t, width), lambda bi, n: (bi, n, u_block)),
                  pl.BlockSpec((1, t, width), lambda bi, n: (bi, n, u_block + 1)),
                  pl.BlockSpec((1, width), lambda bi, n: (0, 0)),
                  pl.BlockSpec((1, width), lambda bi, n: (0, 0)),
                  pl.BlockSpec((B_GROUPS, B_CHUNK, B_CHUNK), lambda bi, n: (0, 0, 0)),
                  pl.BlockSpec((B_CHUNK, B_GROUPS), lambda bi, n: (0, 0))],
        out_specs=pl.BlockSpec((1, t, width), lambda bi, n: (bi, n, 0)),
        out_shape=jax.ShapeDtypeStruct((b, s, width), BF16),
        compiler_params=_params(("parallel", "parallel"), 32),
        name="gmlp",
    )(proj, proj, ln_g.reshape(1, width), ln_b.reshape(1, width), w_s.astype(BF16), b_s.T)


def _out_proj2_body(a_ref, b_ref, w_ref, r_ref, o_ref):
    ka = a_ref.shape[1]
    acc = _dot(a_ref[...], w_ref[0:ka, :]) + _dot(b_ref[...], w_ref[ka:, :])
    o_ref[...] = r_ref[...] + acc


def out_proj2(a, b, w, layer, res, tm=512, tn=2048):
    m, ka = a.shape
    kb = b.shape[1]
    n = w.shape[2]
    tm, tn = min(tm, m), min(tn, n)
    return pl.pallas_call(
        _out_proj2_body,
        grid=(m // tm, n // tn),
        in_specs=[pl.BlockSpec((tm, ka), lambda i, j: (i, 0)),
                  pl.BlockSpec((tm, kb), lambda i, j: (i, 0)),
                  pl.BlockSpec((None, ka + kb, tn), lambda i, j: (layer, 0, j)),
                  pl.BlockSpec((tm, tn), lambda i, j: (i, j))],
        out_specs=pl.BlockSpec((tm, tn), lambda i, j: (i, j)),
        out_shape=jax.ShapeDtypeStruct((m, n), F32),
        compiler_params=_params(("parallel", "parallel"), 48),
        name="out_proj_even",
    )(a, b, w, res)


def _out_proj_gated_body(o_in_ref, gate_ref, gn_ref, w_ref, r_ref, o_ref, y_ref):
    @pl.when(pl.program_id(1) == 0)
    def _():
        gate = gate_ref[...]
        y = _rms(o_in_ref[...], gn_ref[...]) * (gate * _sigmoid(gate))
        y_ref[...] = y.astype(BF16)

    o_ref[...] = r_ref[...] + _dot(y_ref[...], w_ref[...])


def out_proj_gated(o_in, proj, gate_block, gain, w, layer, res, tm=512, tn=2048):
    m, k = o_in.shape
    n = w.shape[2]
    tm, tn = min(tm, m), min(tn, n)
    return pl.pallas_call(
        _out_proj_gated_body,
        grid=(m // tm, n // tn),
        in_specs=[pl.BlockSpec((tm, k), lambda i, j: (i, 0)),
                  pl.BlockSpec((tm, k), lambda i, j: (i, gate_block)),
                  pl.BlockSpec((1, k), lambda i, j: (0, 0)),
                  pl.BlockSpec((None, k, tn), lambda i, j: (layer, 0, j)),
                  pl.BlockSpec((tm, tn), lambda i, j: (i, j))],
        out_specs=pl.BlockSpec((tm, tn), lambda i, j: (i, j)),
        out_shape=jax.ShapeDtypeStruct((m, n), F32),
        scratch_shapes=[pltpu.VMEM((tm, k), BF16)],
        compiler_params=_params(("parallel", "arbitrary"), 56),
        name="out_proj_odd",
    )(o_in, proj, gain.reshape(1, k), w, res)


def _ffn_body(*refs, final_norm):
    if final_norm:
        x_ref, g_ref, wg_ref, wu_ref, wd_ref, fn_ref, o_ref, xn_ref = refs
    else:
        x_ref, g_ref, wg_ref, wu_ref, wd_ref, o_ref, xn_ref = refs
    f = pl.program_id(1)

    @pl.when(f == 0)
    def _():
        x = x_ref[...]
        xn_ref[...] = _rms(x, g_ref[...]).astype(BF16)
        o_ref[...] = x

    xn = xn_ref[...]
    a = _dot(xn, wg_ref[...])
    u = _dot(xn, wu_ref[...])
    hid = (a * _sigmoid(a) * u).astype(BF16)
    o_ref[...] += _dot(hid, wd_ref[...])

    if final_norm:
        @pl.when(f == pl.num_programs(1) - 1)
        def _():
            o_ref[...] = _rms(o_ref[...], fn_ref[...])


def ffn(x, gain, w_gate, w_up, w_down, layer, final_gain=None, tm=1024, tf=512):
    m, k = x.shape
    hdim = w_gate.shape[2]
    tm, tf = min(tm, m), min(tf, hdim)
    vec = pl.BlockSpec((1, k), lambda i, f: (0, 0))
    in_specs = [pl.BlockSpec((tm, k), lambda i, f: (i, 0)), vec,
                pl.BlockSpec((None, k, tf), lambda i, f: (layer, 0, f)),
                pl.BlockSpec((None, k, tf), lambda i, f: (layer, 0, f)),
                pl.BlockSpec((None, tf, k), lambda i, f: (layer, f, 0))]
    args = [x, gain.reshape(1, k), w_gate, w_up, w_down]
    if final_gain is not None:
        in_specs.append(vec)
        args.append(final_gain.reshape(1, k))
    return pl.pallas_call(
        functools.partial(_ffn_body, final_norm=final_gain is not None),
        grid=(m // tm, hdim // tf),
        in_specs=in_specs,
        out_specs=pl.BlockSpec((tm, k), lambda i, f: (i, 0)),
        out_shape=jax.ShapeDtypeStruct((m, k), F32),
        scratch_shapes=[pltpu.VMEM((tm, k), BF16)],
        compiler_params=_params(("parallel", "arbitrary"), 60),
        name="ffn",
    )(*args)


_HALF = 8


def _hgrn_tables(ti, *, proj_ref, lbs, tri_ref, bcum_ref, ckey_ref, qs_ref, val_ref, vb_ref,
                 upd_ref, dec_ref):
    c = HG_CHUNK
    rows = [(ti * HG_PREP + u) * c for u in range(HG_PREP)]
    kdec = [[None, None] for _ in range(HG_PREP)]
    decay = [[None, None] for _ in range(HG_PREP)]
    for d in range(2):
        lb = lbs[d]
        splits, log_k = [], []
        for r in rows:
            f = lb + (1.0 - lb) * _sigmoid(proj_ref[r:r + c, (1 + d) * C_DK:(2 + d) * C_DK])
            logf = jnp.log2(f)
            log_k.append(jnp.log2(1.0 - f))
            hi = logf.astype(BF16)
            rem = logf - hi.astype(F32)
            mid = rem.astype(BF16)
            low = (rem - mid.astype(F32)).astype(BF16)
            splits.append(jnp.concatenate([hi, mid, low], axis=0))
        bc = _dot(tri_ref[d], jnp.concatenate(splits, axis=1))
        for u, r in enumerate(rows):
            bcum = bc[:, u * C_DK:(u + 1) * C_DK]
            ckey = bcum - log_k[u]
            bcum_ref[d, pl.ds(r, c), :] = bcum
            ckey_ref[d, pl.ds(r, c), :] = ckey
            edge = bcum[c - 1:c] if d == 0 else bcum[0:1]
            kdec[u][d] = jnp.exp2(edge - ckey).astype(BF16)
            decay[u][d] = jnp.exp2(edge)
    for u, r in enumerate(rows):
        ci = ti * HG_PREP + u
        val = proj_ref[r:r + c, 3 * C_DK:4 * C_DK]
        val_ref[pl.ds(r, c), :] = val
        vb = val.astype(BF16)
        vb_ref[pl.ds(r, c), :] = vb
        upd_ref[ci] = _dot_tn(vb, jnp.concatenate(kdec[u], axis=1))
        dec_ref[ci] = jnp.concatenate(decay[u], axis=1)
        q = proj_ref[r:r + c, 0:C_DK]
        qs_ref[pl.ds(r, c), :] = q * _sigmoid(q)


def _hgrn_scan(ci, carry, *, upd_ref, dec_ref, sbf_ref, nc):
    st_f, st_b = carry
    cb = nc - 1 - ci
    sbf_ref[0, ci] = st_f.astype(BF16)
    sbf_ref[1, cb] = st_b.astype(BF16)
    st_f = st_f * dec_ref[ci, :, 0:C_DK] + upd_ref[ci, :, 0:C_DK]
    st_b = st_b * dec_ref[cb, :, C_DK:2 * C_DK] + upd_ref[cb, :, C_DK:2 * C_DK]
    return st_f, st_b


def _direct_pairs(reverse):
    pairs = []
    for s in range(HG_SUB):
        for h in range(2):
            sh = s // _HALF
            if not ((h > sh) if reverse else (h < sh)):
                pairs.append((s, h))
    return pairs


def _hgrn_direct_pieces(d, r0, *, bcum_ref, ckey_ref, qs_ref):
    reverse = d == 1
    t_idx = lax.broadcasted_iota(jnp.int32, (_HALF, C_DK), 0)
    b = [bcum_ref[d, pl.ds(r0 + _HALF * h, _HALF), :] for h in range(2)]
    q = [qs_ref[pl.ds(r0 + _HALF * h, _HALF), :] for h in range(2)]
    pieces = []
    for s, h in _direct_pairs(reverse):
        p = q[h] * jnp.exp2(b[h] - ckey_ref[d, pl.ds(r0 + s, 1), :])
        if h == s // _HALF:
            t0 = s - _HALF * h
            p = jnp.where((t_idx <= t0) if reverse else (t_idx >= t0), p, 0.0)
        pieces.append(p)
    rows = len(pieces) // 2 * _HALF
    lhs = jnp.concatenate(pieces, axis=0).astype(BF16)
    return jnp.concatenate([lhs[:rows], lhs[rows:]], axis=1)


def _hgrn_direct_combine(d, r0, rs_ref, row0, *, val_ref):
    pairs = _direct_pairs(d == 1)
    n_half = len(pairs) // 2
    out = [jnp.zeros((_HALF, C_DK), F32), jnp.zeros((_HALF, C_DK), F32)]
    for p_idx, (s, h) in enumerate(pairs):
        col, row = divmod(p_idx, n_half)
        rp = rs_ref[row0 + row * _HALF:row0 + (row + 1) * _HALF, col * C_DK:(col + 1) * C_DK]
        out[h] = out[h] + rp * val_ref[pl.ds(r0 + s, 1), :]
    return jnp.concatenate(out, axis=0)


def _pivot_plan(reverse):
    s, h = HG_SUB, HG_CHUNK // 2
    assert HG_CHUNK == 4 * HG_SUB
    if reverse:
        return [(0, h, h, h, h), (0, s, s, s, s), (h, s, h + s, s, h + s)]
    return [(h, h, 0, h, h - 1), (s, s, 0, s, s - 1), (h + s, s, h, s, h + s - 1)]


def _hgrn_output(ti, carry, *, val_ref, o_ref, bcum_ref, ckey_ref, qs_ref, vb_ref, sbf_ref,
                 w2_ref, pmask_ref, lhs_ref, rs_ref):
    c, sub = HG_CHUNK, HG_SUB
    n_sub = c // sub
    blk_rows = len(_direct_pairs(False)) // 2 * _HALF
    rows = [pl.multiple_of((ti * HG_PAR + u) * c, c) for u in range(HG_PAR)]
    for u, r in enumerate(rows):
        for d in range(2):
            for i in range(n_sub):
                row0 = ((u * 2 + d) * n_sub + i) * blk_rows
                lhs_ref[row0:row0 + blk_rows, :] = _hgrn_direct_pieces(
                    d, r + i * sub, bcum_ref=bcum_ref, ckey_ref=ckey_ref, qs_ref=qs_ref)
    rs_ref[...] = _dot(lhs_ref[...], w2_ref[...])
    pv_groups = []
    for g in range(HG_PAR // HG_PIV):
        q_parts, k_parts, v_parts = [], [], []
        for r in rows[g * HG_PIV:(g + 1) * HG_PIV]:
            for d in range(2):
                for q0, qn, k0, kn, prow in _pivot_plan(d == 1):
                    piv = bcum_ref[d, pl.ds(r + prow, 1), :]
                    q_parts.append(qs_ref[pl.ds(r + q0, qn), :]
                                   * jnp.exp2(bcum_ref[d, pl.ds(r + q0, qn), :] - piv))
                    k_parts.append(jnp.exp2(piv - ckey_ref[d, pl.ds(r + k0, kn), :]))
                    v_parts.append(vb_ref[pl.ds(r + k0, kn), :])
        sc = _dot_nt(jnp.concatenate(q_parts, axis=0).astype(BF16),
                     jnp.concatenate(k_parts, axis=0).astype(BF16)) * pmask_ref[...]
        pv_groups.append(_dot(sc.astype(BF16), jnp.concatenate(v_parts, axis=0)))
    for u, r in enumerate(rows):
        ci = ti * HG_PAR + u
        total = None
        for d in range(2):
            blocks = [_hgrn_direct_combine(d, r + i * sub, rs_ref,
                                           ((u * 2 + d) * n_sub + i) * blk_rows, val_ref=val_ref)
                      for i in range(n_sub)]
            pv = pv_groups[u // HG_PIV]
            off = ((u % HG_PIV) * 2 + d) * c
            for q0, qn, _, _, _ in _pivot_plan(d == 1):
                for j in range(qn // sub):
                    blocks[q0 // sub + j] = (blocks[q0 // sub + j]
                                             + pv[off + j * sub:off + (j + 1) * sub])
                off += qn
            qe = (qs_ref[pl.ds(r, c), :] * jnp.exp2(bcum_ref[d, pl.ds(r, c), :])).astype(BF16)
            o_d = _dot_nt(qe, sbf_ref[d, ci]) + jnp.concatenate(blocks, axis=0)
            total = o_d if total is None else total + o_d
        o_ref[0, pl.ds(r, c), :] = total
    return carry


def _hgrn_body(xn_ref, wq_ref, wf_ref, wb_ref, wi_ref, lbr_ref, o_ref,
               bcum_ref, ckey_ref, qs_ref, val_ref, vb_ref, upd_ref, dec_ref, sbf_ref, tri_ref,
               w2_ref, pmask_ref, lhs_ref, rs_ref, proj_ref, wcat_ref, *, depth, layer_idx):
    s = xn_ref.shape[1]
    c = HG_CHUNK
    nc = s // c
    assert nc % HG_PAR == 0 and nc % HG_PREP == 0 and HG_PAR % HG_PIV == 0

    def lower_bound(d):
        rows = lbr_ref[d * depth:(d + 1) * depth, :]
        e = jnp.exp(rows - jnp.max(rows, axis=0, keepdims=True))
        sm = e / jnp.sum(e, axis=0, keepdims=True)
        acc = sm[0:1] * 0.0
        for l in range(1, layer_idx + 1):
            acc = acc + sm[l:l + 1]
        return acc

    row = lax.broadcasted_iota(jnp.int32, (c, 3 * c), 0)
    col = lax.broadcasted_iota(jnp.int32, (c, 3 * c), 1) % c
    tri_ref[0] = (col <= row).astype(BF16)
    tri_ref[1] = (col >= row).astype(BF16)
    wr = lax.broadcasted_iota(jnp.int32, (2 * C_DK, 2 * C_DK), 0) // C_DK
    wc = lax.broadcasted_iota(jnp.int32, (2 * C_DK, 2 * C_DK), 1) // C_DK
    w2_ref[...] = (wr == wc).astype(BF16)
    n = HG_PIV * 2 * c
    pr = lax.broadcasted_iota(jnp.int32, (n, n), 0)
    pc = lax.broadcasted_iota(jnp.int32, (n, n), 1)
    entry = lambda x: jnp.where(x % c < c // 2, 0, (x % c) // HG_SUB)
    pmask_ref[...] = ((pr // c == pc // c) & (entry(pr) == entry(pc))).astype(F32)

    for g, w_ref in enumerate((wq_ref, wf_ref, wb_ref, wi_ref)):
        wcat_ref[:, g * C_DK:(g + 1) * C_DK] = w_ref[...]
    blk = HG_PREP * c

    def project(t):
        proj_ref[t * blk:(t + 1) * blk, :] = _dot(xn_ref[0, t * blk:(t + 1) * blk, :], wcat_ref[...])

    lbs = (lower_bound(0), lower_bound(1))
    project(0)
    for t in range(nc // HG_PREP):
        if t + 1 < nc // HG_PREP:
            project(t + 1)
        _hgrn_tables(t, proj_ref=proj_ref, lbs=lbs, tri_ref=tri_ref, bcum_ref=bcum_ref,
                     ckey_ref=ckey_ref, qs_ref=qs_ref, val_ref=val_ref, vb_ref=vb_ref,
                     upd_ref=upd_ref, dec_ref=dec_ref)
    zero = jnp.zeros((C_DK, C_DK), F32)
    lax.fori_loop(0, nc, functools.partial(
        _hgrn_scan, upd_ref=upd_ref, dec_ref=dec_ref, sbf_ref=sbf_ref, nc=nc), (zero, zero))
    lax.fori_loop(0, nc // HG_PAR, functools.partial(
        _hgrn_output, val_ref=val_ref, o_ref=o_ref, bcum_ref=bcum_ref, ckey_ref=ckey_ref,
        qs_ref=qs_ref, vb_ref=vb_ref, sbf_ref=sbf_ref, w2_ref=w2_ref, pmask_ref=pmask_ref,
        lhs_ref=lhs_ref, rs_ref=rs_ref), 0)


def hgrn2(xn, w_in, layer, lower_bounds, layer_idx):
    b, s, k = xn.shape
    depth = lower_bounds.shape[1]
    dk = C_DK
    nc = s // HG_CHUNK
    direct_rows = HG_PAR * 2 * (HG_CHUNK // HG_SUB) * (len(_direct_pairs(False)) // 2 * _HALF)
    pivot_rows = HG_PIV * 2 * HG_CHUNK
    wblk = lambda off: pl.BlockSpec((None, k, dk), lambda bi, h: (layer, 0, off * C_HEADS + h))
    return pl.pallas_call(
        functools.partial(_hgrn_body, depth=depth, layer_idx=layer_idx),
        grid=(b, C_HEADS),
        in_specs=[pl.BlockSpec((1, s, k), lambda bi, h: (bi, 0, 0)),
                  wblk(0), wblk(1), wblk(2), wblk(3),
                  pl.BlockSpec((2 * depth, dk), lambda bi, h: (0, h))],
        out_specs=pl.BlockSpec((1, s, dk), lambda bi, h: (bi, 0, h)),
        out_shape=jax.ShapeDtypeStruct((b, s, C_HEADS * dk), F32),
        scratch_shapes=[pltpu.VMEM((2, s, dk), F32),
                        pltpu.VMEM((2, s, dk), F32),
                        pltpu.VMEM((s, dk), F32),
                        pltpu.VMEM((s, dk), F32),
                        pltpu.VMEM((s, dk), BF16),
                        pltpu.VMEM((nc, dk, 2 * dk), F32),
                        pltpu.VMEM((nc, 1, 2 * dk), F32),
                        pltpu.VMEM((2, nc, dk, dk), BF16),
                        pltpu.VMEM((2, HG_CHUNK, 3 * HG_CHUNK), BF16),
                        pltpu.VMEM((2 * dk, 2 * dk), BF16),
                        pltpu.VMEM((pivot_rows, pivot_rows), F32),
                        pltpu.VMEM((direct_rows, 2 * dk), BF16),
                        pltpu.VMEM((direct_rows, 2 * dk), F32),
                        pltpu.VMEM((s, 4 * dk), F32),
                        pltpu.VMEM((k, 4 * dk), BF16)],
        compiler_params=_params(("parallel", "parallel"), 56),
        name="hgrn2",
    )(xn, w_in, w_in, w_in, w_in, lower_bounds.reshape(2 * depth, C_HEADS * dk))


def kernel(x, mix_norm, even_w_in, even_w_out, diff_lq1, diff_lk1, diff_lq2, diff_lk2,
           diff_subln, gmlp_ln_g, gmlp_ln_b, gmlp_w_s, gmlp_b_s, hgrn_w_in, hgrn_w_out,
           hgrn_lower_bounds, hgrn_g_norm, ffn_norm, ffn_w_gate, ffn_w_up, ffn_w_down,
           final_norm):
    b, s, d = x.shape
    depth = mix_norm.shape[0]
    m = b * s
    h = x.reshape(m, d)
    even_w_in = even_w_in.astype(BF16)
    ffn_ws = None
    for l in range(depth):
        if l % 2 == 0:
            e = l // 2
            proj = norm_proj(h, mix_norm[l], even_w_in, e).reshape(b, s, -1)
            lam_init = 0.8 - 0.6 * math.exp(-0.3 * l)
            lam_vec = jnp.stack([diff_lq1[e], diff_lk1[e], diff_lq2[e], diff_lk2[e]]).astype(F32)
            to_cast = ((ffn_w_gate, ffn_w_up, ffn_w_down, hgrn_w_in, hgrn_w_out, even_w_out)
                       if ffn_ws is None else ())
            oa, cast = diff_attention(proj, lam_vec, diff_subln[e], lam_init, to_cast)
            if to_cast:
                ffn_ws, (hgrn_w_in, hgrn_w_out, even_w_out) = cast[:3], cast[3:]
            u_block = (3 * A_HEADS * A_V_DIM) // (B_GROUPS * B_GROUP_DIM)
            ob = gmlp(proj, gmlp_ln_g[e], gmlp_ln_b[e], gmlp_w_s[e], gmlp_b_s[e], u_block)
            h = out_proj2(oa.reshape(m, -1), ob.reshape(m, -1), even_w_out, e, h)
        else:
            o = l // 2
            width = C_HEADS * C_DK
            gate, xn = norm_proj(h, mix_norm[l], hgrn_w_in, o, col0=4 * width, emit_norm=True)
            scan = hgrn2(xn.reshape(b, s, d), hgrn_w_in, o, hgrn_lower_bounds, l)
            h = out_proj_gated(scan.reshape(m, -1), gate, 0, hgrn_g_norm[o], hgrn_w_out, o, h)
        last = l == depth - 1
        w_gate, w_up, w_down = ffn_ws
        h = ffn(h, ffn_norm[l], w_gate, w_up, w_down, l,
                final_gain=final_norm if last else None)
    return h.reshape(b, s, d)
```

```python
import functools
import math

import jax
import jax.numpy as jnp
import numpy as np
from jax import lax
from jax.experimental import pallas as pl
from jax.experimental.pallas import tpu as pltpu

F32 = jnp.float32
BF16 = jnp.bfloat16
EPS = 1e-6
MIB = 1024 * 1024

A_HEADS = 8
A_QK_DIM = 64
A_V_DIM = 128
ROPE_THETA = 500000.0
ROPE_DIM = A_QK_DIM // 4
B_GROUPS = 8
B_GROUP_DIM = 128
B_CHUNK = 128
C_HEADS = 16
C_DK = 128
HG_CHUNK = 64
HG_SUB = 16
HG_PAR = 8
HG_PIV = 2
HG_PREP = 8


def _params(semantics, vmem_mib):
    return pltpu.CompilerParams(dimension_semantics=semantics,
                                vmem_limit_bytes=vmem_mib * MIB)


def _rms(x, gain):
    ms = jnp.mean(x * x, axis=-1, keepdims=True)
    return x * lax.rsqrt(ms + EPS) * gain


def _sigmoid(x):
    return 1.0 / (1.0 + jnp.exp(-x))


def _gelu_tanh(x):
    c = math.sqrt(2.0 / math.pi)
    return x * (0.5 * (1.0 + jnp.tanh(c * (x + 0.044715 * (x * x * x)))))


def _dot(a, b):
    return jnp.dot(a, b, preferred_element_type=F32)


def _dot_nt(a, b):
    return lax.dot_general(a, b, (((1,), (1,)), ((), ())), preferred_element_type=F32)


def _dot_tn(a, b):
    return lax.dot_general(a, b, (((0,), (0,)), ((), ())), preferred_element_type=F32)


def _norm_proj_body(x_ref, g_ref, w_ref, o_ref, xn_ref):
    @pl.when(pl.program_id(1) == 0)
    def _():
        xn_ref[...] = _rms(x_ref[...], g_ref[...]).astype(BF16)

    o_ref[...] = _dot(xn_ref[...], w_ref[...])


def norm_proj(x, gain, w, layer, col0=0, n=None, emit_norm=False, tm=1024, tn=1024):
    m, k = x.shape
    n = w.shape[2] - col0 if n is None else n
    tm, tn = min(tm, m), min(tn, n)
    assert col0 % tn == 0
    out_specs = pl.BlockSpec((tm, tn), lambda i, j: (i, j))
    out_shape = jax.ShapeDtypeStruct((m, n), F32)
    scratch = [pltpu.VMEM((tm, k), BF16)]
    if emit_norm:
        out_specs = [out_specs, pl.BlockSpec((tm, k), lambda i, j: (i, 0))]
        out_shape = [out_shape, jax.ShapeDtypeStruct((m, k), BF16)]
        scratch = []
    return pl.pallas_call(
        _norm_proj_body,
        grid=(m // tm, n // tn),
        in_specs=[pl.BlockSpec((tm, k), lambda i, j: (i, 0)),
                  pl.BlockSpec((1, k), lambda i, j: (0, 0)),
                  pl.BlockSpec((None, k, tn), lambda i, j: (layer, 0, col0 // tn + j))],
        out_specs=out_specs,
        out_shape=out_shape,
        scratch_shapes=scratch,
        compiler_params=_params(("parallel", "arbitrary"), 48),
        name="norm_proj",
    )(x, gain.reshape(1, k), w)


def _rope(x, cos, sin_lo, sin_hi):
    half = ROPE_DIM // 2
    return (x * cos + pltpu.roll(x, half, 1) * sin_lo
            + pltpu.roll(x, x.shape[1] - half, 1) * sin_hi)


def _attn_body(*refs, lam_init, tq, n_cast):
    lam_ref, subln_ref, cos_ref, slo_ref, shi_ref, q_ref, k_ref, v_ref = refs[:8]
    w_refs = refs[8:8 + n_cast]
    o_ref = refs[8 + n_cast]
    wb_refs = refs[9 + n_cast:9 + 2 * n_cast]
    kb_ref, vb_ref = refs[9 + 2 * n_cast:]
    for w_ref, wb_ref in zip(w_refs, wb_refs):
        wb_ref[...] = w_ref[...].astype(BF16)
    s_len = k_ref.shape[1]
    kb_ref[...] = _rope(k_ref[0], cos_ref[...], slo_ref[...], shi_ref[...]).astype(BF16)
    vb_ref[:, 0:A_V_DIM] = v_ref[0].astype(BF16)
    vb_ref[:, A_V_DIM:] = jnp.ones((s_len, A_V_DIM), BF16)
    lv = lam_ref[...]
    lam = (jnp.exp(jnp.sum(lv[0:1] * lv[1:2], axis=-1, keepdims=True))
           - jnp.exp(jnp.sum(lv[2:3] * lv[3:4], axis=-1, keepdims=True)) + lam_init)
    scale = A_QK_DIM ** -0.5 * math.log2(math.e)
    lane = lax.broadcasted_iota(jnp.int32, (tq, A_V_DIM), 1)
    for j in range(s_len // tq):
        rows = slice(j * tq, (j + 1) * tq)
        q = _rope(q_ref[0, rows, :], cos_ref[rows, :], slo_ref[rows, :], shi_ref[rows, :]) * scale
        q2 = jnp.concatenate([jnp.where(lane < A_QK_DIM, q, 0.0),
                              jnp.where(lane >= A_QK_DIM, q, 0.0)], axis=0).astype(BF16)
        s = _dot_nt(q2, kb_ref[...])
        p = jnp.exp2(s - jnp.max(s, axis=-1, keepdims=True))
        ov = _dot(p.astype(BF16), vb_ref[...])
        on = ov[:, 0:A_V_DIM] / ov[:, A_V_DIM:]
        o = on[0:tq] - lam * on[tq:]
        o_ref[0, rows, :] = (_rms(o, subln_ref[...]) * (1.0 - lam_init)).astype(BF16)


def _rope_tables(s):
    half = ROPE_DIM // 2
    pos = np.arange(s, dtype=np.float64)
    inv_freq = ROPE_THETA ** (-np.arange(half, dtype=np.float64) / half)
    ang = pos[:, None] * inv_freq[None, :]
    cos, sin = np.cos(ang), np.sin(ang)
    pad = A_QK_DIM - ROPE_DIM
    one = np.ones((s, pad))
    zero = np.zeros((s, pad))
    zh = np.zeros((s, half))
    cos_c = np.concatenate([cos, cos, one], axis=1)
    lo_c = np.concatenate([zh, sin, zero], axis=1)
    hi_c = np.concatenate([-sin, zh, zero], axis=1)
    rep = lambda t: jnp.asarray(np.concatenate([t, t], axis=1), dtype=F32)
    return rep(cos_c), rep(lo_c), rep(hi_c)


def diff_attention(proj, lam_vec, subln, lam_init, cast_weights=(), tq=128):
    b, s, _ = proj.shape
    tq = min(tq, s)
    cos, lo, hi = _rope_tables(s)
    hd = A_V_DIM
    steps = b * A_HEADS
    table = pl.BlockSpec((s, hd), lambda bi, h: (0, 0))
    head = lambda off: pl.BlockSpec((1, s, hd), lambda bi, h: (bi, 0, off * A_HEADS + h))

    def slab(w):
        layers, rows, cols = w.shape
        per_layer = steps // layers
        assert steps % layers == 0 and rows % (16 * per_layer) == 0
        return pl.BlockSpec((1, rows // per_layer, cols), lambda bi, h: (
            (bi * A_HEADS + h) // per_layer, (bi * A_HEADS + h) % per_layer, 0))

    slabs = [slab(w) for w in cast_weights]
    outs = pl.pallas_call(
        functools.partial(_attn_body, lam_init=lam_init, tq=tq, n_cast=len(cast_weights)),
        grid=(b, A_HEADS),
        in_specs=[pl.BlockSpec((4, A_QK_DIM), lambda bi, h: (0, 0)),
                  pl.BlockSpec((1, hd), lambda bi, h: (0, 0)),
                  table, table, table, head(0), head(1), head(2)] + slabs,
        out_specs=[pl.BlockSpec((1, s, hd), lambda bi, h: (bi, 0, h))] + slabs,
        out_shape=[jax.ShapeDtypeStruct((b, s, A_HEADS * hd), BF16)]
        + [jax.ShapeDtypeStruct(w.shape, BF16) for w in cast_weights],
        scratch_shapes=[pltpu.VMEM((s, hd), BF16), pltpu.VMEM((s, 2 * hd), BF16)],
        compiler_params=_params(("parallel", "parallel"), 56),
        name="diff_attention",
    )(lam_vec, subln.reshape(1, hd), cos, lo, hi, proj, proj, proj, *cast_weights)
    return outs[0], tuple(outs[1:])


def _gmlp_body(u_ref, v_ref, lng_ref, lnb_ref, ws_ref, bs_ref, o_ref, *, n_chunks):
    gd = B_GROUP_DIM
    for c in range(n_chunks):
        rows = slice(c * B_CHUNK, (c + 1) * B_CHUNK)
        vb = _gelu_tanh(v_ref[0, rows, :])
        xc = vb - jnp.mean(vb, axis=-1, keepdims=True)
        var = jnp.mean(xc * xc, axis=-1, keepdims=True)
        y = (xc * lax.rsqrt(var + EPS) * lng_ref[...] + lnb_ref[...]).astype(BF16)
        u = _gelu_tanh(u_ref[0, rows, :])
        for g in range(B_GROUPS):
            cols = slice(g * gd, (g + 1) * gd)
            sv = _dot(ws_ref[g], y[:, cols]) + bs_ref[:, g:g + 1]
            o_ref[0, rows, cols] = (u[:, cols] * sv).astype(BF16)


def gmlp(proj, ln_g, ln_b, w_s, b_s, u_block, n_chunks=4):
    b, s, _ = proj.shape
    width = B_GROUPS * B_GROUP_DIM
    n_chunks = min(n_chunks, s // B_CHUNK)
    t = n_chunks * B_CHUNK
    return pl.pallas_call(
        functools.partial(_gmlp_body, n_chunks=n_chunks),
        grid=(b, s // t),
        in_specs=[pl.BlockSpec((1, t, width), lambda bi, n: (bi, n, u_block)),
                  pl.BlockSpec((1, t, width), lambda bi, n: (bi, n, u_block + 1)),
                  pl.BlockSpec((1, width), lambda bi, n: (0, 0)),
                  pl.BlockSpec((1, width), lambda bi, n: (0, 0)),
                  pl.BlockSpec((B_GROUPS, B_CHUNK, B_CHUNK), lambda bi, n: (0, 0, 0)),
                  pl.BlockSpec((B_CHUNK, B_GROUPS), lambda bi, n: (0, 0))],
        out_specs=pl.BlockSpec((1, t, width), lambda bi, n: (bi, n, 0)),
        out_shape=jax.ShapeDtypeStruct((b, s, width), BF16),
        compiler_params=_params(("parallel", "parallel"), 32),
        name="gmlp",
    )(proj, proj, ln_g.reshape(1, width), ln_b.reshape(1, width), w_s.astype(BF16), b_s.T)


def _out_proj2_body(a_ref, b_ref, w_ref, r_ref, o_ref):
    ka = a_ref.shape[1]
    acc = _dot(a_ref[...], w_ref[0:ka, :]) + _dot(b_ref[...], w_ref[ka:, :])
    o_ref[...] = r_ref[...] + acc


def out_proj2(a, b, w, layer, res, tm=512, tn=2048):
    m, ka = a.shape
    kb = b.shape[1]
    n = w.shape[2]
    tm, tn = min(tm, m), min(tn, n)
    return pl.pallas_call(
        _out_proj2_body,
        grid=(m // tm, n // tn),
        in_specs=[pl.BlockSpec((tm, ka), lambda i, j: (i, 0)),
                  pl.BlockSpec((tm, kb), lambda i, j: (i, 0)),
                  pl.BlockSpec((None, ka + kb, tn), lambda i, j: (layer, 0, j)),
                  pl.BlockSpec((tm, tn), lambda i, j: (i, j))],
        out_specs=pl.BlockSpec((tm, tn), lambda i, j: (i, j)),
        out_shape=jax.ShapeDtypeStruct((m, n), F32),
        compiler_params=_params(("parallel", "parallel"), 48),
        name="out_proj_even",
    )(a, b, w, res)


def _out_proj_gated_body(o_in_ref, gate_ref, gn_ref, w_ref, r_ref, o_ref, y_ref):
    @pl.when(pl.program_id(1) == 0)
    def _():
        gate = gate_ref[...]
        y = _rms(o_in_ref[...], gn_ref[...]) * (gate * _sigmoid(gate))
        y_ref[...] = y.astype(BF16)

    o_ref[...] = r_ref[...] + _dot(y_ref[...], w_ref[...])


def out_proj_gated(o_in, proj, gate_block, gain, w, layer, res, tm=512, tn=2048):
    m, k = o_in.shape
    n = w.shape[2]
    tm, tn = min(tm, m), min(tn, n)
    return pl.pallas_call(
        _out_proj_gated_body,
        grid=(m // tm, n // tn),
        in_specs=[pl.BlockSpec((tm, k), lambda i, j: (i, 0)),
                  pl.BlockSpec((tm, k), lambda i, j: (i, gate_block)),
                  pl.BlockSpec((1, k), lambda i, j: (0, 0)),
                  pl.BlockSpec((None, k, tn), lambda i, j: (layer, 0, j)),
                  pl.BlockSpec((tm, tn), lambda i, j: (i, j))],
        out_specs=pl.BlockSpec((tm, tn), lambda i, j: (i, j)),
        out_shape=jax.ShapeDtypeStruct((m, n), F32),
        scratch_shapes=[pltpu.VMEM((tm, k), BF16)],
        compiler_params=_params(("parallel", "arbitrary"), 56),
        name="out_proj_odd",
    )(o_in, proj, gain.reshape(1, k), w, res)


def _ffn_body(*refs, final_norm):
    if final_norm:
        x_ref, g_ref, wg_ref, wu_ref, wd_ref, fn_ref, o_ref, xn_ref = refs
    else:
        x_ref, g_ref, wg_ref, wu_ref, wd_ref, o_ref, xn_ref = refs
    f = pl.program_id(1)

    @pl.when(f == 0)
    def _():
        x = x_ref[...]
        xn_ref[...] = _rms(x, g_ref[...]).astype(BF16)
        o_ref[...] = x

    xn = xn_ref[...]
    a = _dot(xn, wg_ref[...])
    u = _dot(xn, wu_ref[...])
    hid = (a * _sigmoid(a) * u).astype(BF16)
    o_ref[...] += _dot(hid, wd_ref[...])

    if final_norm:
        @pl.when(f == pl.num_programs(1) - 1)
        def _():
            o_ref[...] = _rms(o_ref[...], fn_ref[...])


def ffn(x, gain, w_gate, w_up, w_down, layer, final_gain=None, tm=1024, tf=512):
    m, k = x.shape
    hdim = w_gate.shape[2]
    tm, tf = min(tm, m), min(tf, hdim)
    vec = pl.BlockSpec((1, k), lambda i, f: (0, 0))
    in_specs = [pl.BlockSpec((tm, k), lambda i, f: (i, 0)), vec,
                pl.BlockSpec((None, k, tf), lambda i, f: (layer, 0, f)),
                pl.BlockSpec((None, k, tf), lambda i, f: (layer, 0, f)),
                pl.BlockSpec((None, tf, k), lambda i, f: (layer, f, 0))]
    args = [x, gain.reshape(1, k), w_gate, w_up, w_down]
    if final_gain is not None:
        in_specs.append(vec)
        args.append(final_gain.reshape(1, k))
    return pl.pallas_call(
        functools.partial(_ffn_body, final_norm=final_gain is not None),
        grid=(m // tm, hdim // tf),
        in_specs=in_specs,
        out_specs=pl.BlockSpec((tm, k), lambda i, f: (i, 0)),
        out_shape=jax.ShapeDtypeStruct((m, k), F32),
        scratch_shapes=[pltpu.VMEM((tm, k), BF16)],
        compiler_params=_params(("parallel", "arbitrary"), 60),
        name="ffn",
    )(*args)


_HALF = 8


def _hgrn_tables(ti, *, proj_ref, lbs, tri_ref, bcum_ref, ckey_ref, qs_ref, val_ref, vb_ref,
                 upd_ref, dec_ref):
    c = HG_CHUNK
    rows = [(ti * HG_PREP + u) * c for u in range(HG_PREP)]
    kdec = [[None, None] for _ in range(HG_PREP)]
    decay = [[None, None] for _ in range(HG_PREP)]
    for d in range(2):
        lb = lbs[d]
        splits, log_k = [], []
        for r in rows:
            f = lb + (1.0 - lb) * _sigmoid(proj_ref[r:r + c, (1 + d) * C_DK:(2 + d) * C_DK])
            logf = jnp.log2(f)
            log_k.append(jnp.log2(1.0 - f))
            hi = logf.astype(BF16)
            rem = logf - hi.astype(F32)
            mid = rem.astype(BF16)
            low = (rem - mid.astype(F32)).astype(BF16)
            splits.append(jnp.concatenate([hi, mid, low], axis=0))
        bc = _dot(tri_ref[d], jnp.concatenate(splits, axis=1))
        for u, r in enumerate(rows):
            bcum = bc[:, u * C_DK:(u + 1) * C_DK]
            ckey = bcum - log_k[u]
            bcum_ref[d, pl.ds(r, c), :] = bcum
            ckey_ref[d, pl.ds(r, c), :] = ckey
            edge = bcum[c - 1:c] if d == 0 else bcum[0:1]
            kdec[u][d] = jnp.exp2(edge - ckey).astype(BF16)
            decay[u][d] = jnp.exp2(edge)
    for u, r in enumerate(rows):
        ci = ti * HG_PREP + u
        val = proj_ref[r:r + c, 3 * C_DK:4 * C_DK]
        val_ref[pl.ds(r, c), :] = val
        vb = val.astype(BF16)
        vb_ref[pl.ds(r, c), :] = vb
        upd_ref[ci] = _dot_tn(vb, jnp.concatenate(kdec[u], axis=1))
        dec_ref[ci] = jnp.concatenate(decay[u], axis=1)
        q = proj_ref[r:r + c, 0:C_DK]
        qs_ref[pl.ds(r, c), :] = q * _sigmoid(q)


def _hgrn_scan(ci, carry, *, upd_ref, dec_ref, sbf_ref, nc):
    st_f, st_b = carry
    cb = nc - 1 - ci
    sbf_ref[0, ci] = st_f.astype(BF16)
    sbf_ref[1, cb] = st_b.astype(BF16)
    st_f = st_f * dec_ref[ci, :, 0:C_DK] + upd_ref[ci, :, 0:C_DK]
    st_b = st_b * dec_ref[cb, :, C_DK:2 * C_DK] + upd_ref[cb, :, C_DK:2 * C_DK]
    return st_f, st_b


def _direct_pairs(reverse):
    pairs = []
    for s in range(HG_SUB):
        for h in range(2):
            sh = s // _HALF
            if not ((h > sh) if reverse else (h < sh)):
                pairs.append((s, h))
    return pairs


def _hgrn_direct_pieces(d, r0, *, bcum_ref, ckey_ref, qs_ref):
    reverse = d == 1
    t_idx = lax.broadcasted_iota(jnp.int32, (_HALF, C_DK), 0)
    b = [bcum_ref[d, pl.ds(r0 + _HALF * h, _HALF), :] for h in range(2)]
    q = [qs_ref[pl.ds(r0 + _HALF * h, _HALF), :] for h in range(2)]
    pieces = []
    for s, h in _direct_pairs(reverse):
        p = q[h] * jnp.exp2(b[h] - ckey_ref[d, pl.ds(r0 + s, 1), :])
        if h == s // _HALF:
            t0 = s - _HALF * h
            p = jnp.where((t_idx <= t0) if reverse else (t_idx >= t0), p, 0.0)
        pieces.append(p)
    rows = len(pieces) // 2 * _HALF
    lhs = jnp.concatenate(pieces, axis=0).astype(BF16)
    return jnp.concatenate([lhs[:rows], lhs[rows:]], axis=1)


def _hgrn_direct_combine(d, r0, rs_ref, row0, *, val_ref):
    pairs = _direct_pairs(d == 1)
    n_half = len(pairs) // 2
    out = [jnp.zeros((_HALF, C_DK), F32), jnp.zeros((_HALF, C_DK), F32)]
    for p_idx, (s, h) in enumerate(pairs):
        col, row = divmod(p_idx, n_half)
        rp = rs_ref[row0 + row * _HALF:row0 + (row + 1) * _HALF, col * C_DK:(col + 1) * C_DK]
        out[h] = out[h] + rp * val_ref[pl.ds(r0 + s, 1), :]
    return jnp.concatenate(out, axis=0)


def _pivot_plan(reverse):
    s, h = HG_SUB, HG_CHUNK // 2
    assert HG_CHUNK == 4 * HG_SUB
    if reverse:
        return [(0, h, h, h, h), (0, s, s, s, s), (h, s, h + s, s, h + s)]
    return [(h, h, 0, h, h - 1), (s, s, 0, s, s - 1), (h + s, s, h, s, h + s - 1)]


def _hgrn_output(ti, carry, *, val_ref, o_ref, bcum_ref, ckey_ref, qs_ref, vb_ref, sbf_ref,
                 w2_ref, pmask_ref, lhs_ref, rs_ref):
    c, sub = HG_CHUNK, HG_SUB
    n_sub = c // sub
    blk_rows = len(_direct_pairs(False)) // 2 * _HALF
    rows = [pl.multiple_of((ti * HG_PAR + u) * c, c) for u in range(HG_PAR)]
    for u, r in enumerate(rows):
        for d in range(2):
            for i in range(n_sub):
                row0 = ((u * 2 + d) * n_sub + i) * blk_rows
                lhs_ref[row0:row0 + blk_rows, :] = _hgrn_direct_pieces(
                    d, r + i * sub, bcum_ref=bcum_ref, ckey_ref=ckey_ref, qs_ref=qs_ref)
    rs_ref[...] = _dot(lhs_ref[...], w2_ref[...])
    pv_groups = []
    for g in range(HG_PAR // HG_PIV):
        q_parts, k_parts, v_parts = [], [], []
        for r in rows[g * HG_PIV:(g + 1) * HG_PIV]:
            for d in range(2):
                for q0, qn, k0, kn, prow in _pivot_plan(d == 1):
                    piv = bcum_ref[d, pl.ds(r + prow, 1), :]
                    q_parts.append(qs_ref[pl.ds(r + q0, qn), :]
                                   * jnp.exp2(bcum_ref[d, pl.ds(r + q0, qn), :] - piv))
                    k_parts.append(jnp.exp2(piv - ckey_ref[d, pl.ds(r + k0, kn), :]))
                    v_parts.append(vb_ref[pl.ds(r + k0, kn), :])
        sc = _dot_nt(jnp.concatenate(q_parts, axis=0).astype(BF16),
                     jnp.concatenate(k_parts, axis=0).astype(BF16)) * pmask_ref[...]
        pv_groups.append(_dot(sc.astype(BF16), jnp.concatenate(v_parts, axis=0)))
    for u, r in enumerate(rows):
        ci = ti * HG_PAR + u
        total = None
        for d in range(2):
            blocks = [_hgrn_direct_combine(d, r + i * sub, rs_ref,
                                           ((u * 2 + d) * n_sub + i) * blk_rows, val_ref=val_ref)
                      for i in range(n_sub)]
            pv = pv_groups[u // HG_PIV]
            off = ((u % HG_PIV) * 2 + d) * c
            for q0, qn, _, _, _ in _pivot_plan(d == 1):
                for j in range(qn // sub):
                    blocks[q0 // sub + j] = (blocks[q0 // sub + j]
                                             + pv[off + j * sub:off + (j + 1) * sub])
                off += qn
            qe = (qs_ref[pl.ds(r, c), :] * jnp.exp2(bcum_ref[d, pl.ds(r, c), :])).astype(BF16)
            o_d = _dot_nt(qe, sbf_ref[d, ci]) + jnp.concatenate(blocks, axis=0)
            total = o_d if total is None else total + o_d
        o_ref[0, pl.ds(r, c), :] = total
    return carry


def _hgrn_body(xn_ref, wq_ref, wf_ref, wb_ref, wi_ref, lbr_ref, o_ref,
               bcum_ref, ckey_ref, qs_ref, val_ref, vb_ref, upd_ref, dec_ref, sbf_ref, tri_ref,
               w2_ref, pmask_ref, lhs_ref, rs_ref, proj_ref, wcat_ref, *, depth, layer_idx):
    s = xn_ref.shape[1]
    c = HG_CHUNK
    nc = s // c
    assert nc % HG_PAR == 0 and nc % HG_PREP == 0 and HG_PAR % HG_PIV == 0

    def lower_bound(d):
        rows = lbr_ref[d * depth:(d + 1) * depth, :]
        e = jnp.exp(rows - jnp.max(rows, axis=0, keepdims=True))
        sm = e / jnp.sum(e, axis=0, keepdims=True)
        acc = sm[0:1] * 0.0
        for l in range(1, layer_idx + 1):
            acc = acc + sm[l:l + 1]
        return acc

    row = lax.broadcasted_iota(jnp.int32, (c, 3 * c), 0)
    col = lax.broadcasted_iota(jnp.int32, (c, 3 * c), 1) % c
    tri_ref[0] = (col <= row).astype(BF16)
    tri_ref[1] = (col >= row).astype(BF16)
    wr = lax.broadcasted_iota(jnp.int32, (2 * C_DK, 2 * C_DK), 0) // C_DK
    wc = lax.broadcasted_iota(jnp.int32, (2 * C_DK, 2 * C_DK), 1) // C_DK
    w2_ref[...] = (wr == wc).astype(BF16)
    n = HG_PIV * 2 * c
    pr = lax.broadcasted_iota(jnp.int32, (n, n), 0)
    pc = lax.broadcasted_iota(jnp.int32, (n, n), 1)
    entry = lambda x: jnp.where(x % c < c // 2, 0, (x % c) // HG_SUB)
    pmask_ref[...] = ((pr // c == pc // c) & (entry(pr) == entry(pc))).astype(F32)

    for g, w_ref in enumerate((wq_ref, wf_ref, wb_ref, wi_ref)):
        wcat_ref[:, g * C_DK:(g + 1) * C_DK] = w_ref[...]
    blk = HG_PREP * c

    def project(t):
        proj_ref[t * blk:(t + 1) * blk, :] = _dot(xn_ref[0, t * blk:(t + 1) * blk, :], wcat_ref[...])

    lbs = (lower_bound(0), lower_bound(1))
    project(0)
    for t in range(nc // HG_PREP):
        if t + 1 < nc // HG_PREP:
            project(t + 1)
        _hgrn_tables(t, proj_ref=proj_ref, lbs=lbs, tri_ref=tri_ref, bcum_ref=bcum_ref,
                     ckey_ref=ckey_ref, qs_ref=qs_ref, val_ref=val_ref, vb_ref=vb_ref,
                     upd_ref=upd_ref, dec_ref=dec_ref)
    zero = jnp.zeros((C_DK, C_DK), F32)
    lax.fori_loop(0, nc, functools.partial(
        _hgrn_scan, upd_ref=upd_ref, dec_ref=dec_ref, sbf_ref=sbf_ref, nc=nc), (zero, zero))
    lax.fori_loop(0, nc // HG_PAR, functools.partial(
        _hgrn_output, val_ref=val_ref, o_ref=o_ref, bcum_ref=bcum_ref, ckey_ref=ckey_ref,
        qs_ref=qs_ref, vb_ref=vb_ref, sbf_ref=sbf_ref, w2_ref=w2_ref, pmask_ref=pmask_ref,
        lhs_ref=lhs_ref, rs_ref=rs_ref), 0)


def hgrn2(xn, w_in, layer, lower_bounds, layer_idx):
    b, s, k = xn.shape
    depth = lower_bounds.shape[1]
    dk = C_DK
    nc = s // HG_CHUNK
    direct_rows = HG_PAR * 2 * (HG_CHUNK // HG_SUB) * (len(_direct_pairs(False)) // 2 * _HALF)
    pivot_rows = HG_PIV * 2 * HG_CHUNK
    wblk = lambda off: pl.BlockSpec((None, k, dk), lambda bi, h: (layer, 0, off * C_HEADS + h))
    return pl.pallas_call(
        functools.partial(_hgrn_body, depth=depth, layer_idx=layer_idx),
        grid=(b, C_HEADS),
        in_specs=[pl.BlockSpec((1, s, k), lambda bi, h: (bi, 0, 0)),
                  wblk(0), wblk(1), wblk(2), wblk(3),
                  pl.BlockSpec((2 * depth, dk), lambda bi, h: (0, h))],
        out_specs=pl.BlockSpec((1, s, dk), lambda bi, h: (bi, 0, h)),
        out_shape=jax.ShapeDtypeStruct((b, s, C_HEADS * dk), F32),
        scratch_shapes=[pltpu.VMEM((2, s, dk), F32),
                        pltpu.VMEM((2, s, dk), F32),
                        pltpu.VMEM((s, dk), F32),
                        pltpu.VMEM((s, dk), F32),
                        pltpu.VMEM((s, dk), BF16),
                        pltpu.VMEM((nc, dk, 2 * dk), F32),
                        pltpu.VMEM((nc, 1, 2 * dk), F32),
                        pltpu.VMEM((2, nc, dk, dk), BF16),
                        pltpu.VMEM((2, HG_CHUNK, 3 * HG_CHUNK), BF16),
                        pltpu.VMEM((2 * dk, 2 * dk), BF16),
                        pltpu.VMEM((pivot_rows, pivot_rows), F32),
                        pltpu.VMEM((direct_rows, 2 * dk), BF16),
                        pltpu.VMEM((direct_rows, 2 * dk), F32),
                        pltpu.VMEM((s, 4 * dk), F32),
                        pltpu.VMEM((k, 4 * dk), BF16)],
        compiler_params=_params(("parallel", "parallel"), 56),
        name="hgrn2",
    )(xn, w_in, w_in, w_in, w_in, lower_bounds.reshape(2 * depth, C_HEADS * dk))


def kernel(x, mix_norm, even_w_in, even_w_out, diff_lq1, diff_lk1, diff_lq2, diff_lk2,
           diff_subln, gmlp_ln_g, gmlp_ln_b, gmlp_w_s, gmlp_b_s, hgrn_w_in, hgrn_w_out,
           hgrn_lower_bounds, hgrn_g_norm, ffn_norm, ffn_w_gate, ffn_w_up, ffn_w_down,
           final_norm):
    b, s, d = x.shape
    depth = mix_norm.shape[0]
    m = b * s
    h = x.reshape(m, d)
    even_w_in = even_w_in.astype(BF16)
    ffn_ws = None
    for l in range(depth):
        if l % 2 == 0:
            e = l // 2
            proj = norm_proj(h, mix_norm[l], even_w_in, e).reshape(b, s, -1)
            lam_init = 0.8 - 0.6 * math.exp(-0.3 * l)
            lam_vec = jnp.stack([diff_lq1[e], diff_lk1[e], diff_lq2[e], diff_lk2[e]]).astype(F32)
            to_cast = ((ffn_w_gate, ffn_w_up, ffn_w_down, hgrn_w_in, hgrn_w_out, even_w_out)
                       if ffn_ws is None else ())
            oa, cast = diff_attention(proj, lam_vec, diff_subln[e], lam_init, to_cast)
            if to_cast:
                ffn_ws, (hgrn_w_in, hgrn_w_out, even_w_out) = cast[:3], cast[3:]
            u_block = (3 * A_HEADS * A_V_DIM) // (B_GROUPS * B_GROUP_DIM)
            ob = gmlp(proj, gmlp_ln_g[e], gmlp_ln_b[e], gmlp_w_s[e], gmlp_b_s[e], u_block)
            h = out_proj2(oa.reshape(m, -1), ob.reshape(m, -1), even_w_out, e, h)
        else:
            o = l // 2
            width = C_HEADS * C_DK
            gate, xn = norm_proj(h, mix_norm[l], hgrn_w_in, o, col0=4 * width, emit_norm=True)
            scan = hgrn2(xn.reshape(b, s, d), hgrn_w_in, o, hgrn_lower_bounds, l)
            h = out_proj_gated(scan.reshape(m, -1), gate, 0, hgrn_g_norm[o], hgrn_w_out, o, h)
        last = l == depth - 1
        w_gate, w_up, w_down = ffn_ws
        h = ffn(h, ffn_norm[l], w_gate, w_up, w_down, l,
                final_gain=final_norm if last else None)
    return h.reshape(b, s, d)
```

```python
import functools
import math

import jax
import jax.numpy as jnp
import numpy as np
from jax import lax
from jax.experimental import pallas as pl
from jax.experimental.pallas import tpu as pltpu

F32 = jnp.float32
BF16 = jnp.bfloat16
EPS = 1e-6
MIB = 1024 * 1024

A_HEADS = 8
A_QK_DIM = 64
A_V_DIM = 128
ROPE_THETA = 500000.0
ROPE_DIM = A_QK_DIM // 4
B_GROUPS = 8
B_GROUP_DIM = 128
B_CHUNK = 128
C_HEADS = 16
C_DK = 128
HG_CHUNK = 64
HG_SUB = 16
HG_PAR = 16
HG_PIV = 2
HG_PREP = 8


def _params(semantics, vmem_mib):
    return pltpu.CompilerParams(dimension_semantics=semantics,
                                vmem_limit_bytes=vmem_mib * MIB)


def _rms(x, gain):
    ms = jnp.mean(x * x, axis=-1, keepdims=True)
    return x * lax.rsqrt(ms + EPS) * gain


def _sigmoid(x):
    return 1.0 / (1.0 + jnp.exp(-x))


def _gelu_tanh(x):
    c = math.sqrt(2.0 / math.pi)
    return x * (0.5 * (1.0 + jnp.tanh(c * (x + 0.044715 * (x * x * x)))))


def _dot(a, b):
    return jnp.dot(a, b, preferred_element_type=F32)


def _dot_nt(a, b):
    return lax.dot_general(a, b, (((1,), (1,)), ((), ())), preferred_element_type=F32)


def _dot_tn(a, b):
    return lax.dot_general(a, b, (((0,), (0,)), ((), ())), preferred_element_type=F32)


def _norm_proj_body(x_ref, g_ref, w_ref, o_ref, xn_ref):
    @pl.when(pl.program_id(1) == 0)
    def _():
        xn_ref[...] = _rms(x_ref[...], g_ref[...]).astype(BF16)

    o_ref[...] = _dot(xn_ref[...], w_ref[...])


def norm_proj(x, gain, w, layer, col0=0, n=None, emit_norm=False, tm=1024, tn=1024):
    m, k = x.shape
    n = w.shape[2] - col0 if n is None else n
    tm, tn = min(tm, m), min(tn, n)
    assert col0 % tn == 0
    out_specs = pl.BlockSpec((tm, tn), lambda i, j: (i, j))
    out_shape = jax.ShapeDtypeStruct((m, n), F32)
    scratch = [pltpu.VMEM((tm, k), BF16)]
    if emit_norm:
        out_specs = [out_specs, pl.BlockSpec((tm, k), lambda i, j: (i, 0))]
        out_shape = [out_shape, jax.ShapeDtypeStruct((m, k), BF16)]
        scratch = []
    return pl.pallas_call(
        _norm_proj_body,
        grid=(m // tm, n // tn),
        in_specs=[pl.BlockSpec((tm, k), lambda i, j: (i, 0)),
                  pl.BlockSpec((1, k), lambda i, j: (0, 0)),
                  pl.BlockSpec((None, k, tn), lambda i, j: (layer, 0, col0 // tn + j))],
        out_specs=out_specs,
        out_shape=out_shape,
        scratch_shapes=scratch,
        compiler_params=_params(("parallel", "arbitrary"), 48),
        name="norm_proj",
    )(x, gain.reshape(1, k), w)


def _rope(x, cos, sin_lo, sin_hi):
    half = ROPE_DIM // 2
    return (x * cos + pltpu.roll(x, half, 1) * sin_lo
            + pltpu.roll(x, x.shape[1] - half, 1) * sin_hi)


def _attn_body(*refs, lam_init, tq, n_cast):
    lam_ref, subln_ref, cos_ref, slo_ref, shi_ref, q_ref, k_ref, v_ref = refs[:8]
    w_refs = refs[8:8 + n_cast]
    o_ref = refs[8 + n_cast]
    wb_refs = refs[9 + n_cast:9 + 2 * n_cast]
    kb_ref, vb_ref = refs[9 + 2 * n_cast:]
    for w_ref, wb_ref in zip(w_refs, wb_refs):
        wb_ref[...] = w_ref[...].astype(BF16)
    s_len = k_ref.shape[1]
    kb_ref[...] = _rope(k_ref[0], cos_ref[...], slo_ref[...], shi_ref[...]).astype(BF16)
    vb_ref[:, 0:A_V_DIM] = v_ref[0].astype(BF16)
    vb_ref[:, A_V_DIM:] = jnp.ones((s_len, A_V_DIM), BF16)
    lv = lam_ref[...]
    lam = (jnp.exp(jnp.sum(lv[0:1] * lv[1:2], axis=-1, keepdims=True))
           - jnp.exp(jnp.sum(lv[2:3] * lv[3:4], axis=-1, keepdims=True)) + lam_init)
    scale = A_QK_DIM ** -0.5 * math.log2(math.e)
    lane = lax.broadcasted_iota(jnp.int32, (tq, A_V_DIM), 1)
    for j in range(s_len // tq):
        rows = slice(j * tq, (j + 1) * tq)
        q = _rope(q_ref[0, rows, :], cos_ref[rows, :], slo_ref[rows, :], shi_ref[rows, :]) * scale
        q2 = jnp.concatenate([jnp.where(lane < A_QK_DIM, q, 0.0),
                              jnp.where(lane >= A_QK_DIM, q, 0.0)], axis=0).astype(BF16)
        s = _dot_nt(q2, kb_ref[...])
        p = jnp.exp2(s - jnp.max(s, axis=-1, keepdims=True))
        ov = _dot(p.astype(BF16), vb_ref[...])
        on = ov[:, 0:A_V_DIM] / ov[:, A_V_DIM:]
        o = on[0:tq] - lam * on[tq:]
        o_ref[0, rows, :] = (_rms(o, subln_ref[...]) * (1.0 - lam_init)).astype(BF16)


def _rope_tables(s):
    half = ROPE_DIM // 2
    pos = np.arange(s, dtype=np.float64)
    inv_freq = ROPE_THETA ** (-np.arange(half, dtype=np.float64) / half)
    ang = pos[:, None] * inv_freq[None, :]
    cos, sin = np.cos(ang), np.sin(ang)
    pad = A_QK_DIM - ROPE_DIM
    one = np.ones((s, pad))
    zero = np.zeros((s, pad))
    zh = np.zeros((s, half))
    cos_c = np.concatenate([cos, cos, one], axis=1)
    lo_c = np.concatenate([zh, sin, zero], axis=1)
    hi_c = np.concatenate([-sin, zh, zero], axis=1)
    rep = lambda t: jnp.asarray(np.concatenate([t, t], axis=1), dtype=F32)
    return rep(cos_c), rep(lo_c), rep(hi_c)


def diff_attention(proj, lam_vec, subln, lam_init, cast_weights=(), tq=128):
    b, s, _ = proj.shape
    tq = min(tq, s)
    cos, lo, hi = _rope_tables(s)
    hd = A_V_DIM
    steps = b * A_HEADS
    table = pl.BlockSpec((s, hd), lambda bi, h: (0, 0))
    head = lambda off: pl.BlockSpec((1, s, hd), lambda bi, h: (bi, 0, off * A_HEADS + h))

    def slab(w):
        layers, rows, cols = w.shape
        per_layer = steps // layers
        assert steps % layers == 0 and rows % (16 * per_layer) == 0
        return pl.BlockSpec((1, rows // per_layer, cols), lambda bi, h: (
            (bi * A_HEADS + h) // per_layer, (bi * A_HEADS + h) % per_layer, 0))

    slabs = [slab(w) for w in cast_weights]
    outs = pl.pallas_call(
        functools.partial(_attn_body, lam_init=lam_init, tq=tq, n_cast=len(cast_weights)),
        grid=(b, A_HEADS),
        in_specs=[pl.BlockSpec((4, A_QK_DIM), lambda bi, h: (0, 0)),
                  pl.BlockSpec((1, hd), lambda bi, h: (0, 0)),
                  table, table, table, head(0), head(1), head(2)] + slabs,
        out_specs=[pl.BlockSpec((1, s, hd), lambda bi, h: (bi, 0, h))] + slabs,
        out_shape=[jax.ShapeDtypeStruct((b, s, A_HEADS * hd), BF16)]
        + [jax.ShapeDtypeStruct(w.shape, BF16) for w in cast_weights],
        scratch_shapes=[pltpu.VMEM((s, hd), BF16), pltpu.VMEM((s, 2 * hd), BF16)],
        compiler_params=_params(("parallel", "parallel"), 56),
        name="diff_attention",
    )(lam_vec, subln.reshape(1, hd), cos, lo, hi, proj, proj, proj, *cast_weights)
    return outs[0], tuple(outs[1:])


def _gmlp_body(u_ref, v_ref, lng_ref, lnb_ref, ws_ref, bs_ref, o_ref, *, n_chunks):
    gd = B_GROUP_DIM
    for c in range(n_chunks):
        rows = slice(c * B_CHUNK, (c + 1) * B_CHUNK)
        vb = _gelu_tanh(v_ref[0, rows, :])
        xc = vb - jnp.mean(vb, axis=-1, keepdims=True)
        var = jnp.mean(xc * xc, axis=-1, keepdims=True)
        y = (xc * lax.rsqrt(var + EPS) * lng_ref[...] + lnb_ref[...]).astype(BF16)
        u = _gelu_tanh(u_ref[0, rows, :])
        for g in range(B_GROUPS):
            cols = slice(g * gd, (g + 1) * gd)
            sv = _dot(ws_ref[g], y[:, cols]) + bs_ref[:, g:g + 1]
            o_ref[0, rows, cols] = (u[:, cols] * sv).astype(BF16)


def gmlp(proj, ln_g, ln_b, w_s, b_s, u_block, n_chunks=8):
    b, s, _ = proj.shape
    width = B_GROUPS * B_GROUP_DIM
    n_chunks = min(n_chunks, s // B_CHUNK)
    t = n_chunks * B_CHUNK
    return pl.pallas_call(
        functools.partial(_gmlp_body, n_chunks=n_chunks),
        grid=(b, s // t),
        in_specs=[pl.BlockSpec((1, t, width), lambda bi, n: (bi, n, u_block)),
                  pl.BlockSpec((1, t, width), lambda bi, n: (bi, n, u_block + 1)),
                  pl.BlockSpec((1, width), lambda bi, n: (0, 0)),
                  pl.BlockSpec((1, width), lambda bi, n: (0, 0)),
                  pl.BlockSpec((B_GROUPS, B_CHUNK, B_CHUNK), lambda bi, n: (0, 0, 0)),
                  pl.BlockSpec((B_CHUNK, B_GROUPS), lambda bi, n: (0, 0))],
        out_specs=pl.BlockSpec((1, t, width), lambda bi, n: (bi, n, 0)),
        out_shape=jax.ShapeDtypeStruct((b, s, width), BF16),
        compiler_params=_params(("parallel", "parallel"), 32),
        name="gmlp",
    )(proj, proj, ln_g.reshape(1, width), ln_b.reshape(1, width), w_s.astype(BF16), b_s.T)


def _out_proj2_body(a_ref, b_ref, w_ref, r_ref, o_ref):
    ka = a_ref.shape[1]
    acc = _dot(a_ref[...], w_ref[0:ka, :]) + _dot(b_ref[...], w_ref[ka:, :])
    o_ref[...] = r_ref[...] + acc


def out_proj2(a, b, w, layer, res, tm=512, tn=2048):
    m, ka = a.shape
    kb = b.shape[1]
    n = w.shape[2]
    tm, tn = min(tm, m), min(tn, n)
    return pl.pallas_call(
        _out_proj2_body,
        grid=(m // tm, n // tn),
        in_specs=[pl.BlockSpec((tm, ka), lambda i, j: (i, 0)),
                  pl.BlockSpec((tm, kb), lambda i, j: (i, 0)),
                  pl.BlockSpec((None, ka + kb, tn), lambda i, j: (layer, 0, j)),
                  pl.BlockSpec((tm, tn), lambda i, j: (i, j))],
        out_specs=pl.BlockSpec((tm, tn), lambda i, j: (i, j)),
        out_shape=jax.ShapeDtypeStruct((m, n), F32),
        compiler_params=_params(("parallel", "parallel"), 48),
        name="out_proj_even",
    )(a, b, w, res)


def _out_proj_gated_body(o_in_ref, gate_ref, gn_ref, w_ref, r_ref, o_ref, y_ref):
    @pl.when(pl.program_id(1) == 0)
    def _():
        gate = gate_ref[...]
        y = _rms(o_in_ref[...], gn_ref[...]) * (gate * _sigmoid(gate))
        y_ref[...] = y.astype(BF16)

    o_ref[...] = r_ref[...] + _dot(y_ref[...], w_ref[...])


def out_proj_gated(o_in, proj, gate_block, gain, w, layer, res, tm=512, tn=2048):
    m, k = o_in.shape
    n = w.shape[2]
    tm, tn = min(tm, m), min(tn, n)
    return pl.pallas_call(
        _out_proj_gated_body,
        grid=(m // tm, n // tn),
        in_specs=[pl.BlockSpec((tm, k), lambda i, j: (i, 0)),
                  pl.BlockSpec((tm, k), lambda i, j: (i, gate_block)),
                  pl.BlockSpec((1, k), lambda i, j: (0, 0)),
                  pl.BlockSpec((None, k, tn), lambda i, j: (layer, 0, j)),
                  pl.BlockSpec((tm, tn), lambda i, j: (i, j))],
        out_specs=pl.BlockSpec((tm, tn), lambda i, j: (i, j)),
        out_shape=jax.ShapeDtypeStruct((m, n), F32),
        scratch_shapes=[pltpu.VMEM((tm, k), BF16)],
        compiler_params=_params(("parallel", "arbitrary"), 56),
        name="out_proj_odd",
    )(o_in, proj, gain.reshape(1, k), w, res)


def _ffn_body(*refs, final_norm):
    if final_norm:
        x_ref, g_ref, wg_ref, wu_ref, wd_ref, fn_ref, o_ref, xn_ref = refs
    else:
        x_ref, g_ref, wg_ref, wu_ref, wd_ref, o_ref, xn_ref = refs
    f = pl.program_id(1)

    @pl.when(f == 0)
    def _():
        x = x_ref[...]
        xn_ref[...] = _rms(x, g_ref[...]).astype(BF16)
        o_ref[...] = x

    xn = xn_ref[...]
    a = _dot(xn, wg_ref[...])
    u = _dot(xn, wu_ref[...])
    hid = (a * _sigmoid(a) * u).astype(BF16)
    o_ref[...] += _dot(hid, wd_ref[...])

    if final_norm:
        @pl.when(f == pl.num_programs(1) - 1)
        def _():
            o_ref[...] = _rms(o_ref[...], fn_ref[...])


def ffn(x, gain, w_gate, w_up, w_down, layer, final_gain=None, tm=1024, tf=512):
    m, k = x.shape
    hdim = w_gate.shape[2]
    tm, tf = min(tm, m), min(tf, hdim)
    vec = pl.BlockSpec((1, k), lambda i, f: (0, 0))
    in_specs = [pl.BlockSpec((tm, k), lambda i, f: (i, 0)), vec,
                pl.BlockSpec((None, k, tf), lambda i, f: (layer, 0, f)),
                pl.BlockSpec((None, k, tf), lambda i, f: (layer, 0, f)),
                pl.BlockSpec((None, tf, k), lambda i, f: (layer, f, 0))]
    args = [x, gain.reshape(1, k), w_gate, w_up, w_down]
    if final_gain is not None:
        in_specs.append(vec)
        args.append(final_gain.reshape(1, k))
    return pl.pallas_call(
        functools.partial(_ffn_body, final_norm=final_gain is not None),
        grid=(m // tm, hdim // tf),
        in_specs=in_specs,
        out_specs=pl.BlockSpec((tm, k), lambda i, f: (i, 0)),
        out_shape=jax.ShapeDtypeStruct((m, k), F32),
        scratch_shapes=[pltpu.VMEM((tm, k), BF16)],
        compiler_params=_params(("parallel", "arbitrary"), 60),
        name="ffn",
    )(*args)


_HALF = 8


def _hgrn_tables(ti, *, proj_ref, lbs, tri_ref, bcum_ref, ckey_ref, qs_ref, val_ref, vb_ref,
                 upd_ref, dec_ref):
    c = HG_CHUNK
    rows = [(ti * HG_PREP + u) * c for u in range(HG_PREP)]
    kdec = [[None, None] for _ in range(HG_PREP)]
    decay = [[None, None] for _ in range(HG_PREP)]
    for d in range(2):
        lb = lbs[d]
        splits, log_k = [], []
        for r in rows:
            f = lb + (1.0 - lb) * _sigmoid(proj_ref[r:r + c, (1 + d) * C_DK:(2 + d) * C_DK])
            logf = jnp.log2(f)
            log_k.append(jnp.log2(1.0 - f))
            hi = logf.astype(BF16)
            rem = logf - hi.astype(F32)
            mid = rem.astype(BF16)
            low = (rem - mid.astype(F32)).astype(BF16)
            splits.append(jnp.concatenate([hi, mid, low], axis=0))
        bc = _dot(tri_ref[d], jnp.concatenate(splits, axis=1))
        for u, r in enumerate(rows):
            bcum = bc[:, u * C_DK:(u + 1) * C_DK]
            ckey = bcum - log_k[u]
            bcum_ref[d, pl.ds(r, c), :] = bcum
            ckey_ref[d, pl.ds(r, c), :] = ckey
            edge = bcum[c - 1:c] if d == 0 else bcum[0:1]
            kdec[u][d] = jnp.exp2(edge - ckey).astype(BF16)
            decay[u][d] = jnp.exp2(edge)
    for u, r in enumerate(rows):
        ci = ti * HG_PREP + u
        val = proj_ref[r:r + c, 3 * C_DK:4 * C_DK]
        val_ref[pl.ds(r, c), :] = val
        vb = val.astype(BF16)
        vb_ref[pl.ds(r, c), :] = vb
        upd_ref[ci] = _dot_tn(vb, jnp.concatenate(kdec[u], axis=1))
        dec_ref[ci] = jnp.concatenate(decay[u], axis=1)
        q = proj_ref[r:r + c, 0:C_DK]
        qs_ref[pl.ds(r, c), :] = q * _sigmoid(q)


def _hgrn_scan(ci, carry, *, upd_ref, dec_ref, sbf_ref, nc):
    st_f, st_b = carry
    cb = nc - 1 - ci
    sbf_ref[0, ci] = st_f.astype(BF16)
    sbf_ref[1, cb] = st_b.astype(BF16)
    st_f = st_f * dec_ref[ci, :, 0:C_DK] + upd_ref[ci, :, 0:C_DK]
    st_b = st_b * dec_ref[cb, :, C_DK:2 * C_DK] + upd_ref[cb, :, C_DK:2 * C_DK]
    return st_f, st_b


def _direct_pairs(reverse):
    pairs = []
    for s in range(HG_SUB):
        for h in range(2):
            sh = s // _HALF
            if not ((h > sh) if reverse else (h < sh)):
                pairs.append((s, h))
    return pairs


def _hgrn_direct_pieces(d, r0, *, bcum_ref, ckey_ref, qs_ref):
    reverse = d == 1
    t_idx = lax.broadcasted_iota(jnp.int32, (_HALF, C_DK), 0)
    b = [bcum_ref[d, pl.ds(r0 + _HALF * h, _HALF), :] for h in range(2)]
    q = [qs_ref[pl.ds(r0 + _HALF * h, _HALF), :] for h in range(2)]
    pieces = []
    for s, h in _direct_pairs(reverse):
        p = q[h] * jnp.exp2(b[h] - ckey_ref[d, pl.ds(r0 + s, 1), :])
        if h == s // _HALF:
            t0 = s - _HALF * h
            p = jnp.where((t_idx <= t0) if reverse else (t_idx >= t0), p, 0.0)
        pieces.append(p)
    rows = len(pieces) // 2 * _HALF
    lhs = jnp.concatenate(pieces, axis=0).astype(BF16)
    return jnp.concatenate([lhs[:rows], lhs[rows:]], axis=1)


def _hgrn_direct_combine(d, r0, rs_ref, row0, *, val_ref):
    pairs = _direct_pairs(d == 1)
    n_half = len(pairs) // 2
    out = [jnp.zeros((_HALF, C_DK), F32), jnp.zeros((_HALF, C_DK), F32)]
    for p_idx, (s, h) in enumerate(pairs):
        col, row = divmod(p_idx, n_half)
        rp = rs_ref[row0 + row * _HALF:row0 + (row + 1) * _HALF, col * C_DK:(col + 1) * C_DK]
        out[h] = out[h] + rp * val_ref[pl.ds(r0 + s, 1), :]
    return jnp.concatenate(out, axis=0)


def _pivot_plan(reverse):
    s, h = HG_SUB, HG_CHUNK // 2
    assert HG_CHUNK == 4 * HG_SUB
    if reverse:
        return [(0, h, h, h, h), (0, s, s, s, s), (h, s, h + s, s, h + s)]
    return [(h, h, 0, h, h - 1), (s, s, 0, s, s - 1), (h + s, s, h, s, h + s - 1)]


def _hgrn_output(ti, carry, *, val_ref, o_ref, bcum_ref, ckey_ref, qs_ref, vb_ref, sbf_ref,
                 w2_ref, pmask_ref, lhs_ref, rs_ref):
    c, sub = HG_CHUNK, HG_SUB
    n_sub = c // sub
    blk_rows = len(_direct_pairs(False)) // 2 * _HALF
    rows = [pl.multiple_of((ti * HG_PAR + u) * c, c) for u in range(HG_PAR)]
    for u, r in enumerate(rows):
        for d in range(2):
            for i in range(n_sub):
                row0 = ((u * 2 + d) * n_sub + i) * blk_rows
                lhs_ref[row0:row0 + blk_rows, :] = _hgrn_direct_pieces(
                    d, r + i * sub, bcum_ref=bcum_ref, ckey_ref=ckey_ref, qs_ref=qs_ref)
    rs_ref[...] = _dot(lhs_ref[...], w2_ref[...])
    pv_groups = []
    for g in range(HG_PAR // HG_PIV):
        q_parts, k_parts, v_parts = [], [], []
        for r in rows[g * HG_PIV:(g + 1) * HG_PIV]:
            for d in range(2):
                for q0, qn, k0, kn, prow in _pivot_plan(d == 1):
                    piv = bcum_ref[d, pl.ds(r + prow, 1), :]
                    q_parts.append(qs_ref[pl.ds(r + q0, qn), :]
                                   * jnp.exp2(bcum_ref[d, pl.ds(r + q0, qn), :] - piv))
                    k_parts.append(jnp.exp2(piv - ckey_ref[d, pl.ds(r + k0, kn), :]))
                    v_parts.append(vb_ref[pl.ds(r + k0, kn), :])
        sc = _dot_nt(jnp.concatenate(q_parts, axis=0).astype(BF16),
                     jnp.concatenate(k_parts, axis=0).astype(BF16)) * pmask_ref[...]
        pv_groups.append(_dot(sc.astype(BF16), jnp.concatenate(v_parts, axis=0)))
    for u, r in enumerate(rows):
        ci = ti * HG_PAR + u
        total = None
        for d in range(2):
            blocks = [_hgrn_direct_combine(d, r + i * sub, rs_ref,
                                           ((u * 2 + d) * n_sub + i) * blk_rows, val_ref=val_ref)
                      for i in range(n_sub)]
            pv = pv_groups[u // HG_PIV]
            off = ((u % HG_PIV) * 2 + d) * c
            for q0, qn, _, _, _ in _pivot_plan(d == 1):
                for j in range(qn // sub):
                    blocks[q0 // sub + j] = (blocks[q0 // sub + j]
                                             + pv[off + j * sub:off + (j + 1) * sub])
                off += qn
            qe = (qs_ref[pl.ds(r, c), :] * jnp.exp2(bcum_ref[d, pl.ds(r, c), :])).astype(BF16)
            o_d = _dot_nt(qe, sbf_ref[d, ci]) + jnp.concatenate(blocks, axis=0)
            total = o_d if total is None else total + o_d
        o_ref[0, pl.ds(r, c), :] = total
    return carry


def _hgrn_body(xn_ref, wq_ref, wf_ref, wb_ref, wi_ref, lbr_ref, o_ref,
               bcum_ref, ckey_ref, qs_ref, val_ref, vb_ref, upd_ref, dec_ref, sbf_ref, tri_ref,
               w2_ref, pmask_ref, lhs_ref, rs_ref, proj_ref, wcat_ref, *, depth, layer_idx):
    s = xn_ref.shape[1]
    c = HG_CHUNK
    nc = s // c
    assert nc % HG_PAR == 0 and nc % HG_PREP == 0 and HG_PAR % HG_PIV == 0

    def lower_bound(d):
        rows = lbr_ref[d * depth:(d + 1) * depth, :]
        e = jnp.exp(rows - jnp.max(rows, axis=0, keepdims=True))
        sm = e / jnp.sum(e, axis=0, keepdims=True)
        acc = sm[0:1] * 0.0
        for l in range(1, layer_idx + 1):
            acc = acc + sm[l:l + 1]
        return acc

    row = lax.broadcasted_iota(jnp.int32, (c, 3 * c), 0)
    col = lax.broadcasted_iota(jnp.int32, (c, 3 * c), 1) % c
    tri_ref[0] = (col <= row).astype(BF16)
    tri_ref[1] = (col >= row).astype(BF16)
    wr = lax.broadcasted_iota(jnp.int32, (2 * C_DK, 2 * C_DK), 0) // C_DK
    wc = lax.broadcasted_iota(jnp.int32, (2 * C_DK, 2 * C_DK), 1) // C_DK
    w2_ref[...] = (wr == wc).astype(BF16)
    n = HG_PIV * 2 * c
    pr = lax.broadcasted_iota(jnp.int32, (n, n), 0)
    pc = lax.broadcasted_iota(jnp.int32, (n, n), 1)
    entry = lambda x: jnp.where(x % c < c // 2, 0, (x % c) // HG_SUB)
    pmask_ref[...] = ((pr // c == pc // c) & (entry(pr) == entry(pc))).astype(F32)

    for g, w_ref in enumerate((wq_ref, wf_ref, wb_ref, wi_ref)):
        wcat_ref[:, g * C_DK:(g + 1) * C_DK] = w_ref[...]
    blk = HG_PREP * c

    def project(t):
        proj_ref[t * blk:(t + 1) * blk, :] = _dot(xn_ref[0, t * blk:(t + 1) * blk, :], wcat_ref[...])

    lbs = (lower_bound(0), lower_bound(1))
    project(0)
    for t in range(nc // HG_PREP):
        if t + 1 < nc // HG_PREP:
            project(t + 1)
        _hgrn_tables(t, proj_ref=proj_ref, lbs=lbs, tri_ref=tri_ref, bcum_ref=bcum_ref,
                     ckey_ref=ckey_ref, qs_ref=qs_ref, val_ref=val_ref, vb_ref=vb_ref,
                     upd_ref=upd_ref, dec_ref=dec_ref)
    zero = jnp.zeros((C_DK, C_DK), F32)
    lax.fori_loop(0, nc, functools.partial(
        _hgrn_scan, upd_ref=upd_ref, dec_ref=dec_ref, sbf_ref=sbf_ref, nc=nc), (zero, zero))
    lax.fori_loop(0, nc // HG_PAR, functools.partial(
        _hgrn_output, val_ref=val_ref, o_ref=o_ref, bcum_ref=bcum_ref, ckey_ref=ckey_ref,
        qs_ref=qs_ref, vb_ref=vb_ref, sbf_ref=sbf_ref, w2_ref=w2_ref, pmask_ref=pmask_ref,
        lhs_ref=lhs_ref, rs_ref=rs_ref), 0)


def hgrn2(xn, w_in, layer, lower_bounds, layer_idx):
    b, s, k = xn.shape
    depth = lower_bounds.shape[1]
    dk = C_DK
    nc = s // HG_CHUNK
    direct_rows = HG_PAR * 2 * (HG_CHUNK // HG_SUB) * (len(_direct_pairs(False)) // 2 * _HALF)
    pivot_rows = HG_PIV * 2 * HG_CHUNK
    wblk = lambda off: pl.BlockSpec((None, k, dk), lambda bi, h: (layer, 0, off * C_HEADS + h))
    return pl.pallas_call(
        functools.partial(_hgrn_body, depth=depth, layer_idx=layer_idx),
        grid=(b, C_HEADS),
        in_specs=[pl.BlockSpec((1, s, k), lambda bi, h: (bi, 0, 0)),
                  wblk(0), wblk(1), wblk(2), wblk(3),
                  pl.BlockSpec((2 * depth, dk), lambda bi, h: (0, h))],
        out_specs=pl.BlockSpec((1, s, dk), lambda bi, h: (bi, 0, h)),
        out_shape=jax.ShapeDtypeStruct((b, s, C_HEADS * dk), F32),
        scratch_shapes=[pltpu.VMEM((2, s, dk), F32),
                        pltpu.VMEM((2, s, dk), F32),
                        pltpu.VMEM((s, dk), F32),
                        pltpu.VMEM((s, dk), F32),
                        pltpu.VMEM((s, dk), BF16),
                        pltpu.VMEM((nc, dk, 2 * dk), F32),
                        pltpu.VMEM((nc, 1, 2 * dk), F32),
                        pltpu.VMEM((2, nc, dk, dk), BF16),
                        pltpu.VMEM((2, HG_CHUNK, 3 * HG_CHUNK), BF16),
                        pltpu.VMEM((2 * dk, 2 * dk), BF16),
                        pltpu.VMEM((pivot_rows, pivot_rows), F32),
                        pltpu.VMEM((direct_rows, 2 * dk), BF16),
                        pltpu.VMEM((direct_rows, 2 * dk), F32),
                        pltpu.VMEM((s, 4 * dk), F32),
                        pltpu.VMEM((k, 4 * dk), BF16)],
        compiler_params=_params(("parallel", "parallel"), 56),
        name="hgrn2",
    )(xn, w_in, w_in, w_in, w_in, lower_bounds.reshape(2 * depth, C_HEADS * dk))


def kernel(x, mix_norm, even_w_in, even_w_out, diff_lq1, diff_lk1, diff_lq2, diff_lk2,
           diff_subln, gmlp_ln_g, gmlp_ln_b, gmlp_w_s, gmlp_b_s, hgrn_w_in, hgrn_w_out,
           hgrn_lower_bounds, hgrn_g_norm, ffn_norm, ffn_w_gate, ffn_w_up, ffn_w_down,
           final_norm):
    b, s, d = x.shape
    depth = mix_norm.shape[0]
    m = b * s
    h = x.reshape(m, d)
    even_w_in = even_w_in.astype(BF16)
    ffn_ws = None
    for l in range(depth):
        if l % 2 == 0:
            e = l // 2
            proj = norm_proj(h, mix_norm[l], even_w_in, e).reshape(b, s, -1)
            lam_init = 0.8 - 0.6 * math.exp(-0.3 * l)
            lam_vec = jnp.stack([diff_lq1[e], diff_lk1[e], diff_lq2[e], diff_lk2[e]]).astype(F32)
            to_cast = ((ffn_w_gate, ffn_w_up, ffn_w_down, hgrn_w_in, hgrn_w_out, even_w_out)
                       if ffn_ws is None else ())
            oa, cast = diff_attention(proj, lam_vec, diff_subln[e], lam_init, to_cast)
            if to_cast:
                ffn_ws, (hgrn_w_in, hgrn_w_out, even_w_out) = cast[:3], cast[3:]
            u_block = (3 * A_HEADS * A_V_DIM) // (B_GROUPS * B_GROUP_DIM)
            ob = gmlp(proj, gmlp_ln_g[e], gmlp_ln_b[e], gmlp_w_s[e], gmlp_b_s[e], u_block)
            h = out_proj2(oa.reshape(m, -1), ob.reshape(m, -1), even_w_out, e, h)
        else:
            o = l // 2
            width = C_HEADS * C_DK
            gate, xn = norm_proj(h, mix_norm[l], hgrn_w_in, o, col0=4 * width, emit_norm=True)
            scan = hgrn2(xn.reshape(b, s, d), hgrn_w_in, o, hgrn_lower_bounds, l)
            h = out_proj_gated(scan.reshape(m, -1), gate, 0, hgrn_g_norm[o], hgrn_w_out, o, h)
        last = l == depth - 1
        w_gate, w_up, w_down = ffn_ws
        h = ffn(h, ffn_norm[l], w_gate, w_up, w_down, l,
                final_gain=final_norm if last else None)
    return h.reshape(b, s, d)
```
